```python
import math
import jax, jax.numpy as jnp
from jax import lax
import numpy as np

D_MODEL = 2048
BATCH = 1
SEQ = 16384
DEPTH = 1
DEC_BATCH = 32
DEC_SEQ = 16
PAST_LEN = 4096

CHUNK = 64
Q_BLOCK = 128
ROPE_THETA = 500000.0
EPS = 1e-6
D_PLE = 256

H_A = 8
DH_A = 64
DV_A = 2 * DH_A
W_A = H_A * DV_A
H_B = 8
DH_B = 128
W_B = H_B * DH_B
IDX_HEADS = 16
IDX_DIM = 64
TOPK_MAX = 256

SECTION_SIZES = (H_A * 2 * DH_A, H_A * 2 * DH_A, W_A, W_A,
                 W_B, W_B, W_B, W_B,
                 IDX_HEADS * IDX_DIM, IDX_DIM, IDX_HEADS,
                 2 * D_MODEL)
N_IN = sum(SECTION_SIZES)

kernel_name = 'diff_dsa_gated_hybrid_stream_step'


def rms_norm(x, g):
    xf = x.astype(jnp.float32)
    y = xf * lax.rsqrt(jnp.mean(xf * xf, axis=-1, keepdims=True) + EPS)
    return (y * g.astype(jnp.float32)).astype(x.dtype)


def partial_rope(x, pos):
    d = x.shape[-1]
    r = d // 4
    inv = ROPE_THETA ** (-jnp.arange(0, r, 2, dtype=jnp.float32) / r)
    ang = pos.astype(jnp.float32)[:, None] * inv[None, :]
    cos = jnp.cos(ang)[:, None, :]
    sin = jnp.sin(ang)[:, None, :]
    xr = x[..., :r].astype(jnp.float32)
    x1, x2 = xr[..., : r // 2], xr[..., r // 2:]
    rot = jnp.concatenate([x1 * cos - x2 * sin, x2 * cos + x1 * sin], axis=-1)
    return jnp.concatenate([rot.astype(x.dtype), x[..., r:]], axis=-1)


def chunk_end(pos):
    return (pos // CHUNK + 1) * CHUNK - 1


def layer_inputs(x, pos, ln_g, w_in, qn_a, kn_a, qn_b, kn_b, kn_i):
    B, T, _ = x.shape
    z = rms_norm(x, ln_g) @ w_in
    splits = np.cumsum(SECTION_SIZES)[:-1].tolist()
    qa, ka, va, ga, qb, kb, vb, gb, qi, ki, wi, mg = jnp.split(z, splits, axis=-1)
    qa = partial_rope(rms_norm(qa.reshape(B, T, 2 * H_A, DH_A), qn_a), pos).reshape(B, T, H_A, 2, DH_A)
    ka = partial_rope(rms_norm(ka.reshape(B, T, 2 * H_A, DH_A), kn_a), pos).reshape(B, T, H_A, 2, DH_A)
    va = va.reshape(B, T, H_A, DV_A)
    qb = partial_rope(rms_norm(qb.reshape(B, T, H_B, DH_B), qn_b), pos)
    kb = partial_rope(rms_norm(kb.reshape(B, T, H_B, DH_B), kn_b), pos)
    vb = vb.reshape(B, T, H_B, DH_B)
    qi = partial_rope(qi.reshape(B, T, IDX_HEADS, IDX_DIM), pos)
    ki = partial_rope(rms_norm(ki, kn_i)[:, :, None, :], pos)[:, :, 0, :]
    wi = wi * (IDX_HEADS ** -0.5)
    return qa, ka, va, ga, qb, kb, vb, gb, qi, ki, wi, mg


def diff_attend(q, k, v, q_pos, k_pos, lam, lam_init, sub_g):
    s = jnp.einsum('bqhcd,bkhcd->bhcqk', q, k, preferred_element_type=jnp.float32) * (DH_A ** -0.5)
    vis = k_pos[None, :] <= chunk_end(q_pos)[:, None]
    p = jax.nn.softmax(jnp.where(vis, s, -jnp.inf), axis=-1)
    a = p[:, :, 0] - lam * p[:, :, 1]
    o = jnp.einsum('bhqk,bkhd->bqhd', a.astype(v.dtype), v)
    return rms_norm(o, sub_g) * (1.0 - lam_init)


def dsa_attend(q, qi, wi, k, v, ki, q_pos, k_pos, topk):
    logits = jnp.einsum('bqhd,bkd->bqhk', qi, ki, preferred_element_type=jnp.float32) * (IDX_DIM ** -0.5)
    score = jnp.einsum('bqh,bqhk->bqk', wi.astype(jnp.float32), jax.nn.relu(logits))
    q_end = chunk_end(q_pos)
    vis = k_pos[None, :] <= q_end[:, None]
    _, idx = lax.top_k(jnp.where(vis, score, -jnp.inf), topk)
    valid = k_pos[idx] <= q_end[None, :, None]
    gather = jax.vmap(lambda a, i: a[i])
    k_sel = gather(k, idx)
    v_sel = gather(v, idx)
    s = jnp.einsum('bqhd,bqjhd->bhqj', q, k_sel, preferred_element_type=jnp.float32) * (DH_B ** -0.5)
    p = jax.nn.softmax(jnp.where(valid[:, None], s, -jnp.inf), axis=-1)
    return jnp.einsum('bhqj,bqjhd->bqhd', p.astype(v.dtype), v_sel)


def layer_output(x, p, oa, ga, ob, gb, mg, w_ba, w_bb, w_out, ple_g, w_pg, w_ple):
    B, T, _ = x.shape
    ya = (oa.reshape(B, T, W_A) * jax.nn.silu(ga)) @ w_ba
    yb = (ob.reshape(B, T, W_B) * jax.nn.silu(gb)) @ w_bb
    ma, mb = jnp.split(jax.nn.sigmoid(mg), 2, axis=-1)
    h = x + (ma * ya + mb * yb) @ w_out
    gate = jax.nn.sigmoid(rms_norm(h, ple_g) @ w_pg)
    return h + gate * (p @ w_ple)


def unblock(a):
    a = jnp.moveaxis(a, 0, 1)
    return a.reshape((a.shape[0], a.shape[1] * a.shape[2]) + a.shape[3:])


def setup_inputs(seed: int = 0) -> dict:
    key = jax.random.key(seed)
    ks = iter(jax.random.split(key, 32))

    def nrm(shape, scale=1.0):
        return scale * jax.random.normal(next(ks), shape, jnp.float32)

    def gain(shape):
        return 1.0 + 0.01 * nrm(shape)

    return {
        'x_prompt': nrm((BATCH, SEQ, D_MODEL)),
        'x_sample': nrm((DEC_BATCH, DEC_SEQ, D_MODEL)),
        'p_prompt': nrm((DEPTH, BATCH, SEQ, D_PLE)),
        'p_sample': nrm((DEPTH, DEC_BATCH, DEC_SEQ, D_PLE)),
        'cache_diff_k': nrm((DEPTH, DEC_BATCH, PAST_LEN, H_A, 2, DH_A)),
        'cache_diff_v': nrm((DEPTH, DEC_BATCH, PAST_LEN, H_A, DV_A)),
        'cache_dsa_k': nrm((DEPTH, DEC_BATCH, PAST_LEN, H_B, DH_B)),
        'cache_dsa_v': nrm((DEPTH, DEC_BATCH, PAST_LEN, H_B, DH_B)),
        'cache_idx_k': nrm((DEPTH, DEC_BATCH, PAST_LEN, IDX_DIM)),
        'ln_g': gain((DEPTH, D_MODEL)),
        'w_in': nrm((DEPTH, D_MODEL, N_IN), D_MODEL ** -0.5),
        'q_norm_a': gain((DEPTH, DH_A)),
        'k_norm_a': gain((DEPTH, DH_A)),
        'lam_q1': nrm((DEPTH, DH_A), 0.1),
        'lam_k1': nrm((DEPTH, DH_A), 0.1),
        'lam_q2': nrm((DEPTH, DH_A), 0.1),
        'lam_k2': nrm((DEPTH, DH_A), 0.1),
        'subln_a': gain((DEPTH, DV_A)),
        'q_norm_b': gain((DEPTH, DH_B)),
        'k_norm_b': gain((DEPTH, DH_B)),
        'k_norm_idx': gain((DEPTH, IDX_DIM)),
        'w_branch_a': nrm((DEPTH, W_A, D_MODEL), W_A ** -0.5),
        'w_branch_b': nrm((DEPTH, W_B, D_MODEL), W_B ** -0.5),
        'w_out': nrm((DEPTH, D_MODEL, D_MODEL), D_MODEL ** -0.5),
        'ple_norm': gain((DEPTH, D_MODEL)),
        'w_ple_gate': nrm((DEPTH, D_MODEL, D_MODEL), D_MODEL ** -0.5),
        'w_ple': nrm((DEPTH, D_PLE, D_MODEL), D_PLE ** -0.5),
    }


def reference(x_prompt, x_sample, p_prompt, p_sample, cache_diff_k, cache_diff_v, cache_dsa_k, cache_dsa_v,
              cache_idx_k, ln_g, w_in, q_norm_a, k_norm_a, lam_q1, lam_k1, lam_q2, lam_k2, subln_a,
              q_norm_b, k_norm_b, k_norm_idx, w_branch_a, w_branch_b, w_out, ple_norm, w_ple_gate, w_ple):
    t_p = x_prompt.shape[1]
    t_s = x_sample.shape[1]
    past = cache_diff_k.shape[2]
    n_blocks = t_p // Q_BLOCK
    topk_p = min(TOPK_MAX, t_p // 4)
    topk_s = min(TOPK_MAX, (past + t_s) // 4)
    pos_p = jnp.arange(t_p)
    pos_s = past + jnp.arange(t_s)
    kpos_s = jnp.arange(past + t_s)

    hp, hs = x_prompt, x_sample
    dkp, dvp, skp, svp, ikp = [], [], [], [], []
    dks, dvs, sks, svs, iks = [], [], [], [], []
    for l in range(DEPTH):
        lam_init = 0.8 - 0.6 * math.exp(-0.3 * l)
        lam = (jnp.exp(jnp.sum(lam_q1[l].astype(jnp.float32) * lam_k1[l].astype(jnp.float32)))
               - jnp.exp(jnp.sum(lam_q2[l].astype(jnp.float32) * lam_k2[l].astype(jnp.float32))) + lam_init)
        norms = (ln_g[l], w_in[l], q_norm_a[l], k_norm_a[l], q_norm_b[l], k_norm_b[l], k_norm_idx[l])
        outs = (w_branch_a[l], w_branch_b[l], w_out[l], ple_norm[l], w_ple_gate[l], w_ple[l])

        qa, ka, va, ga, qb, kb, vb, gb, qi, ki, wi, mg = layer_inputs(hp, pos_p, *norms)

        def block(i, qa=qa, ka=ka, va=va, qb=qb, kb=kb, vb=vb, qi=qi, ki=ki, wi=wi, lam=lam,
                  lam_init=lam_init, sg=subln_a[l]):
            s0 = i * Q_BLOCK
            qpos = s0 + jnp.arange(Q_BLOCK)
            sl = lambda a: lax.dynamic_slice_in_dim(a, s0, Q_BLOCK, axis=1)
            oa_b = diff_attend(sl(qa), ka, va, qpos, pos_p, lam, lam_init, sg)
            ob_b = dsa_attend(sl(qb), sl(qi), sl(wi), kb, vb, ki, qpos, pos_p, topk_p)
            return oa_b, ob_b

        oa, ob = lax.map(block, jnp.arange(n_blocks))
        hp = layer_output(hp, p_prompt[l], unblock(oa), ga, unblock(ob), gb, mg, *outs)
        dkp.append(ka); dvp.append(va); skp.append(kb); svp.append(vb); ikp.append(ki)

        qa, ka, va, ga, qb, kb, vb, gb, qi, ki, wi, mg = layer_inputs(hs, pos_s, *norms)
        ka_all = jnp.concatenate([cache_diff_k[l].astype(ka.dtype), ka], axis=1)
        va_all = jnp.concatenate([cache_diff_v[l].astype(va.dtype), va], axis=1)
        kb_all = jnp.concatenate([cache_dsa_k[l].astype(kb.dtype), kb], axis=1)
        vb_all = jnp.concatenate([cache_dsa_v[l].astype(vb.dtype), vb], axis=1)
        ki_all = jnp.concatenate([cache_idx_k[l].astype(ki.dtype), ki], axis=1)
        oa_s = diff_attend(qa, ka_all, va_all, pos_s, kpos_s, lam, lam_init, subln_a[l])
        ob_s = dsa_attend(qb, qi, wi, kb_all, vb_all, ki_all, pos_s, kpos_s, topk_s)
        hs = layer_output(hs, p_sample[l], oa_s, ga, ob_s, gb, mg, *outs)
        dks.append(ka); dvs.append(va); sks.append(kb); svs.append(vb); iks.append(ki)

    return (hp, hs,
            jnp.stack(dkp), jnp.stack(dvp), jnp.stack(skp), jnp.stack(svp), jnp.stack(ikp),
            jnp.stack(dks), jnp.stack(dvs), jnp.stack(sks), jnp.stack(svs), jnp.stack(iks))
```

```python
import functools
import math

import jax
import jax.numpy as jnp
from jax import lax
from jax.experimental import pallas as pl
from jax.experimental.pallas import tpu as pltpu

CHUNK = 64
ROPE_THETA = 500000.0
EPS = 1e-6
H_A = 8
DH_A = 64
DV_A = 128
H_B = 8
DH_B = 128
IDX_HEADS = 16
IDX_DIM = 64
TOPK_MAX = 256
LANES = 128
GROUPS = 8
SEC = GROUPS * LANES
NEG = -1e30
INT_MIN = -(2 ** 31)
INT_MAX = 2 ** 31 - 1
VMEM_LIMIT = 56 * 1024 * 1024

_NT = (((1,), (1,)), ((), ()))


def _cdiv(a, b):
    return (a + b - 1) // b


def _params(sem, vmem=VMEM_LIMIT):
    return pltpu.CompilerParams(dimension_semantics=sem, vmem_limit_bytes=vmem)


def _rmsnorm_kernel(x_ref, g_ref, o_ref):
    x = x_ref[...]
    ms = jnp.mean(x * x, axis=-1, keepdims=True)
    o_ref[...] = (x * lax.rsqrt(ms + EPS) * g_ref[...]).astype(o_ref.dtype)


def _rmsnorm(x, g, tm):
    r, d = x.shape
    return pl.pallas_call(
        _rmsnorm_kernel,
        grid=(r // tm,),
        in_specs=[pl.BlockSpec((tm, d), lambda i: (i, 0)),
                  pl.BlockSpec((1, d), lambda i: (0, 0))],
        out_specs=pl.BlockSpec((tm, d), lambda i: (i, 0)),
        out_shape=jax.ShapeDtypeStruct((r, d), jnp.bfloat16),
        compiler_params=_params(("parallel",)),
        name="rmsnorm",
    )(x, g.reshape(1, d))


def _head_norm(z, g, d):
    sq = z * z
    if d == LANES:
        r = lax.rsqrt(jnp.sum(sq, axis=-1, keepdims=True) * (1.0 / d) + EPS)
    else:
        lo = lax.broadcasted_iota(jnp.int32, z.shape, 1) < d
        s_lo = jnp.sum(jnp.where(lo, sq, 0.0), axis=-1, keepdims=True)
        s_hi = jnp.sum(jnp.where(lo, 0.0, sq), axis=-1, keepdims=True)
        r = jnp.where(lo, lax.rsqrt(s_lo * (1.0 / d) + EPS), lax.rsqrt(s_hi * (1.0 / d) + EPS))
    return z * r * g


def _rope(z, c, a, b, half):
    return z * c + pltpu.roll(z, LANES - half, 1) * a + pltpu.roll(z, half, 1) * b


def _proj_kernel(xn_ref, w_ref, g_ref, rc_ref, ra_ref, rb_ref, *out_refs, norm, rope, scale, kinds):
    z = jnp.dot(xn_ref[...], w_ref[...], preferred_element_type=jnp.float32)
    for c in range(GROUPS):
        zc = z[:, c * LANES:(c + 1) * LANES]
        if norm:
            zc = _head_norm(zc, g_ref[...], norm)
        if rope:
            zc = _rope(zc, rc_ref[...], ra_ref[...], rb_ref[...], rope // 8)
        for kind, o_ref in zip(kinds, out_refs):
            if kind == "f32":
                o_ref[:, c * LANES:(c + 1) * LANES] = zc
            else:
                o_ref[c] = (zc * scale).astype(o_ref.dtype)


def _proj(xn, w, gain, tables, *, norm, rope, scale, kinds, tm):
    r, d = xn.shape
    nj = w.shape[1] // SEC
    rc, ra, rb = tables
    out_shape, out_specs = [], []
    for kind in kinds:
        if kind == "f32":
            out_shape.append(jax.ShapeDtypeStruct((r, nj * SEC), jnp.float32))
            out_specs.append(pl.BlockSpec((tm, SEC), lambda j, i: (i, j)))
        else:
            assert nj == 1
            out_shape.append(jax.ShapeDtypeStruct((GROUPS, r, LANES), jnp.bfloat16))
            out_specs.append(pl.BlockSpec((GROUPS, tm, LANES), lambda j, i: (0, i, 0)))
    row = pl.BlockSpec((tm, LANES), lambda j, i: (i, 0))
    outs = pl.pallas_call(
        functools.partial(_proj_kernel, norm=norm, rope=rope, scale=scale, kinds=tuple(kinds)),
        grid=(nj, r // tm),
        in_specs=[pl.BlockSpec((tm, d), lambda j, i: (i, 0)),
                  pl.BlockSpec((d, SEC), lambda j, i: (0, j)),
                  pl.BlockSpec((1, LANES), lambda j, i: (0, 0)),
                  row, row, row],
        out_specs=out_specs,
        out_shape=out_shape,
        compiler_params=_params(("parallel", "parallel")),
        name="proj",
    )(xn, w, gain, rc, ra, rb)
    return outs


def _small_proj_kernel(xn_ref, w_ref, g_ref, rc_ref, ra_ref, rb_ref, f_ref, k2_ref, *, wscale):
    z = jnp.dot(xn_ref[...], w_ref[...], preferred_element_type=jnp.float32)
    lo = lax.broadcasted_iota(jnp.int32, z.shape, 1) < IDX_DIM
    ms = jnp.sum(jnp.where(lo, z * z, 0.0), axis=-1, keepdims=True) * (1.0 / IDX_DIM)
    zn = z * lax.rsqrt(ms + EPS) * g_ref[...]
    zr = _rope(zn, rc_ref[...], ra_ref[...], rb_ref[...], IDX_DIM // 8)
    f_ref[...] = jnp.where(lo, zr, z * wscale)
    k2_ref[...] = jnp.where(lo, zr, pltpu.roll(zr, IDX_DIM, 1)).astype(k2_ref.dtype)


def _small_proj(xn, w, gain, tables, tm):
    r, d = xn.shape
    rc, ra, rb = tables
    row = pl.BlockSpec((tm, LANES), lambda i: (i, 0))
    return pl.pallas_call(
        functools.partial(_small_proj_kernel, wscale=IDX_HEADS ** -0.5),
        grid=(r // tm,),
        in_specs=[pl.BlockSpec((tm, d), lambda i: (i, 0)),
                  pl.BlockSpec((d, LANES), lambda i: (0, 0)),
                  pl.BlockSpec((1, LANES), lambda i: (0, 0)),
                  row, row, row],
        out_specs=[row, row],
        out_shape=[jax.ShapeDtypeStruct((r, LANES), jnp.float32),
                   jax.ShapeDtypeStruct((r, LANES), jnp.bfloat16)],
        compiler_params=_params(("parallel",)),
        name="small_proj",
    )(xn, w, gain, rc, ra, rb)


def _rope_tables(pos, d):
    r = d // 4
    h = r // 2
    n = pos.shape[0]
    inv = ROPE_THETA ** (-jnp.arange(0, r, 2, dtype=jnp.float32) / r)
    ang = pos.astype(jnp.float32)[:, None] * inv[None, :]
    cos, sin = jnp.cos(ang), jnp.sin(ang)
    zh = jnp.zeros((n, h), jnp.float32)
    zr = jnp.zeros((n, d - r), jnp.float32)
    c = jnp.concatenate([cos, cos, jnp.ones((n, d - r), jnp.float32)], axis=-1)
    a = jnp.concatenate([-sin, zh, zr], axis=-1)
    b = jnp.concatenate([zh, sin, zr], axis=-1)
    rep = LANES // d
    return tuple(jnp.tile(t, (1, rep)) for t in (c, a, b))


def _online_update(s, v, m_ref, l_ref, acc_ref):
    m_prev = m_ref[...]
    m_new = jnp.maximum(m_prev, jnp.max(s, axis=-1, keepdims=True))
    alpha = jnp.exp(m_prev - m_new)
    p = jnp.exp(s - m_new)
    l_ref[...] = alpha * l_ref[...] + jnp.sum(p, axis=-1, keepdims=True)
    acc_ref[...] = alpha * acc_ref[...] + jnp.dot(p.astype(v.dtype), v, preferred_element_type=jnp.float32)
    m_ref[...] = m_new


def _lambda(lam_ref, lam_init):
    lp = lam_ref[...]
    s1 = jnp.sum(lp[0:1] * lp[1:2], axis=-1, keepdims=True)
    s2 = jnp.sum(lp[2:3] * lp[3:4], axis=-1, keepdims=True)
    return jnp.exp(s1) - jnp.exp(s2) + lam_init


def _diff_finish(o0, o1, lam, sg, lam_init):
    o = o0 - lam * o1
    ms = jnp.mean(o * o, axis=-1, keepdims=True)
    return o * lax.rsqrt(ms + EPS) * sg * (1.0 - lam_init)


def _chunk_end(pos):
    return (pos // CHUNK + 1) * CHUNK - 1


def _diff_attn_kernel(q_ref, k_ref, v_ref, lam_ref, sg_ref, o_ref, qs_ref, m_ref, l_ref, acc_ref,
                      *, tq, tk, lam_init):
    qi = pl.program_id(1)
    kj = pl.program_id(2)
    nk = _cdiv((qi + 1) * tq, tk)

    @pl.when(kj == 0)
    def _():
        q = q_ref[...]
        lo = lax.broadcasted_iota(jnp.int32, q.shape, 1) < DH_A
        zero = jnp.zeros_like(q)
        qs_ref[0:tq, :] = jnp.where(lo, q, zero)
        qs_ref[tq:2 * tq, :] = jnp.where(lo, zero, q)
        m_ref[...] = jnp.full(m_ref.shape, NEG, jnp.float32)
        l_ref[...] = jnp.zeros(l_ref.shape, jnp.float32)
        acc_ref[...] = jnp.zeros(acc_ref.shape, jnp.float32)

    def step(masked):
        s = lax.dot_general(qs_ref[...], k_ref[...], _NT, preferred_element_type=jnp.float32)
        if masked:
            row = lax.broadcasted_iota(jnp.int32, s.shape, 0)
            row = jnp.where(row >= tq, row - tq, row)
            kpos = kj * tk + lax.broadcasted_iota(jnp.int32, s.shape, 1)
            s = jnp.where(kpos <= _chunk_end(qi * tq + row), s, NEG)
        _online_update(s, v_ref[...], m_ref, l_ref, acc_ref)

    full = (kj + 1) * tk <= qi * tq + CHUNK

    @pl.when(jnp.logical_and(kj < nk, full))
    def _():
        step(False)

    @pl.when(jnp.logical_and(kj < nk, jnp.logical_not(full)))
    def _():
        step(True)

    @pl.when(kj == nk - 1)
    def _():
        o = acc_ref[...] / l_ref[...]
        o_ref[...] = _diff_finish(o[0:tq], o[tq:2 * tq], _lambda(lam_ref, lam_init), sg_ref[...], lam_init)


def _diff_attn(q, k, v, lam_p, sg, lam_init, tq, tk):
    _, t, _ = q.shape
    nkv = lambda qi: _cdiv((qi + 1) * tq, tk)
    kv_spec = pl.BlockSpec((None, tk, LANES), lambda h, qi, kj: (h, jnp.minimum(kj, nkv(qi) - 1), 0))
    return pl.pallas_call(
        functools.partial(_diff_attn_kernel, tq=tq, tk=tk, lam_init=lam_init),
        grid=(H_A, t // tq, t // tk),
        in_specs=[pl.BlockSpec((None, tq, LANES), lambda h, qi, kj: (h, qi, 0)),
                  kv_spec, kv_spec,
                  pl.BlockSpec((4, DH_A), lambda h, qi, kj: (0, 0)),
                  pl.BlockSpec((1, DV_A), lambda h, qi, kj: (0, 0))],
        out_specs=pl.BlockSpec((tq, LANES), lambda h, qi, kj: (qi, h)),
        out_shape=jax.ShapeDtypeStruct((t, H_A * DV_A), jnp.float32),
        scratch_shapes=[pltpu.VMEM((2 * tq, LANES), jnp.bfloat16),
                        pltpu.VMEM((2 * tq, 1), jnp.float32),
                        pltpu.VMEM((2 * tq, 1), jnp.float32),
                        pltpu.VMEM((2 * tq, DV_A), jnp.float32)],
        compiler_params=_params(("parallel", "parallel", "arbitrary")),
        name="diff_attn",
    )(q, k, v, lam_p, sg)


def _sort_key(score):
    bits = pltpu.bitcast(score, jnp.int32)
    key = jnp.where(bits < 0, bits ^ INT_MAX, bits)
    return jnp.where(key == -1, 0, key)


def _index_kernel(q_ref, w_ref, k_ref, bias_ref, qs_ref, key_ref, p_ref, *, tq, tk, t, topk):
    qi = pl.program_id(0)
    kj = pl.program_id(1)
    nk = _cdiv((qi + 1) * tq, tk)
    nsub = tk // LANES

    @pl.when(kj == 0)
    def _():
        for p in range(GROUPS):
            q = q_ref[p]
            lo = lax.broadcasted_iota(jnp.int32, q.shape, 1) < IDX_DIM
            zero = jnp.zeros_like(q)
            qs_ref[(2 * p) * tq:(2 * p + 1) * tq, :] = jnp.where(lo, q, zero)
            qs_ref[(2 * p + 1) * tq:(2 * p + 2) * tq, :] = jnp.where(lo, zero, q)

    @pl.when(kj < nk)
    def _():
        logits = lax.dot_general(qs_ref[...], k_ref[...], _NT, preferred_element_type=jnp.float32)
        w = w_ref[...]
        score = None
        for h in range(IDX_HEADS):
            term = jnp.maximum(logits[h * tq:(h + 1) * tq], 0.0) * w[:, h:h + 1]
            score = term if score is None else score + term
        qpos = qi * tq + lax.broadcasted_iota(jnp.int32, score.shape, 0)
        kpos = kj * tk + lax.broadcasted_iota(jnp.int32, score.shape, 1)
        key = jnp.where(kpos <= _chunk_end(qpos), _sort_key(score), INT_MIN)
        key_ref[:, pl.ds(pl.multiple_of(kj * tk, tk), tk)] = key

    @pl.when(kj == nk - 1)
    def _():
        lane = lax.broadcasted_iota(jnp.int32, (tq, LANES), 1)
        cend = _chunk_end(qi * tq + lax.broadcasted_iota(jnp.int32, (tq, LANES), 0))

        def count(pred):
            def body(j, acc):
                for c in range(nsub):
                    off = pl.multiple_of(j * tk + c * LANES, LANES)
                    acc = acc + jnp.where(pred(key_ref[:, pl.ds(off, LANES)], off + lane), 1.0, 0.0)
                return acc
            acc = lax.fori_loop(0, nk, body, jnp.zeros((tq, LANES), jnp.float32))
            return jnp.broadcast_to(jnp.sum(acc, axis=1, keepdims=True), (tq, LANES))

        kf = float(topk)
        c0 = count(lambda kt, idx: kt >= 0)
        prefix = jnp.where(c0 >= kf, 0, INT_MIN).astype(jnp.int32)

        def bit_body(b, prefix):
            cand = prefix | lax.shift_left(jnp.int32(1), jnp.int32(30) - b)
            c = count(lambda kt, idx: kt >= cand)
            return jnp.where(c >= kf, cand, prefix)

        thr = lax.fori_loop(0, 31, bit_body, prefix)
        n_gt = count(lambda kt, idx: kt > thr)
        n_ge = count(lambda kt, idx: kt >= thr)
        need = kf - n_gt
        p_ref[...] = jnp.full((tq, LANES), INT_MAX, jnp.int32)

        @pl.when(jnp.max(n_ge) > kf)
        def _():
            nbits = max(1, (t - 1).bit_length())

            def pos_body(b, pos):
                cand = pos | lax.shift_left(jnp.int32(1), jnp.int32(nbits - 1) - b)
                g = count(lambda kt, idx: jnp.logical_and(kt == thr, idx < cand))
                return jnp.where(g < need, cand, pos)

            p_ref[...] = lax.fori_loop(0, nbits, pos_body, jnp.zeros((tq, LANES), jnp.int32))

        last = p_ref[...]

        def write_body(j, carry):
            for c in range(nsub):
                off = pl.multiple_of(j * tk + c * LANES, LANES)
                kt = key_ref[:, pl.ds(off, LANES)]
                idx = off + lane
                sel = jnp.logical_or(kt > thr, jnp.logical_and(kt == thr, idx <= last))
                sel = jnp.logical_and(sel, idx <= cend)
                bias_ref[:, pl.ds(off, LANES)] = jnp.where(sel, 0.0, NEG).astype(bias_ref.dtype)
            return carry

        lax.fori_loop(0, nk, write_body, 0)

        def fill_body(j, carry):
            off = pl.multiple_of(j * tk, tk)
            bias_ref[:, pl.ds(off, tk)] = jnp.full((tq, tk), NEG, bias_ref.dtype)
            return carry

        lax.fori_loop(nk, t // tk, fill_body, 0)


def _index_select(qi16, wi, ki2, topk, tq, tk):
    _, t, _ = qi16.shape
    nkv = lambda qi: _cdiv((qi + 1) * tq, tk)
    return pl.pallas_call(
        functools.partial(_index_kernel, tq=tq, tk=tk, t=t, topk=topk),
        grid=(t // tq, t // tk),
        in_specs=[pl.BlockSpec((GROUPS, tq, LANES), lambda qi, kj: (0, qi, 0)),
                  pl.BlockSpec((tq, IDX_HEADS), lambda qi, kj: (qi, 0)),
                  pl.BlockSpec((tk, LANES), lambda qi, kj: (jnp.minimum(kj, nkv(qi) - 1), 0))],
        out_specs=pl.BlockSpec((tq, t), lambda qi, kj: (qi, 0)),
        out_shape=jax.ShapeDtypeStruct((t, t), jnp.bfloat16),
        scratch_shapes=[pltpu.VMEM((IDX_HEADS * tq, LANES), jnp.bfloat16),
                        pltpu.VMEM((tq, t), jnp.int32),
                        pltpu.VMEM((tq, LANES), jnp.int32)],
        compiler_params=_params(("parallel", "arbitrary")),
        name="index_select",
    )(qi16, wi, ki2)


def _dsa_attn_kernel(q_ref, k_ref, v_ref, bias_ref, o_ref, m_ref, l_ref, acc_ref, *, tq, tk):
    qi = pl.program_id(0)
    kj = pl.program_id(1)
    nk = _cdiv((qi + 1) * tq, tk)

    @pl.when(kj == 0)
    def _():
        m_ref[...] = jnp.full(m_ref.shape, NEG, jnp.float32)
        l_ref[...] = jnp.zeros(l_ref.shape, jnp.float32)
        acc_ref[...] = jnp.zeros(acc_ref.shape, jnp.float32)

    @pl.when(kj < nk)
    def _():
        bias = bias_ref[...].astype(jnp.float32)
        for h in range(H_B):
            s = lax.dot_general(q_ref[h], k_ref[h], _NT, preferred_element_type=jnp.float32) + bias
            _online_update(s, v_ref[h], m_ref.at[h], l_ref.at[h], acc_ref.at[h])

    @pl.when(kj == nk - 1)
    def _():
        for h in range(H_B):
            o_ref[:, h * LANES:(h + 1) * LANES] = acc_ref[h] / l_ref[h]


def _dsa_attn(q, k, v, bias, tq, tk):
    _, t, _ = q.shape
    nkv = lambda qi: _cdiv((qi + 1) * tq, tk)
    kv_spec = pl.BlockSpec((H_B, tk, LANES), lambda qi, kj: (0, jnp.minimum(kj, nkv(qi) - 1), 0))
    return pl.pallas_call(
        functools.partial(_dsa_attn_kernel, tq=tq, tk=tk),
        grid=(t // tq, t // tk),
        in_specs=[pl.BlockSpec((H_B, tq, LANES), lambda qi, kj: (0, qi, 0)),
                  kv_spec, kv_spec,
                  pl.BlockSpec((tq, tk), lambda qi, kj: (qi, jnp.minimum(kj, nkv(qi) - 1)))],
        out_specs=pl.BlockSpec((tq, H_B * DH_B), lambda qi, kj: (qi, 0)),
        out_shape=jax.ShapeDtypeStruct((t, H_B * DH_B), jnp.float32),
        scratch_shapes=[pltpu.VMEM((H_B, tq, 1), jnp.float32),
                        pltpu.VMEM((H_B, tq, 1), jnp.float32),
                        pltpu.VMEM((H_B, tq, DH_B), jnp.float32)],
        compiler_params=_params(("parallel", "arbitrary")),
        name="dsa_attn",
    )(q, k, v, bias)


def _sample_kernel(qa_ref, qb_ref, qi_ref, wi_ref, ckd_ref, cvd_ref, ckb_ref, cvb_ref, cki_ref,
                   nkd_ref, nvd_ref, nkb_ref, nvb_ref, nki_ref, lam_ref, sg_ref,
                   oa_ref, ob_ref,
                   qda_ref, qdb_ref, bias_ref, key_ref, ma_ref, la_ref, acca_ref, mb_ref, lb_ref, accb_ref,
                   *, ts, past, tk, topk, lam_init):
    kj = pl.program_id(1)
    nkt = past // tk
    width = past + LANES
    ra = 2 * H_A * ts
    rb = H_B * ts

    def new_key_valid(rows):
        t_q = lax.broadcasted_iota(jnp.int32, (rows, LANES), 0) % ts
        lane = lax.broadcasted_iota(jnp.int32, (rows, LANES), 1)
        return jnp.logical_and(lane < ts, past + lane <= _chunk_end(past + t_q))

    @pl.when(kj == 0)
    def _():
        qa = qa_ref[...]
        grp = lax.broadcasted_iota(jnp.int32, qa.shape, 1) // DH_A
        for hc in range(2 * H_A):
            qda_ref[hc * ts:(hc + 1) * ts, :] = jnp.where(grp == hc, qa, jnp.zeros_like(qa))
        qb = qb_ref[...]
        grp = lax.broadcasted_iota(jnp.int32, qb.shape, 1) // DH_B
        for h in range(H_B):
            qdb_ref[h * ts:(h + 1) * ts, :] = jnp.where(grp == h, qb, jnp.zeros_like(qb))
        for ref in (ma_ref, mb_ref):
            ref[...] = jnp.full(ref.shape, NEG, jnp.float32)
        for ref in (la_ref, lb_ref, acca_ref, accb_ref):
            ref[...] = jnp.zeros(ref.shape, jnp.float32)

        qi = qi_ref[...]
        w = wi_ref[...]

        def scores(keys16):
            logits = lax.dot_general(qi, keys16, _NT, preferred_element_type=jnp.float32)
            score = None
            for h in range(IDX_HEADS):
                term = jnp.maximum(logits[h * ts:(h + 1) * ts], 0.0) * w[h * ts:(h + 1) * ts]
                score = term if score is None else score + term
            return score

        step = min(past, 1024)
        for c in range(past // step):
            keys16 = cki_ref[c * step:(c + 1) * step, :].astype(jnp.bfloat16)
            key_ref[:, c * step:(c + 1) * step] = _sort_key(scores(keys16))
        key_ref[:, past:width] = jnp.where(new_key_valid(ts), _sort_key(scores(nki_ref[...])), INT_MIN)

        kf = float(topk)

        def count(pred):
            idx = lax.broadcasted_iota(jnp.int32, (ts, width), 1)
            c = jnp.sum(jnp.where(pred(key_ref[...], idx), 1.0, 0.0), axis=1, keepdims=True)
            return jnp.broadcast_to(c, (ts, LANES))

        def wide(x):
            return x[:, 0:1]

        c0 = count(lambda kt, idx: kt >= 0)
        prefix = jnp.where(c0 >= kf, 0, INT_MIN).astype(jnp.int32)

        def bit_body(b, prefix):
            cand = prefix | lax.shift_left(jnp.int32(1), jnp.int32(30) - b)
            c = count(lambda kt, idx: kt >= wide(cand))
            return jnp.where(c >= kf, cand, prefix)

        thr = lax.fori_loop(0, 31, bit_body, prefix)
        n_gt = count(lambda kt, idx: kt > wide(thr))
        need = kf - n_gt
        nbits = max(1, (width - 1).bit_length())

        def pos_body(b, pos):
            cand = pos | lax.shift_left(jnp.int32(1), jnp.int32(nbits - 1) - b)
            g = count(lambda kt, idx: jnp.logical_and(kt == wide(thr), idx < wide(cand)))
            return jnp.where(g < need, cand, pos)

        last = lax.fori_loop(0, nbits, pos_body, jnp.zeros((ts, LANES), jnp.int32))
        kt = key_ref[...]
        idx = lax.broadcasted_iota(jnp.int32, (ts, width), 1)
        sel = jnp.logical_or(kt > wide(thr), jnp.logical_and(kt == wide(thr), idx <= wide(last)))
        sel = jnp.logical_and(sel, kt != INT_MIN)
        bias = jnp.where(sel, 0.0, NEG)
        for h in range(H_B):
            bias_ref[h * ts:(h + 1) * ts, :] = bias

    sa = lax.dot_general(qda_ref[...], ckd_ref[...].astype(jnp.bfloat16), _NT, preferred_element_type=jnp.float32)
    _online_update(sa, cvd_ref[...].astype(jnp.bfloat16), ma_ref, la_ref, acca_ref)
    sb = lax.dot_general(qdb_ref[...], ckb_ref[...].astype(jnp.bfloat16), _NT, preferred_element_type=jnp.float32)
    sb = sb + bias_ref[:, pl.ds(pl.multiple_of(kj * tk, tk), tk)]
    _online_update(sb, cvb_ref[...].astype(jnp.bfloat16), mb_ref, lb_ref, accb_ref)

    @pl.when(kj == nkt - 1)
    def _():
        sa = lax.dot_general(qda_ref[...], nkd_ref[...], _NT, preferred_element_type=jnp.float32)
        sa = jnp.where(new_key_valid(ra), sa, NEG)
        _online_update(sa, nvd_ref[...], ma_ref, la_ref, acca_ref)
        sb = lax.dot_general(qdb_ref[...], nkb_ref[...], _NT, preferred_element_type=jnp.float32)
        sb = sb + bias_ref[:, past:width]
        _online_update(sb, nvb_ref[...], mb_ref, lb_ref, accb_ref)

        lam = _lambda(lam_ref, lam_init)
        oa = acca_ref[...] / la_ref[...]
        ob = accb_ref[...] / lb_ref[...]
        for h in range(H_A):
            cols = slice(h * DV_A, (h + 1) * DV_A)
            o0 = oa[(2 * h) * ts:(2 * h + 1) * ts, cols]
            o1 = oa[(2 * h + 1) * ts:(2 * h + 2) * ts, cols]
            oa_ref[:, cols] = _diff_finish(o0, o1, lam, sg_ref[...], lam_init)
        for h in range(H_B):
            cols = slice(h * DH_B, (h + 1) * DH_B)
            ob_ref[:, cols] = ob[h * ts:(h + 1) * ts, cols]


def _sample_attn(l, qa, qb, qi, wi, caches, news, lam_p, sg, lam_init, topk, tk):
    bsz, ts, _ = qa.shape
    ckd, cvd, ckb, cvb, cki = caches
    past = ckd.shape[2]
    width = past + LANES
    ra, rb = 2 * H_A * ts, H_B * ts
    cache_spec = pl.BlockSpec((None, None, tk, SEC), lambda b, kj: (l, b, kj, 0))
    new_spec = pl.BlockSpec((None, LANES, SEC), lambda b, kj: (b, 0, 0))
    row_spec = pl.BlockSpec((None, ts, SEC), lambda b, kj: (b, 0, 0))
    return pl.pallas_call(
        functools.partial(_sample_kernel, ts=ts, past=past, tk=tk, topk=topk, lam_init=lam_init),
        grid=(bsz, past // tk),
        in_specs=[row_spec, row_spec,
                  pl.BlockSpec((None, IDX_HEADS * ts, IDX_DIM), lambda b, kj: (b, 0, 0)),
                  pl.BlockSpec((None, IDX_HEADS * ts, 1), lambda b, kj: (b, 0, 0)),
                  cache_spec, cache_spec, cache_spec, cache_spec,
                  pl.BlockSpec((None, None, past, IDX_DIM), lambda b, kj: (l, b, 0, 0)),
                  new_spec, new_spec, new_spec, new_spec,
                  pl.BlockSpec((None, LANES, IDX_DIM), lambda b, kj: (b, 0, 0)),
                  pl.BlockSpec((4, DH_A), lambda b, kj: (0, 0)),
                  pl.BlockSpec((1, DV_A), lambda b, kj: (0, 0))],
        out_specs=[row_spec, row_spec],
        out_shape=[jax.ShapeDtypeStruct((bsz, ts, SEC), jnp.float32),
                   jax.ShapeDtypeStruct((bsz, ts, SEC), jnp.float32)],
        scratch_shapes=[pltpu.VMEM((ra, SEC), jnp.bfloat16),
                        pltpu.VMEM((rb, SEC), jnp.bfloat16),
                        pltpu.VMEM((rb, width), jnp.float32),
                        pltpu.VMEM((ts, width), jnp.int32),
                        pltpu.VMEM((ra, 1), jnp.float32),
                        pltpu.VMEM((ra, 1), jnp.float32),
                        pltpu.VMEM((ra, SEC), jnp.float32),
                        pltpu.VMEM((rb, 1), jnp.float32),
                        pltpu.VMEM((rb, 1), jnp.float32),
                        pltpu.VMEM((rb, SEC), jnp.float32)],
        compiler_params=_params(("parallel", "arbitrary")),
        name="sample_attn",
    )(qa, qb, qi, wi, ckd, cvd, ckb, cvb, cki, *news, lam_p, sg)


def _silu(x):
    return x * jax.nn.sigmoid(x)


def _merge_kernel(oa_ref, ga_ref, ob_ref, gb_ref, mg_ref, wa_ref, wb_ref, u_ref):
    d = u_ref.shape[-1]
    ya = jnp.dot((oa_ref[...] * _silu(ga_ref[...])).astype(jnp.bfloat16), wa_ref[...],
                 preferred_element_type=jnp.float32)
    yb = jnp.dot((ob_ref[...] * _silu(gb_ref[...])).astype(jnp.bfloat16), wb_ref[...],
                 preferred_element_type=jnp.float32)
    mg = mg_ref[...]
    u = jax.nn.sigmoid(mg[:, :d]) * ya + jax.nn.sigmoid(mg[:, d:]) * yb
    u_ref[...] = u.astype(u_ref.dtype)


def _resid_kernel(x_ref, u_ref, wo_ref, g_ref, h_ref, hn_ref):
    h = x_ref[...] + jnp.dot(u_ref[...], wo_ref[...], preferred_element_type=jnp.float32)
    h_ref[...] = h
    ms = jnp.mean(h * h, axis=-1, keepdims=True)
    hn_ref[...] = (h * lax.rsqrt(ms + EPS) * g_ref[...]).astype(hn_ref.dtype)


def _ple_kernel(h_ref, hn_ref, p_ref, wg_ref, wp_ref, o_ref):
    gate = jax.nn.sigmoid(jnp.dot(hn_ref[...], wg_ref[...], preferred_element_type=jnp.float32))
    e = jnp.dot(p_ref[...].astype(jnp.bfloat16), wp_ref[...], preferred_element_type=jnp.float32)
    o_ref[...] = h_ref[...] + gate * e


def _layer_output(x, p, oa, ga, ob, gb, mg, w_ba, w_bb, w_out, ple_g, w_pg, w_ple, tm):
    r, d = x.shape
    wa = oa.shape[1]
    dp = p.shape[1]
    rows = lambda n: pl.BlockSpec((tm, n), lambda i: (i, 0))
    full = lambda a, b: pl.BlockSpec((a, b), lambda i: (0, 0))
    u = pl.pallas_call(
        _merge_kernel,
        grid=(r // tm,),
        in_specs=[rows(wa), rows(wa), rows(wa), rows(wa), rows(2 * d), full(wa, d), full(wa, d)],
        out_specs=rows(d),
        out_shape=jax.ShapeDtypeStruct((r, d), jnp.bfloat16),
        compiler_params=_params(("parallel",)),
        name="merge",
    )(oa, ga, ob, gb, mg, w_ba, w_bb)
    h, hn = pl.pallas_call(
        _resid_kernel,
        grid=(r // tm,),
        in_specs=[rows(d), rows(d), full(d, d), full(1, d)],
        out_specs=[rows(d), rows(d)],
        out_shape=[jax.ShapeDtypeStruct((r, d), jnp.float32), jax.ShapeDtypeStruct((r, d), jnp.bfloat16)],
        compiler_params=_params(("parallel",)),
        name="resid",
    )(x, u, w_out, ple_g)
    return pl.pallas_call(
        _ple_kernel,
        grid=(r // tm,),
        in_specs=[rows(d), rows(d), rows(dp), full(d, d), full(dp, d)],
        out_specs=rows(d),
        out_shape=jax.ShapeDtypeStruct((r, d), jnp.float32),
        compiler_params=_params(("parallel",)),
        name="ple",
    )(h, hn, p, w_pg, w_ple)


def _tile(n, pref):
    t = min(n, pref)
    while n % t:
        t //= 2
    return t


def _layer_inputs(x, pos, lw, tm):
    r, _ = x.shape
    xn = _rmsnorm(x, lw["ln_g"], tm)
    t64 = _rope_tables(pos, DH_A)
    t128 = _rope_tables(pos, DH_B)
    none = t64
    one = jnp.ones((1, LANES), jnp.float32)
    w = lw["w_sec"]
    pj = functools.partial(_proj, xn, tm=tm)
    (qa16,) = pj(w["qa"], lw["qn_a"], t64, norm=DH_A, rope=DH_A, scale=DH_A ** -0.5, kinds=["hm16"])
    ka, ka16 = pj(w["ka"], lw["kn_a"], t64, norm=DH_A, rope=DH_A, scale=1.0, kinds=["f32", "hm16"])
    va, va16 = pj(w["va"], one, none, norm=0, rope=0, scale=1.0, kinds=["f32", "hm16"])
    (ga,) = pj(w["ga"], one, none, norm=0, rope=0, scale=1.0, kinds=["f32"])
    (qb16,) = pj(w["qb"], lw["qn_b"], t128, norm=DH_B, rope=DH_B, scale=DH_B ** -0.5, kinds=["hm16"])
    kb, kb16 = pj(w["kb"], lw["kn_b"], t128, norm=DH_B, rope=DH_B, scale=1.0, kinds=["f32", "hm16"])
    vb, vb16 = pj(w["vb"], one, none, norm=0, rope=0, scale=1.0, kinds=["f32", "hm16"])
    (gb,) = pj(w["gb"], one, none, norm=0, rope=0, scale=1.0, kinds=["f32"])
    (qi16,) = pj(w["qi"], one, t64, norm=0, rope=DH_A, scale=IDX_DIM ** -0.5, kinds=["hm16"])
    (mg,) = pj(w["mg"], one, none, norm=0, rope=0, scale=1.0, kinds=["f32"])
    small, ki2 = _small_proj(xn, w["small"], lw["kn_i"], t64, tm)
    ki = small[:, :IDX_DIM]
    wi = small[:, IDX_DIM:IDX_DIM + IDX_HEADS]
    return dict(qa16=qa16, ka=ka, ka16=ka16, va=va, va16=va16, ga=ga, qb16=qb16, kb=kb, kb16=kb16,
                vb=vb, vb16=vb16, gb=gb, qi16=qi16, mg=mg, ki=ki, ki2=ki2, wi=wi)


def _row_major(hm):
    g, r, n = hm.shape
    return jnp.transpose(hm, (1, 0, 2)).reshape(r, g * n)


def _pad_new(a, bsz, ts):
    a = a.reshape(bsz, ts, a.shape[-1]).astype(jnp.bfloat16)
    return jnp.pad(a, ((0, 0), (0, LANES - ts), (0, 0)))


def kernel(x_prompt, x_sample, p_prompt, p_sample, cache_diff_k, cache_diff_v, cache_dsa_k, cache_dsa_v, cache_idx_k, ln_g, w_in, q_norm_a, k_norm_a, lam_q1, lam_k1, lam_q2, lam_k2, subln_a, q_norm_b, k_norm_b, k_norm_idx, w_branch_a, w_branch_b, w_out, ple_norm, w_ple_gate, w_ple):
    depth = w_in.shape[0]
    bp, t_p, d_model = x_prompt.shape
    bsz, t_s, _ = x_sample.shape
    past = cache_diff_k.shape[2]
    assert bp == 1 and t_s <= LANES and t_p % LANES == 0 and past % LANES == 0
    topk_p = min(TOPK_MAX, t_p // 4)
    topk_s = min(TOPK_MAX, (past + t_s) // 4)
    pos_p = jnp.arange(t_p)
    pos_s = jnp.tile(past + jnp.arange(t_s), bsz)
    r_s = bsz * t_s

    tm_p = _tile(t_p, 512)
    tm_s = _tile(r_s, 512)
    tq_a, tk_a = _tile(t_p, 256), _tile(t_p, 1024)
    tq_i, tk_i = _tile(t_p, 128), _tile(t_p, 512)
    tq_b, tk_b = _tile(t_p, 256), _tile(t_p, 1024)
    tk_s = _tile(past, 512)
    tm_o = _tile(t_p, 256)
    tm_os = _tile(r_s, 256)

    caches = (cache_diff_k.reshape(depth, bsz, past, SEC), cache_diff_v.reshape(depth, bsz, past, SEC),
              cache_dsa_k.reshape(depth, bsz, past, SEC), cache_dsa_v.reshape(depth, bsz, past, SEC),
              cache_idx_k)

    sizes = (SEC,) * 9 + (IDX_DIM, IDX_HEADS, 2 * d_model)
    names = ("qa", "ka", "va", "ga", "qb", "kb", "vb", "gb", "qi", "ki", "wi", "mg")
    offs = [0]
    for s in sizes:
        offs.append(offs[-1] + s)

    hp = x_prompt.reshape(t_p, d_model)
    hs = x_sample.reshape(r_s, d_model)
    outs_p = [[] for _ in range(5)]
    outs_s = [[] for _ in range(5)]
    for l in range(depth):
        lam_init = 0.8 - 0.6 * math.exp(-0.3 * l)
        wl = w_in[l].astype(jnp.bfloat16)
        w_sec = {n: wl[:, offs[i]:offs[i + 1]] for i, n in enumerate(names)}
        w_sec["small"] = jnp.pad(jnp.concatenate([w_sec.pop("ki"), w_sec.pop("wi")], axis=1),
                                 ((0, 0), (0, LANES - IDX_DIM - IDX_HEADS)))
        lane_gain = lambda g: jnp.tile(g.astype(jnp.float32), LANES // g.shape[0]).reshape(1, LANES)
        lw = dict(ln_g=ln_g[l], w_sec=w_sec, qn_a=lane_gain(q_norm_a[l]), kn_a=lane_gain(k_norm_a[l]),
                  qn_b=lane_gain(q_norm_b[l]), kn_b=lane_gain(k_norm_b[l]), kn_i=lane_gain(k_norm_idx[l]))
        lam_p = jnp.stack([lam_q1[l], lam_k1[l], lam_q2[l], lam_k2[l]]).astype(jnp.float32)
        sg = subln_a[l].reshape(1, DV_A).astype(jnp.float32)
        w_o = (w_branch_a[l].astype(jnp.bfloat16), w_branch_b[l].astype(jnp.bfloat16),
               w_out[l].astype(jnp.bfloat16), ple_norm[l].reshape(1, d_model),
               w_ple_gate[l].astype(jnp.bfloat16), w_ple[l].astype(jnp.bfloat16))

        a = _layer_inputs(hp, pos_p, lw, tm_p)
        oa = _diff_attn(a["qa16"], a["ka16"], a["va16"], lam_p, sg, lam_init, tq_a, tk_a)
        bias = _index_select(a["qi16"], a["wi"], a["ki2"], topk_p, tq_i, tk_i)
        ob = _dsa_attn(a["qb16"], a["kb16"], a["vb16"], bias, tq_b, tk_b)
        hp = _layer_output(hp, p_prompt[l].reshape(t_p, -1), oa, a["ga"], ob, a["gb"], a["mg"], *w_o, tm_o)
        for lst, key in zip(outs_p, ("ka", "va", "kb", "vb", "ki")):
            lst.append(a[key])

        s = _layer_inputs(hs, pos_s, lw, tm_s)
        qa_s = _row_major(s["qa16"]).reshape(bsz, t_s, SEC)
        qb_s = _row_major(s["qb16"]).reshape(bsz, t_s, SEC)
        qi_s = _row_major(s["qi16"]).reshape(bsz, t_s, IDX_HEADS, IDX_DIM)
        qi_s = jnp.transpose(qi_s, (0, 2, 1, 3)).reshape(bsz, IDX_HEADS * t_s, IDX_DIM)
        wi_s = jnp.transpose(s["wi"].reshape(bsz, t_s, IDX_HEADS), (0, 2, 1)).reshape(bsz, IDX_HEADS * t_s, 1)
        news = tuple(_pad_new(s[k], bsz, t_s) for k in ("ka", "va", "kb", "vb", "ki"))
        oa_s, ob_s = _sample_attn(l, qa_s, qb_s, qi_s, wi_s, caches, news, lam_p, sg, lam_init, topk_s, tk_s)
        hs = _layer_output(hs, p_sample[l].reshape(r_s, -1), oa_s.reshape(r_s, SEC), s["ga"],
                           ob_s.reshape(r_s, SEC), s["gb"], s["mg"], *w_o, tm_os)
        for lst, key in zip(outs_s, ("ka", "va", "kb", "vb", "ki")):
            lst.append(s[key])

    def stack(lst, shape):
        return jnp.stack(lst).reshape((depth,) + shape)

    return (hp.reshape(bp, t_p, d_model), hs.reshape(bsz, t_s, d_model),
            stack(outs_p[0], (bp, t_p, H_A, 2, DH_A)), stack(outs_p[1], (bp, t_p, H_A, DV_A)),
            stack(outs_p[2], (bp, t_p, H_B, DH_B)), stack(outs_p[3], (bp, t_p, H_B, DH_B)),
            stack(outs_p[4], (bp, t_p, IDX_DIM)),
            stack(outs_s[0], (bsz, t_s, H_A, 2, DH_A)), stack(outs_s[1], (bsz, t_s, H_A, DV_A)),
            stack(outs_s[2], (bsz, t_s, H_B, DH_B)), stack(outs_s[3], (bsz, t_s, H_B, DH_B)),
            stack(outs_s[4], (bsz, t_s, IDX_DIM)))
```

```python
import functools
import math

import jax
import jax.numpy as jnp
import numpy as np
from jax import lax
from jax.experimental import pallas as pl
from jax.experimental.pallas import tpu as pltpu

CHUNK = 64
ROPE_THETA = 500000.0
EPS = 1e-6
H_A = 8
DH_A = 64
DV_A = 128
H_B = 8
DH_B = 128
IDX_HEADS = 16
IDX_DIM = 64
TOPK_MAX = 256
LANES = 128
GROUPS = 8
SEC = GROUPS * LANES
ROW_GROUP = 128
NEG = -1e30
LOG2E = 1.4426950408889634
MAX_UNSHIFTED = 60.0
INT_MIN = -(2 ** 31)
INT_MAX = 2 ** 31 - 1
VMEM_LIMIT = 56 * 1024 * 1024

_NT = (((1,), (1,)), ((), ()))


def _cdiv(a, b):
    return (a + b - 1) // b


def _params(sem, vmem=VMEM_LIMIT):
    return pltpu.CompilerParams(dimension_semantics=sem, vmem_limit_bytes=vmem)


def _rmsnorm_kernel(x_ref, g_ref, o_ref):
    x = x_ref[...]
    ms = jnp.mean(x * x, axis=-1, keepdims=True)
    o_ref[...] = (x * lax.rsqrt(ms + EPS) * g_ref[...]).astype(o_ref.dtype)


def _rmsnorm(x, g, tm):
    r, d = x.shape
    return pl.pallas_call(
        _rmsnorm_kernel,
        grid=(r // tm,),
        in_specs=[pl.BlockSpec((tm, d), lambda i: (i, 0)),
                  pl.BlockSpec((1, d), lambda i: (0, 0))],
        out_specs=pl.BlockSpec((tm, d), lambda i: (i, 0)),
        out_shape=jax.ShapeDtypeStruct((r, d), jnp.bfloat16),
        compiler_params=_params(("parallel",)),
        name="rmsnorm",
    )(x, g.reshape(1, d))


def _head_norm(z, g, d):
    sq = z * z
    if d == LANES:
        r = lax.rsqrt(jnp.sum(sq, axis=-1, keepdims=True) * (1.0 / d) + EPS)
    else:
        lo = lax.broadcasted_iota(jnp.int32, z.shape, 1) < d
        s_lo = jnp.sum(jnp.where(lo, sq, 0.0), axis=-1, keepdims=True)
        s_hi = jnp.sum(jnp.where(lo, 0.0, sq), axis=-1, keepdims=True)
        r = jnp.where(lo, lax.rsqrt(s_lo * (1.0 / d) + EPS), lax.rsqrt(s_hi * (1.0 / d) + EPS))
    return z * r * g


def _rope(z, c, a, b, half):
    return z * c + pltpu.roll(z, LANES - half, 1) * a + pltpu.roll(z, half, 1) * b


def _proj_kernel(xn_ref, w_ref, g_ref, rc_ref, ra_ref, rb_ref, *out_refs, norm, rope, scale, kinds):
    z = jnp.dot(xn_ref[...], w_ref[...], preferred_element_type=jnp.float32)
    for c in range(GROUPS):
        zc = z[:, c * LANES:(c + 1) * LANES]
        if norm:
            zc = _head_norm(zc, g_ref[...], norm)
        if rope:
            zc = _rope(zc, rc_ref[...], ra_ref[...], rb_ref[...], rope // 8)
        for kind, o_ref in zip(kinds, out_refs):
            if kind == "f32":
                o_ref[:, c * LANES:(c + 1) * LANES] = zc
            else:
                o_ref[c] = (zc * scale).astype(o_ref.dtype)


def _proj(xn, w, gain, tables, *, norm, rope, scale, kinds, tm):
    r, d = xn.shape
    nj = w.shape[1] // SEC
    rc, ra, rb = tables
    out_shape, out_specs = [], []
    for kind in kinds:
        if kind == "f32":
            out_shape.append(jax.ShapeDtypeStruct((r, nj * SEC), jnp.float32))
            out_specs.append(pl.BlockSpec((tm, SEC), lambda j, i: (i, j)))
        else:
            assert nj == 1
            out_shape.append(jax.ShapeDtypeStruct((GROUPS, r, LANES), jnp.bfloat16))
            out_specs.append(pl.BlockSpec((GROUPS, tm, LANES), lambda j, i: (0, i, 0)))
    row = pl.BlockSpec((tm, LANES), lambda j, i: (i, 0))
    outs = pl.pallas_call(
        functools.partial(_proj_kernel, norm=norm, rope=rope, scale=scale, kinds=tuple(kinds)),
        grid=(nj, r // tm),
        in_specs=[pl.BlockSpec((tm, d), lambda j, i: (i, 0)),
                  pl.BlockSpec((d, SEC), lambda j, i: (0, j)),
                  pl.BlockSpec((1, LANES), lambda j, i: (0, 0)),
                  row, row, row],
        out_specs=out_specs,
        out_shape=out_shape,
        compiler_params=_params(("parallel", "parallel")),
        name="proj",
    )(xn, w, gain, rc, ra, rb)
    return outs


def _small_proj_kernel(xn_ref, w_ref, g_ref, rc_ref, ra_ref, rb_ref, f_ref, k2_ref, *, wscale):
    z = jnp.dot(xn_ref[...], w_ref[...], preferred_element_type=jnp.float32)
    lo = lax.broadcasted_iota(jnp.int32, z.shape, 1) < IDX_DIM
    ms = jnp.sum(jnp.where(lo, z * z, 0.0), axis=-1, keepdims=True) * (1.0 / IDX_DIM)
    zn = z * lax.rsqrt(ms + EPS) * g_ref[...]
    zr = _rope(zn, rc_ref[...], ra_ref[...], rb_ref[...], IDX_DIM // 8)
    f_ref[...] = jnp.where(lo, zr, z * wscale)
    k2_ref[...] = jnp.where(lo, zr, pltpu.roll(zr, IDX_DIM, 1)).astype(k2_ref.dtype)


def _small_proj(xn, w, gain, tables, tm):
    r, d = xn.shape
    rc, ra, rb = tables
    row = pl.BlockSpec((tm, LANES), lambda i: (i, 0))
    return pl.pallas_call(
        functools.partial(_small_proj_kernel, wscale=IDX_HEADS ** -0.5),
        grid=(r // tm,),
        in_specs=[pl.BlockSpec((tm, d), lambda i: (i, 0)),
                  pl.BlockSpec((d, LANES), lambda i: (0, 0)),
                  pl.BlockSpec((1, LANES), lambda i: (0, 0)),
                  row, row, row],
        out_specs=[row, row],
        out_shape=[jax.ShapeDtypeStruct((r, LANES), jnp.float32),
                   jax.ShapeDtypeStruct((r, LANES), jnp.bfloat16)],
        compiler_params=_params(("parallel",)),
        name="small_proj",
    )(xn, w, gain, rc, ra, rb)


def _rope_tables(pos, d):
    r = d // 4
    h = r // 2
    n = pos.shape[0]
    inv = ROPE_THETA ** (-jnp.arange(0, r, 2, dtype=jnp.float32) / r)
    ang = pos.astype(jnp.float32)[:, None] * inv[None, :]
    cos, sin = jnp.cos(ang), jnp.sin(ang)
    zh = jnp.zeros((n, h), jnp.float32)
    zr = jnp.zeros((n, d - r), jnp.float32)
    c = jnp.concatenate([cos, cos, jnp.ones((n, d - r), jnp.float32)], axis=-1)
    a = jnp.concatenate([-sin, zh, zr], axis=-1)
    b = jnp.concatenate([zh, sin, zr], axis=-1)
    rep = LANES // d
    return tuple(jnp.tile(t, (1, rep)) for t in (c, a, b))


def _online_update(s, v, m_ref, l_ref, acc_ref):
    m_prev = m_ref[...]
    m_new = jnp.maximum(m_prev, jnp.max(s, axis=-1, keepdims=True))
    alpha = jnp.exp2(m_prev - m_new)
    p = jnp.exp2(s - m_new)
    l_ref[...] = alpha * l_ref[...] + jnp.sum(p, axis=-1, keepdims=True)
    acc_ref[...] = alpha * acc_ref[...] + jnp.dot(p.astype(v.dtype), v, preferred_element_type=jnp.float32)
    m_ref[...] = m_new


def _causal_steps(t, tq, tk):
    qs, ks = [], []
    for qi in range(t // tq):
        nk = _cdiv((qi + 1) * tq, tk)
        qs += [qi] * nk
        ks += list(range(nk))
    return np.asarray(qs, np.int32), np.asarray(ks, np.int32)


def _unshifted_ok(gq, gk, d, scale):
    bound = jnp.max(jnp.abs(gq)) * jnp.max(jnp.abs(gk)) * (d * scale * (1.0 + 2.0 ** -6))
    return (bound <= MAX_UNSHIFTED).astype(jnp.int32).reshape(1)


def _lambda(lam_ref, lam_init):
    lp = lam_ref[...]
    s1 = jnp.sum(lp[0:1] * lp[1:2], axis=-1, keepdims=True)
    s2 = jnp.sum(lp[2:3] * lp[3:4], axis=-1, keepdims=True)
    return jnp.exp(s1) - jnp.exp(s2) + lam_init


def _diff_finish(o0, o1, lam, sg, lam_init):
    o = o0 - lam * o1
    ms = jnp.mean(o * o, axis=-1, keepdims=True)
    return o * lax.rsqrt(ms + EPS) * sg * (1.0 - lam_init)


def _chunk_end(pos):
    return (pos // CHUNK + 1) * CHUNK - 1


def _diff_attn_kernel(qt_ref, kt_ref, ok_ref, q_ref, k_ref, v_ref, lam_ref, sg_ref, o_ref,
                      qs_ref, v1_ref, acc_ref, m_ref, l_ref, *, tq, tk, lam_init):
    step_id = pl.program_id(1)
    qi = qt_ref[step_id]
    kj = kt_ref[step_id]
    nk = _cdiv((qi + 1) * tq, tk)
    unshifted = ok_ref[0] == 1
    full = (kj + 1) * tk <= qi * tq + CHUNK
    rows2 = 2 * tq

    @pl.when(kj == 0)
    def _():
        q = q_ref[...]
        lo = lax.broadcasted_iota(jnp.int32, q.shape, 1) < DH_A
        zero = jnp.zeros_like(q)
        qs_ref[0:tq, :] = jnp.where(lo, q, zero)
        qs_ref[tq:rows2, :] = jnp.where(lo, zero, q)
        v1_ref[:, DV_A:] = jnp.ones((tk, LANES), v1_ref.dtype)
        m_ref[...] = jnp.full(m_ref.shape, NEG, jnp.float32)
        l_ref[...] = jnp.zeros(l_ref.shape, jnp.float32)
        acc_ref[...] = jnp.zeros(acc_ref.shape, jnp.float32)

    v1_ref[:, :DV_A] = v_ref[...]

    def visible(r0, nrows):
        row = r0 + lax.broadcasted_iota(jnp.int32, (nrows, tk), 0)
        row = jnp.where(row >= tq, row - tq, row)
        kpos = kj * tk + lax.broadcasted_iota(jnp.int32, (nrows, tk), 1)
        return kpos <= _chunk_end(qi * tq + row)

    def fast_step(masked):
        k = k_ref[...]
        v1 = v1_ref[...]
        for r0 in range(0, rows2, ROW_GROUP):
            rows = slice(r0, r0 + ROW_GROUP)
            s = lax.dot_general(qs_ref[rows, :], k, _NT, preferred_element_type=jnp.float32)
            if masked:
                s = jnp.where(visible(r0, ROW_GROUP), s, NEG)
            p = jnp.exp2(s).astype(v1.dtype)
            acc_ref[rows, :] += jnp.dot(p, v1, preferred_element_type=jnp.float32)

    def exact_step(masked):
        s = lax.dot_general(qs_ref[...], k_ref[...], _NT, preferred_element_type=jnp.float32)
        if masked:
            s = jnp.where(visible(0, rows2), s, NEG)
        _online_update(s, v_ref[...], m_ref, l_ref, acc_ref.at[:, :DV_A])

    for fast, step in ((True, fast_step), (False, exact_step)):
        mode = unshifted if fast else jnp.logical_not(unshifted)

        @pl.when(jnp.logical_and(mode, full))
        def _():
            step(False)

        @pl.when(jnp.logical_and(mode, jnp.logical_not(full)))
        def _():
            step(True)

    @pl.when(kj == nk - 1)
    def _():
        @pl.when(jnp.logical_not(unshifted))
        def _():
            acc_ref[:, DV_A:] = jnp.broadcast_to(l_ref[...], (rows2, LANES))

        o = acc_ref[:, :DV_A] / acc_ref[:, DV_A:]
        o_ref[...] = _diff_finish(o[0:tq], o[tq:rows2], _lambda(lam_ref, lam_init), sg_ref[...], lam_init)


def _diff_attn(q, k, v, ok, lam_p, sg, lam_init, tq, tk):
    _, t, _ = q.shape
    qt, kt = _causal_steps(t, tq, tk)
    kv_spec = pl.BlockSpec((None, tk, LANES), lambda h, s, qt, kt, ok: (h, kt[s], 0))
    const = lambda h, s, qt, kt, ok: (0, 0)
    grid_spec = pltpu.PrefetchScalarGridSpec(
        num_scalar_prefetch=3,
        grid=(H_A, len(qt)),
        in_specs=[pl.BlockSpec((None, tq, LANES), lambda h, s, qt, kt, ok: (h, qt[s], 0)),
                  kv_spec, kv_spec,
                  pl.BlockSpec((4, DH_A), const),
                  pl.BlockSpec((1, DV_A), const)],
        out_specs=pl.BlockSpec((tq, LANES), lambda h, s, qt, kt, ok: (qt[s], h)),
        scratch_shapes=[pltpu.VMEM((2 * tq, LANES), jnp.bfloat16),
                        pltpu.VMEM((tk, DV_A + LANES), jnp.bfloat16),
                        pltpu.VMEM((2 * tq, DV_A + LANES), jnp.float32),
                        pltpu.VMEM((2 * tq, 1), jnp.float32),
                        pltpu.VMEM((2 * tq, 1), jnp.float32)])
    return pl.pallas_call(
        functools.partial(_diff_attn_kernel, tq=tq, tk=tk, lam_init=lam_init),
        grid_spec=grid_spec,
        out_shape=jax.ShapeDtypeStruct((t, H_A * DV_A), jnp.float32),
        compiler_params=_params(("parallel", "arbitrary")),
        name="diff_attn",
    )(jnp.asarray(qt), jnp.asarray(kt), ok, q, k, v, lam_p, sg)


def _sort_key(score):
    bits = pltpu.bitcast(score, jnp.int32)
    key = jnp.where(bits < 0, bits ^ INT_MAX, bits)
    return jnp.where(key == -1, 0, key)


def _index_kernel(qt_ref, kt_ref, q_ref, w_ref, k_ref, bias_ref, qs_ref, key_ref, p_ref, *, tq, tk, t, topk):
    step_id = pl.program_id(0)
    qi = qt_ref[step_id]
    kj = kt_ref[step_id]
    nk = _cdiv((qi + 1) * tq, tk)
    nsub = tk // LANES

    @pl.when(kj == 0)
    def _():
        for p in range(GROUPS):
            q = q_ref[p]
            lo = lax.broadcasted_iota(jnp.int32, q.shape, 1) < IDX_DIM
            zero = jnp.zeros_like(q)
            qs_ref[(2 * p) * tq:(2 * p + 1) * tq, :] = jnp.where(lo, q, zero)
            qs_ref[(2 * p + 1) * tq:(2 * p + 2) * tq, :] = jnp.where(lo, zero, q)

    logits = lax.dot_general(qs_ref[...], k_ref[...], _NT, preferred_element_type=jnp.float32)
    w = w_ref[...]
    score = None
    for h in range(IDX_HEADS):
        term = jnp.maximum(logits[h * tq:(h + 1) * tq], 0.0) * w[:, h:h + 1]
        score = term if score is None else score + term
    qpos = qi * tq + lax.broadcasted_iota(jnp.int32, score.shape, 0)
    kpos = kj * tk + lax.broadcasted_iota(jnp.int32, score.shape, 1)
    key = jnp.where(kpos <= _chunk_end(qpos), _sort_key(score), INT_MIN)
    key_ref[:, pl.ds(pl.multiple_of(kj * tk, tk), tk)] = key

    @pl.when(kj == nk - 1)
    def _():
        lane = lax.broadcasted_iota(jnp.int32, (tq, LANES), 1)
        cend = _chunk_end(qi * tq + lax.broadcasted_iota(jnp.int32, (tq, LANES), 0))

        def rows(x):
            return jnp.broadcast_to(x, (tq, LANES))

        def sweep(fn, init):
            def body(j, acc):
                for c in range(nsub):
                    off = pl.multiple_of(j * tk + c * LANES, LANES)
                    acc = fn(acc, key_ref[:, pl.ds(off, LANES)], off + lane, c % 2)
                return acc
            return lax.fori_loop(0, nk, body, init)

        def count(pred):
            acc = sweep(lambda a, kt, idx, par: a + jnp.where(pred(kt, idx), 1.0, 0.0),
                        jnp.zeros((tq, LANES), jnp.float32))
            return rows(jnp.sum(acc, axis=1, keepdims=True))

        def key_score(key):
            f = pltpu.bitcast(jnp.where(key < 0, key ^ INT_MAX, key), jnp.float32)
            return jnp.where(key == INT_MIN, -jnp.inf, f)

        kf = float(topk)
        ninf = jnp.full((tq, LANES), -jnp.inf, jnp.float32)

        def group_max(acc, kt, idx, par):
            f = key_score(kt)
            return (jnp.maximum(acc[0], f), acc[1]) if par == 0 else (acc[0], jnp.maximum(acc[1], f))

        m0, m1 = sweep(group_max, (ninf, ninf))
        lo = _sort_key(rows(jnp.min(jnp.minimum(m0, m1), axis=1, keepdims=True)))
        hi = _sort_key(rows(jnp.max(jnp.maximum(m0, m1), axis=1, keepdims=True))) + 1
        c_lo = count(lambda kt, idx: kt >= lo)
        c_hi = jnp.zeros((tq, LANES), jnp.float32)
        all_sel = cend + 1 <= topk
        lo = jnp.where(all_sel, INT_MIN + 1, lo)
        done = jnp.where(jnp.logical_or(all_sel, c_lo == kf), 1.0, 0.0)

        def search_cond(st):
            return jnp.logical_and(st[0] < 80, st[6] < 0.5)

        def search_body(st):
            it, lo, hi, c_lo, c_hi, done, _ = st
            mid = (lo >> 1) + (hi >> 1) + (lo & hi & 1)
            lo_f, hi_f = key_score(lo), key_score(hi)
            frac = (jnp.log(c_lo) - math.log(kf)) / (jnp.log(c_lo) - jnp.log(jnp.maximum(c_hi, 0.5)))
            t_f = lo_f + (hi_f - lo_f) * frac
            interp = jnp.logical_and((it & 1) == 0, jnp.abs(t_f) < jnp.inf)
            t = jnp.where(interp, _sort_key(t_f), mid)
            t = jnp.minimum(jnp.maximum(t, lo + 1), hi - 1)
            c = count(lambda kt, idx: kt >= t)
            up = jnp.logical_and(done < 0.5, c >= kf)
            dn = jnp.logical_and(done < 0.5, c < kf)
            lo, c_lo = jnp.where(up, t, lo), jnp.where(up, c, c_lo)
            hi, c_hi = jnp.where(dn, t, hi), jnp.where(dn, c, c_hi)
            done = jnp.where(jnp.logical_or(c_lo == kf, hi == lo + 1), 1.0, done)
            return it + 1, lo, hi, c_lo, c_hi, done, jnp.min(done)

        st = (jnp.int32(0), lo, hi, c_lo, c_hi, done, jnp.min(done))
        _, thr, hi, c_lo, c_hi, _, _ = lax.while_loop(search_cond, search_body, st)
        tied = jnp.logical_and(jnp.logical_not(all_sel), c_lo > kf)
        need = jnp.where(tied, kf - c_hi, jnp.inf)
        p_ref[...] = jnp.full((tq, LANES), INT_MAX, jnp.int32)

        @pl.when(jnp.max(jnp.where(tied, 1.0, 0.0)) > 0.5)
        def _():
            nbits = max(1, (t - 1).bit_length())

            def pos_body(b, pos):
                cand = pos | lax.shift_left(jnp.int32(1), jnp.int32(nbits - 1) - b)
                g = count(lambda kt, idx: jnp.logical_and(kt == thr, idx < cand))
                return jnp.where(g < need, cand, pos)

            p_ref[...] = lax.fori_loop(0, nbits, pos_body, jnp.zeros((tq, LANES), jnp.int32))

        last = p_ref[...]

        def write_body(j, carry):
            for c in range(nsub):
                off = pl.multiple_of(j * tk + c * LANES, LANES)
                kt = key_ref[:, pl.ds(off, LANES)]
                idx = off + lane
                sel = jnp.logical_or(kt > thr, jnp.logical_and(kt == thr, idx <= last))
                sel = jnp.logical_and(sel, idx <= cend)
                bias_ref[:, pl.ds(off, LANES)] = jnp.where(sel, 0.0, NEG).astype(bias_ref.dtype)
            return carry

        lax.fori_loop(0, nk, write_body, 0)

        def fill_body(j, carry):
            off = pl.multiple_of(j * tk, tk)
            bias_ref[:, pl.ds(off, tk)] = jnp.full((tq, tk), NEG, bias_ref.dtype)
            return carry

        lax.fori_loop(nk, t // tk, fill_body, 0)


def _index_select(qi16, wi, ki2, topk, tq, tk):
    _, t, _ = qi16.shape
    qt, kt = _causal_steps(t, tq, tk)
    grid_spec = pltpu.PrefetchScalarGridSpec(
        num_scalar_prefetch=2,
        grid=(len(qt),),
        in_specs=[pl.BlockSpec((GROUPS, tq, LANES), lambda s, qt, kt: (0, qt[s], 0)),
                  pl.BlockSpec((tq, IDX_HEADS), lambda s, qt, kt: (qt[s], 0)),
                  pl.BlockSpec((tk, LANES), lambda s, qt, kt: (kt[s], 0))],
        out_specs=pl.BlockSpec((tq, t), lambda s, qt, kt: (qt[s], 0)),
        scratch_shapes=[pltpu.VMEM((IDX_HEADS * tq, LANES), jnp.bfloat16),
                        pltpu.VMEM((tq, t), jnp.int32),
                        pltpu.VMEM((tq, LANES), jnp.int32)])
    return pl.pallas_call(
        functools.partial(_index_kernel, tq=tq, tk=tk, t=t, topk=topk),
        grid_spec=grid_spec,
        out_shape=jax.ShapeDtypeStruct((t, t), jnp.bfloat16),
        compiler_params=_params(("arbitrary",)),
        name="index_select",
    )(jnp.asarray(qt), jnp.asarray(kt), qi16, wi, ki2)


def _dsa_attn_kernel(qt_ref, kt_ref, ok_ref, q_ref, k_ref, v_ref, bias_ref, o_ref,
                     v1_ref, acc_ref, m_ref, l_ref, *, tq, tk):
    step_id = pl.program_id(0)
    qi = qt_ref[step_id]
    kj = kt_ref[step_id]
    nk = _cdiv((qi + 1) * tq, tk)
    unshifted = ok_ref[0] == 1

    @pl.when(kj == 0)
    def _():
        v1_ref[:, :, DH_B:] = jnp.ones((H_B, tk, LANES), v1_ref.dtype)
        m_ref[...] = jnp.full(m_ref.shape, NEG, jnp.float32)
        l_ref[...] = jnp.zeros(l_ref.shape, jnp.float32)
        acc_ref[...] = jnp.zeros(acc_ref.shape, jnp.float32)

    @pl.when(unshifted)
    def _():
        v1_ref[:, :, :DH_B] = v_ref[...]
        for r0 in range(0, tq, ROW_GROUP):
            rows = slice(r0, r0 + ROW_GROUP)
            bias = bias_ref[rows, :].astype(jnp.float32)
            for h in range(H_B):
                s = lax.dot_general(q_ref[h, rows, :], k_ref[h], _NT, preferred_element_type=jnp.float32) + bias
                p = jnp.exp2(s).astype(v1_ref.dtype)
                acc_ref[h, rows, :] += jnp.dot(p, v1_ref[h], preferred_element_type=jnp.float32)

    @pl.when(jnp.logical_not(unshifted))
    def _():
        bias = bias_ref[...].astype(jnp.float32)
        for h in range(H_B):
            s = lax.dot_general(q_ref[h], k_ref[h], _NT, preferred_element_type=jnp.float32) + bias
            _online_update(s, v_ref[h], m_ref.at[h], l_ref.at[h], acc_ref.at[h, :, :DH_B])

    @pl.when(kj == nk - 1)
    def _():
        @pl.when(jnp.logical_not(unshifted))
        def _():
            acc_ref[:, :, DH_B:] = jnp.broadcast_to(l_ref[...], (H_B, tq, LANES))

        for h in range(H_B):
            o_ref[:, h * LANES:(h + 1) * LANES] = acc_ref[h, :, :DH_B] / acc_ref[h, :, DH_B:]


def _dsa_attn(q, k, v, bias, ok, tq, tk):
    _, t, _ = q.shape
    qt, kt = _causal_steps(t, tq, tk)
    kv_spec = pl.BlockSpec((H_B, tk, LANES), lambda s, qt, kt, ok: (0, kt[s], 0))
    grid_spec = pltpu.PrefetchScalarGridSpec(
        num_scalar_prefetch=3,
        grid=(len(qt),),
        in_specs=[pl.BlockSpec((H_B, tq, LANES), lambda s, qt, kt, ok: (0, qt[s], 0)),
                  kv_spec, kv_spec,
                  pl.BlockSpec((tq, tk), lambda s, qt, kt, ok: (qt[s], kt[s]))],
        out_specs=pl.BlockSpec((tq, H_B * DH_B), lambda s, qt, kt, ok: (qt[s], 0)),
        scratch_shapes=[pltpu.VMEM((H_B, tk, DH_B + LANES), jnp.bfloat16),
                        pltpu.VMEM((H_B, tq, DH_B + LANES), jnp.float32),
                        pltpu.VMEM((H_B, tq, 1), jnp.float32),
                        pltpu.VMEM((H_B, tq, 1), jnp.float32)])
    return pl.pallas_call(
        functools.partial(_dsa_attn_kernel, tq=tq, tk=tk),
        grid_spec=grid_spec,
        out_shape=jax.ShapeDtypeStruct((t, H_B * DH_B), jnp.float32),
        compiler_params=_params(("arbitrary",)),
        name="dsa_attn",
    )(jnp.asarray(qt), jnp.asarray(kt), ok, q, k, v, bias)


def _sample_kernel(qa_ref, qb_ref, qi_ref, wi_ref, ckd_ref, cvd_ref, ckb_ref, cvb_ref, cki_ref,
                   nkd_ref, nvd_ref, nkb_ref, nvb_ref, nki_ref, lam_ref, sg_ref,
                   oa_ref, ob_ref,
                   qda_ref, qdb_ref, bias_ref, key_ref, ma_ref, la_ref, acca_ref, mb_ref, lb_ref, accb_ref,
                   *, ts, past, tk, topk, lam_init):
    kj = pl.program_id(1)
    nkt = past // tk
    width = past + LANES
    ra = 2 * H_A * ts
    rb = H_B * ts

    def new_key_valid(rows):
        t_q = lax.broadcasted_iota(jnp.int32, (rows, LANES), 0) % ts
        lane = lax.broadcasted_iota(jnp.int32, (rows, LANES), 1)
        return jnp.logical_and(lane < ts, past + lane <= _chunk_end(past + t_q))

    @pl.when(kj == 0)
    def _():
        qa = qa_ref[...]
        grp = lax.broadcasted_iota(jnp.int32, qa.shape, 1) // DH_A
        for hc in range(2 * H_A):
            qda_ref[hc * ts:(hc + 1) * ts, :] = jnp.where(grp == hc, qa, jnp.zeros_like(qa))
        qb = qb_ref[...]
        grp = lax.broadcasted_iota(jnp.int32, qb.shape, 1) // DH_B
        for h in range(H_B):
            qdb_ref[h * ts:(h + 1) * ts, :] = jnp.where(grp == h, qb, jnp.zeros_like(qb))
        for ref in (ma_ref, mb_ref):
            ref[...] = jnp.full(ref.shape, NEG, jnp.float32)
        for ref in (la_ref, lb_ref, acca_ref, accb_ref):
            ref[...] = jnp.zeros(ref.shape, jnp.float32)

        qi = qi_ref[...]
        w = wi_ref[...]

        def scores(keys16):
            logits = lax.dot_general(qi, keys16, _NT, preferred_element_type=jnp.float32)
            score = None
            for h in range(IDX_HEADS):
                term = jnp.maximum(logits[h * ts:(h + 1) * ts], 0.0) * w[h * ts:(h + 1) * ts]
                score = term if score is None else score + term
            return score

        step = min(past, 1024)
        for c in range(past // step):
            keys16 = cki_ref[c * step:(c + 1) * step, :].astype(jnp.bfloat16)
            key_ref[:, c * step:(c + 1) * step] = _sort_key(scores(keys16))
        key_ref[:, past:width] = jnp.where(new_key_valid(ts), _sort_key(scores(nki_ref[...])), INT_MIN)

        kf = float(topk)

        def count(pred):
            idx = lax.broadcasted_iota(jnp.int32, (ts, width), 1)
            c = jnp.sum(jnp.where(pred(key_ref[...], idx), 1.0, 0.0), axis=1, keepdims=True)
            return jnp.broadcast_to(c, (ts, LANES))

        def wide(x):
            return x[:, 0:1]

        c0 = count(lambda kt, idx: kt >= 0)
        prefix = jnp.where(c0 >= kf, 0, INT_MIN).astype(jnp.int32)

        def bit_body(b, prefix):
            cand = prefix | lax.shift_left(jnp.int32(1), jnp.int32(30) - b)
            c = count(lambda kt, idx: kt >= wide(cand))
            return jnp.where(c >= kf, cand, prefix)

        thr = lax.fori_loop(0, 31, bit_body, prefix)
        n_gt = count(lambda kt, idx: kt > wide(thr))
        need = kf - n_gt
        nbits = max(1, (width - 1).bit_length())

        def pos_body(b, pos):
            cand = pos | lax.shift_left(jnp.int32(1), jnp.int32(nbits - 1) - b)
            g = count(lambda kt, idx: jnp.logical_and(kt == wide(thr), idx < wide(cand)))
            return jnp.where(g < need, cand, pos)

        last = lax.fori_loop(0, nbits, pos_body, jnp.zeros((ts, LANES), jnp.int32))
        kt = key_ref[...]
        idx = lax.broadcasted_iota(jnp.int32, (ts, width), 1)
        sel = jnp.logical_or(kt > wide(thr), jnp.logical_and(kt == wide(thr), idx <= wide(last)))
        sel = jnp.logical_and(sel, kt != INT_MIN)
        bias = jnp.where(sel, 0.0, NEG)
        for h in range(H_B):
            bias_ref[h * ts:(h + 1) * ts, :] = bias

    sa = lax.dot_general(qda_ref[...], ckd_ref[...].astype(jnp.bfloat16), _NT, preferred_element_type=jnp.float32)
    _online_update(sa, cvd_ref[...].astype(jnp.bfloat16), ma_ref, la_ref, acca_ref)
    sb = lax.dot_general(qdb_ref[...], ckb_ref[...].astype(jnp.bfloat16), _NT, preferred_element_type=jnp.float32)
    sb = sb + bias_ref[:, pl.ds(pl.multiple_of(kj * tk, tk), tk)]
    _online_update(sb, cvb_ref[...].astype(jnp.bfloat16), mb_ref, lb_ref, accb_ref)

    @pl.when(kj == nkt - 1)
    def _():
        sa = lax.dot_general(qda_ref[...], nkd_ref[...], _NT, preferred_element_type=jnp.float32)
        sa = jnp.where(new_key_valid(ra), sa, NEG)
        _online_update(sa, nvd_ref[...], ma_ref, la_ref, acca_ref)
        sb = lax.dot_general(qdb_ref[...], nkb_ref[...], _NT, preferred_element_type=jnp.float32)
        sb = sb + bias_ref[:, past:width]
        _online_update(sb, nvb_ref[...], mb_ref, lb_ref, accb_ref)

        lam = _lambda(lam_ref, lam_init)
        oa = acca_ref[...] / la_ref[...]
        ob = accb_ref[...] / lb_ref[...]
        for h in range(H_A):
            cols = slice(h * DV_A, (h + 1) * DV_A)
            o0 = oa[(2 * h) * ts:(2 * h + 1) * ts, cols]
            o1 = oa[(2 * h + 1) * ts:(2 * h + 2) * ts, cols]
            oa_ref[:, cols] = _diff_finish(o0, o1, lam, sg_ref[...], lam_init)
        for h in range(H_B):
            cols = slice(h * DH_B, (h + 1) * DH_B)
            ob_ref[:, cols] = ob[h * ts:(h + 1) * ts, cols]


def _sample_attn(l, qa, qb, qi, wi, caches, news, lam_p, sg, lam_init, topk, tk):
    bsz, ts, _ = qa.shape
    ckd, cvd, ckb, cvb, cki = caches
    past = ckd.shape[2]
    width = past + LANES
    ra, rb = 2 * H_A * ts, H_B * ts
    cache_spec = pl.BlockSpec((None, None, tk, SEC), lambda b, kj: (l, b, kj, 0))
    new_spec = pl.BlockSpec((None, LANES, SEC), lambda b, kj: (b, 0, 0))
    row_spec = pl.BlockSpec((None, ts, SEC), lambda b, kj: (b, 0, 0))
    return pl.pallas_call(
        functools.partial(_sample_kernel, ts=ts, past=past, tk=tk, topk=topk, lam_init=lam_init),
        grid=(bsz, past // tk),
        in_specs=[row_spec, row_spec,
                  pl.BlockSpec((None, IDX_HEADS * ts, IDX_DIM), lambda b, kj: (b, 0, 0)),
                  pl.BlockSpec((None, IDX_HEADS * ts, 1), lambda b, kj: (b, 0, 0)),
                  cache_spec, cache_spec, cache_spec, cache_spec,
                  pl.BlockSpec((None, None, past, IDX_DIM), lambda b, kj: (l, b, 0, 0)),
                  new_spec, new_spec, new_spec, new_spec,
                  pl.BlockSpec((None, LANES, IDX_DIM), lambda b, kj: (b, 0, 0)),
                  pl.BlockSpec((4, DH_A), lambda b, kj: (0, 0)),
                  pl.BlockSpec((1, DV_A), lambda b, kj: (0, 0))],
        out_specs=[row_spec, row_spec],
        out_shape=[jax.ShapeDtypeStruct((bsz, ts, SEC), jnp.float32),
                   jax.ShapeDtypeStruct((bsz, ts, SEC), jnp.float32)],
        scratch_shapes=[pltpu.VMEM((ra, SEC), jnp.bfloat16),
                        pltpu.VMEM((rb, SEC), jnp.bfloat16),
                        pltpu.VMEM((rb, width), jnp.float32),
                        pltpu.VMEM((ts, width), jnp.int32),
                        pltpu.VMEM((ra, 1), jnp.float32),
                        pltpu.VMEM((ra, 1), jnp.float32),
                        pltpu.VMEM((ra, SEC), jnp.float32),
                        pltpu.VMEM((rb, 1), jnp.float32),
                        pltpu.VMEM((rb, 1), jnp.float32),
                        pltpu.VMEM((rb, SEC), jnp.float32)],
        compiler_params=_params(("parallel", "arbitrary")),
        name="sample_attn",
    )(qa, qb, qi, wi, ckd, cvd, ckb, cvb, cki, *news, lam_p, sg)


def _silu(x):
    return x * jax.nn.sigmoid(x)


def _merge_kernel(oa_ref, ga_ref, ob_ref, gb_ref, mg_ref, wa_ref, wb_ref, u_ref):
    d = u_ref.shape[-1]
    ya = jnp.dot((oa_ref[...] * _silu(ga_ref[...])).astype(jnp.bfloat16), wa_ref[...],
                 preferred_element_type=jnp.float32)
    yb = jnp.dot((ob_ref[...] * _silu(gb_ref[...])).astype(jnp.bfloat16), wb_ref[...],
                 preferred_element_type=jnp.float32)
    mg = mg_ref[...]
    u = jax.nn.sigmoid(mg[:, :d]) * ya + jax.nn.sigmoid(mg[:, d:]) * yb
    u_ref[...] = u.astype(u_ref.dtype)


def _resid_kernel(x_ref, u_ref, wo_ref, g_ref, h_ref, hn_ref):
    h = x_ref[...] + jnp.dot(u_ref[...], wo_ref[...], preferred_element_type=jnp.float32)
    h_ref[...] = h
    ms = jnp.mean(h * h, axis=-1, keepdims=True)
    hn_ref[...] = (h * lax.rsqrt(ms + EPS) * g_ref[...]).astype(hn_ref.dtype)


def _ple_kernel(h_ref, hn_ref, p_ref, wg_ref, wp_ref, o_ref):
    gate = jax.nn.sigmoid(jnp.dot(hn_ref[...], wg_ref[...], preferred_element_type=jnp.float32))
    e = jnp.dot(p_ref[...].astype(jnp.bfloat16), wp_ref[...], preferred_element_type=jnp.float32)
    o_ref[...] = h_ref[...] + gate * e


def _layer_output(x, p, oa, ga, ob, gb, mg, w_ba, w_bb, w_out, ple_g, w_pg, w_ple, tm):
    r, d = x.shape
    wa = oa.shape[1]
    dp = p.shape[1]
    rows = lambda n: pl.BlockSpec((tm, n), lambda i: (i, 0))
    full = lambda a, b: pl.BlockSpec((a, b), lambda i: (0, 0))
    u = pl.pallas_call(
        _merge_kernel,
        grid=(r // tm,),
        in_specs=[rows(wa), rows(wa), rows(wa), rows(wa), rows(2 * d), full(wa, d), full(wa, d)],
        out_specs=rows(d),
        out_shape=jax.ShapeDtypeStruct((r, d), jnp.bfloat16),
        compiler_params=_params(("parallel",)),
        name="merge",
    )(oa, ga, ob, gb, mg, w_ba, w_bb)
    h, hn = pl.pallas_call(
        _resid_kernel,
        grid=(r // tm,),
        in_specs=[rows(d), rows(d), full(d, d), full(1, d)],
        out_specs=[rows(d), rows(d)],
        out_shape=[jax.ShapeDtypeStruct((r, d), jnp.float32), jax.ShapeDtypeStruct((r, d), jnp.bfloat16)],
        compiler_params=_params(("parallel",)),
        name="resid",
    )(x, u, w_out, ple_g)
    return pl.pallas_call(
        _ple_kernel,
        grid=(r // tm,),
        in_specs=[rows(d), rows(d), rows(dp), full(d, d), full(dp, d)],
        out_specs=rows(d),
        out_shape=jax.ShapeDtypeStruct((r, d), jnp.float32),
        compiler_params=_params(("parallel",)),
        name="ple",
    )(h, hn, p, w_pg, w_ple)


def _tile(n, pref):
    t = min(n, pref)
    while n % t:
        t //= 2
    return t


def _layer_inputs(x, pos, lw, tm):
    xn = _rmsnorm(x, lw["ln_g"], tm)
    t64 = _rope_tables(pos, DH_A)
    t128 = _rope_tables(pos, DH_B)
    none = t64
    one = jnp.ones((1, LANES), jnp.float32)
    w = lw["w_sec"]
    pj = functools.partial(_proj, xn, tm=tm)
    (qa16,) = pj(w["qa"], lw["qn_a"], t64, norm=DH_A, rope=DH_A, scale=DH_A ** -0.5 * LOG2E, kinds=["hm16"])
    ka, ka16 = pj(w["ka"], lw["kn_a"], t64, norm=DH_A, rope=DH_A, scale=1.0, kinds=["f32", "hm16"])
    va, va16 = pj(w["va"], one, none, norm=0, rope=0, scale=1.0, kinds=["f32", "hm16"])
    (ga,) = pj(w["ga"], one, none, norm=0, rope=0, scale=1.0, kinds=["f32"])
    (qb16,) = pj(w["qb"], lw["qn_b"], t128, norm=DH_B, rope=DH_B, scale=DH_B ** -0.5 * LOG2E, kinds=["hm16"])
    kb, kb16 = pj(w["kb"], lw["kn_b"], t128, norm=DH_B, rope=DH_B, scale=1.0, kinds=["f32", "hm16"])
    vb, vb16 = pj(w["vb"], one, none, norm=0, rope=0, scale=1.0, kinds=["f32", "hm16"])
    (gb,) = pj(w["gb"], one, none, norm=0, rope=0, scale=1.0, kinds=["f32"])
    (qi16,) = pj(w["qi"], one, t64, norm=0, rope=DH_A, scale=IDX_DIM ** -0.5, kinds=["hm16"])
    (mg,) = pj(w["mg"], one, none, norm=0, rope=0, scale=1.0, kinds=["f32"])
    small, ki2 = _small_proj(xn, w["small"], lw["kn_i"], t64, tm)
    ki = small[:, :IDX_DIM]
    wi = small[:, IDX_DIM:IDX_DIM + IDX_HEADS]
    return dict(qa16=qa16, ka=ka, ka16=ka16, va=va, va16=va16, ga=ga, qb16=qb16, kb=kb, kb16=kb16,
                vb=vb, vb16=vb16, gb=gb, qi16=qi16, mg=mg, ki=ki, ki2=ki2, wi=wi)


def _row_major(hm):
    g, r, n = hm.shape
    return jnp.transpose(hm, (1, 0, 2)).reshape(r, g * n)


def _pad_new(a, bsz, ts):
    a = a.reshape(bsz, ts, a.shape[-1]).astype(jnp.bfloat16)
    return jnp.pad(a, ((0, 0), (0, LANES - ts), (0, 0)))


def kernel(x_prompt, x_sample, p_prompt, p_sample, cache_diff_k, cache_diff_v, cache_dsa_k, cache_dsa_v, cache_idx_k, ln_g, w_in, q_norm_a, k_norm_a, lam_q1, lam_k1, lam_q2, lam_k2, subln_a, q_norm_b, k_norm_b, k_norm_idx, w_branch_a, w_branch_b, w_out, ple_norm, w_ple_gate, w_ple):
    depth = w_in.shape[0]
    bp, t_p, d_model = x_prompt.shape
    bsz, t_s, _ = x_sample.shape
    past = cache_diff_k.shape[2]
    assert bp == 1 and t_s <= LANES and t_p % LANES == 0 and past % LANES == 0
    topk_p = min(TOPK_MAX, t_p // 4)
    topk_s = min(TOPK_MAX, (past + t_s) // 4)
    pos_p = jnp.arange(t_p)
    pos_s = jnp.tile(past + jnp.arange(t_s), bsz)
    r_s = bsz * t_s

    tm_p = _tile(t_p, 512)
    tm_s = _tile(r_s, 512)
    tq_a, tk_a = _tile(t_p, 256), _tile(t_p, 1024)
    tq_i, tk_i = _tile(t_p, 128), _tile(t_p, 512)
    tq_b, tk_b = _tile(t_p, 256), _tile(t_p, 1024)
    tk_s = _tile(past, 512)
    tm_o = _tile(t_p, 256)
    tm_os = _tile(r_s, 256)

    caches = (cache_diff_k.reshape(depth, bsz, past, SEC), cache_diff_v.reshape(depth, bsz, past, SEC),
              cache_dsa_k.reshape(depth, bsz, past, SEC), cache_dsa_v.reshape(depth, bsz, past, SEC),
              cache_idx_k)

    sizes = (SEC,) * 9 + (IDX_DIM, IDX_HEADS, 2 * d_model)
    names = ("qa", "ka", "va", "ga", "qb", "kb", "vb", "gb", "qi", "ki", "wi", "mg")
    offs = [0]
    for s in sizes:
        offs.append(offs[-1] + s)

    hp = x_prompt.reshape(t_p, d_model)
    hs = x_sample.reshape(r_s, d_model)
    outs_p = [[] for _ in range(5)]
    outs_s = [[] for _ in range(5)]
    for l in range(depth):
        lam_init = 0.8 - 0.6 * math.exp(-0.3 * l)
        wl = w_in[l].astype(jnp.bfloat16)
        w_sec = {n: wl[:, offs[i]:offs[i + 1]] for i, n in enumerate(names)}
        w_sec["small"] = jnp.pad(jnp.concatenate([w_sec.pop("ki"), w_sec.pop("wi")], axis=1),
                                 ((0, 0), (0, LANES - IDX_DIM - IDX_HEADS)))
        lane_gain = lambda g: jnp.tile(g.astype(jnp.float32), LANES // g.shape[0]).reshape(1, LANES)
        lw = dict(ln_g=ln_g[l], w_sec=w_sec, qn_a=lane_gain(q_norm_a[l]), kn_a=lane_gain(k_norm_a[l]),
                  qn_b=lane_gain(q_norm_b[l]), kn_b=lane_gain(k_norm_b[l]), kn_i=lane_gain(k_norm_idx[l]))
        ok_a = _unshifted_ok(q_norm_a[l], k_norm_a[l], DH_A, DH_A ** -0.5)
        ok_b = _unshifted_ok(q_norm_b[l], k_norm_b[l], DH_B, DH_B ** -0.5)
        lam_p = jnp.stack([lam_q1[l], lam_k1[l], lam_q2[l], lam_k2[l]]).astype(jnp.float32)
        sg = subln_a[l].reshape(1, DV_A).astype(jnp.float32)
        w_o = (w_branch_a[l].astype(jnp.bfloat16), w_branch_b[l].astype(jnp.bfloat16),
               w_out[l].astype(jnp.bfloat16), ple_norm[l].reshape(1, d_model),
               w_ple_gate[l].astype(jnp.bfloat16), w_ple[l].astype(jnp.bfloat16))

        a = _layer_inputs(hp, pos_p, lw, tm_p)
        oa = _diff_attn(a["qa16"], a["ka16"], a["va16"], ok_a, lam_p, sg, lam_init, tq_a, tk_a)
        bias = _index_select(a["qi16"], a["wi"], a["ki2"], topk_p, tq_i, tk_i)
        ob = _dsa_attn(a["qb16"], a["kb16"], a["vb16"], bias, ok_b, tq_b, tk_b)
        hp = _layer_output(hp, p_prompt[l].reshape(t_p, -1), oa, a["ga"], ob, a["gb"], a["mg"], *w_o, tm_o)
        for lst, key in zip(outs_p, ("ka", "va", "kb", "vb", "ki")):
            lst.append(a[key])

        s = _layer_inputs(hs, pos_s, lw, tm_s)
        qa_s = _row_major(s["qa16"]).reshape(bsz, t_s, SEC)
        qb_s = _row_major(s["qb16"]).reshape(bsz, t_s, SEC)
        qi_s = _row_major(s["qi16"]).reshape(bsz, t_s, IDX_HEADS, IDX_DIM)
        qi_s = jnp.transpose(qi_s, (0, 2, 1, 3)).reshape(bsz, IDX_HEADS * t_s, IDX_DIM)
        wi_s = jnp.transpose(s["wi"].reshape(bsz, t_s, IDX_HEADS), (0, 2, 1)).reshape(bsz, IDX_HEADS * t_s, 1)
        news = tuple(_pad_new(s[k], bsz, t_s) for k in ("ka", "va", "kb", "vb", "ki"))
        oa_s, ob_s = _sample_attn(l, qa_s, qb_s, qi_s, wi_s, caches, news, lam_p, sg, lam_init, topk_s, tk_s)
        hs = _layer_output(hs, p_sample[l].reshape(r_s, -1), oa_s.reshape(r_s, SEC), s["ga"],
                           ob_s.reshape(r_s, SEC), s["gb"], s["mg"], *w_o, tm_os)
        for lst, key in zip(outs_s, ("ka", "va", "kb", "vb", "ki")):
            lst.append(s[key])

    def stack(lst, shape):
        return jnp.stack(lst).reshape((depth,) + shape)

    return (hp.reshape(bp, t_p, d_model), hs.reshape(bsz, t_s, d_model),
            stack(outs_p[0], (bp, t_p, H_A, 2, DH_A)), stack(outs_p[1], (bp, t_p, H_A, DV_A)),
            stack(outs_p[2], (bp, t_p, H_B, DH_B)), stack(outs_p[3], (bp, t_p, H_B, DH_B)),
            stack(outs_p[4], (bp, t_p, IDX_DIM)),
            stack(outs_s[0], (bsz, t_s, H_A, 2, DH_A)), stack(outs_s[1], (bsz, t_s, H_A, DV_A)),
            stack(outs_s[2], (bsz, t_s, H_B, DH_B)), stack(outs_s[3], (bsz, t_s, H_B, DH_B)),
            stack(outs_s[4], (bsz, t_s, IDX_DIM)))
```

```python
import functools
import math

import jax
import jax.numpy as jnp
import numpy as np
from jax import lax
from jax.experimental import pallas as pl
from jax.experimental.pallas import tpu as pltpu

CHUNK = 64
ROPE_THETA = 500000.0
EPS = 1e-6
H_A = 8
DH_A = 64
DV_A = 128
H_B = 8
DH_B = 128
IDX_HEADS = 16
IDX_DIM = 64
TOPK_MAX = 256
LANES = 128
GROUPS = 8
SEC = GROUPS * LANES
ROW_GROUP = 128
NEG = -1e30
LOG2E = 1.4426950408889634
MAX_UNSHIFTED = 60.0
INT_MIN = -(2 ** 31)
INT_MAX = 2 ** 31 - 1
VMEM_LIMIT = 56 * 1024 * 1024

_NT = (((1,), (1,)), ((), ()))


def _cdiv(a, b):
    return (a + b - 1) // b


def _params(sem, vmem=VMEM_LIMIT):
    return pltpu.CompilerParams(dimension_semantics=sem, vmem_limit_bytes=vmem)


def _rmsnorm_kernel(x_ref, g_ref, o_ref):
    x = x_ref[...]
    ms = jnp.mean(x * x, axis=-1, keepdims=True)
    o_ref[...] = (x * lax.rsqrt(ms + EPS) * g_ref[...]).astype(o_ref.dtype)


def _rmsnorm(x, g, tm):
    r, d = x.shape
    return pl.pallas_call(
        _rmsnorm_kernel,
        grid=(r // tm,),
        in_specs=[pl.BlockSpec((tm, d), lambda i: (i, 0)),
                  pl.BlockSpec((1, d), lambda i: (0, 0))],
        out_specs=pl.BlockSpec((tm, d), lambda i: (i, 0)),
        out_shape=jax.ShapeDtypeStruct((r, d), jnp.bfloat16),
        compiler_params=_params(("parallel",)),
        name="rmsnorm",
    )(x, g.reshape(1, d))


def _head_norm(z, g, d):
    sq = z * z
    if d == LANES:
        r = lax.rsqrt(jnp.sum(sq, axis=-1, keepdims=True) * (1.0 / d) + EPS)
    else:
        lo = lax.broadcasted_iota(jnp.int32, z.shape, 1) < d
        s_lo = jnp.sum(jnp.where(lo, sq, 0.0), axis=-1, keepdims=True)
        s_hi = jnp.sum(jnp.where(lo, 0.0, sq), axis=-1, keepdims=True)
        r = jnp.where(lo, lax.rsqrt(s_lo * (1.0 / d) + EPS), lax.rsqrt(s_hi * (1.0 / d) + EPS))
    return z * r * g


def _rope(z, c, a, b, half):
    return z * c + pltpu.roll(z, LANES - half, 1) * a + pltpu.roll(z, half, 1) * b


def _proj_kernel(xn_ref, w_ref, g_ref, rc_ref, ra_ref, rb_ref, *out_refs, norm, rope, scale, kinds):
    z = jnp.dot(xn_ref[...], w_ref[...], preferred_element_type=jnp.float32)
    for c in range(GROUPS):
        zc = z[:, c * LANES:(c + 1) * LANES]
        if norm:
            zc = _head_norm(zc, g_ref[...], norm)
        if rope:
            zc = _rope(zc, rc_ref[...], ra_ref[...], rb_ref[...], rope // 8)
        for kind, o_ref in zip(kinds, out_refs):
            if kind == "f32":
                o_ref[:, c * LANES:(c + 1) * LANES] = zc
            else:
                o_ref[c] = (zc * scale).astype(o_ref.dtype)


def _proj(xn, w, gain, tables, *, norm, rope, scale, kinds, tm):
    r, d = xn.shape
    nj = w.shape[1] // SEC
    rc, ra, rb = tables
    out_shape, out_specs = [], []
    for kind in kinds:
        if kind == "f32":
            out_shape.append(jax.ShapeDtypeStruct((r, nj * SEC), jnp.float32))
            out_specs.append(pl.BlockSpec((tm, SEC), lambda j, i: (i, j)))
        else:
            assert nj == 1
            out_shape.append(jax.ShapeDtypeStruct((GROUPS, r, LANES), jnp.bfloat16))
            out_specs.append(pl.BlockSpec((GROUPS, tm, LANES), lambda j, i: (0, i, 0)))
    row = pl.BlockSpec((tm, LANES), lambda j, i: (i, 0))
    outs = pl.pallas_call(
        functools.partial(_proj_kernel, norm=norm, rope=rope, scale=scale, kinds=tuple(kinds)),
        grid=(nj, r // tm),
        in_specs=[pl.BlockSpec((tm, d), lambda j, i: (i, 0)),
                  pl.BlockSpec((d, SEC), lambda j, i: (0, j)),
                  pl.BlockSpec((1, LANES), lambda j, i: (0, 0)),
                  row, row, row],
        out_specs=out_specs,
        out_shape=out_shape,
        compiler_params=_params(("parallel", "parallel")),
        name="proj",
    )(xn, w, gain, rc, ra, rb)
    return outs


def _small_proj_kernel(xn_ref, w_ref, g_ref, rc_ref, ra_ref, rb_ref, f_ref, k2_ref, *, wscale):
    z = jnp.dot(xn_ref[...], w_ref[...], preferred_element_type=jnp.float32)
    lo = lax.broadcasted_iota(jnp.int32, z.shape, 1) < IDX_DIM
    ms = jnp.sum(jnp.where(lo, z * z, 0.0), axis=-1, keepdims=True) * (1.0 / IDX_DIM)
    zn = z * lax.rsqrt(ms + EPS) * g_ref[...]
    zr = _rope(zn, rc_ref[...], ra_ref[...], rb_ref[...], IDX_DIM // 8)
    f_ref[...] = jnp.where(lo, zr, z * wscale)
    k2_ref[...] = jnp.where(lo, zr, pltpu.roll(zr, IDX_DIM, 1)).astype(k2_ref.dtype)


def _small_proj(xn, w, gain, tables, tm):
    r, d = xn.shape
    rc, ra, rb = tables
    row = pl.BlockSpec((tm, LANES), lambda i: (i, 0))
    return pl.pallas_call(
        functools.partial(_small_proj_kernel, wscale=IDX_HEADS ** -0.5),
        grid=(r // tm,),
        in_specs=[pl.BlockSpec((tm, d), lambda i: (i, 0)),
                  pl.BlockSpec((d, LANES), lambda i: (0, 0)),
                  pl.BlockSpec((1, LANES), lambda i: (0, 0)),
                  row, row, row],
        out_specs=[row, row],
        out_shape=[jax.ShapeDtypeStruct((r, LANES), jnp.float32),
                   jax.ShapeDtypeStruct((r, LANES), jnp.bfloat16)],
        compiler_params=_params(("parallel",)),
        name="small_proj",
    )(xn, w, gain, rc, ra, rb)


def _rope_tables(pos, d):
    r = d // 4
    h = r // 2
    n = pos.shape[0]
    inv = ROPE_THETA ** (-jnp.arange(0, r, 2, dtype=jnp.float32) / r)
    ang = pos.astype(jnp.float32)[:, None] * inv[None, :]
    cos, sin = jnp.cos(ang), jnp.sin(ang)
    zh = jnp.zeros((n, h), jnp.float32)
    zr = jnp.zeros((n, d - r), jnp.float32)
    c = jnp.concatenate([cos, cos, jnp.ones((n, d - r), jnp.float32)], axis=-1)
    a = jnp.concatenate([-sin, zh, zr], axis=-1)
    b = jnp.concatenate([zh, sin, zr], axis=-1)
    rep = LANES // d
    return tuple(jnp.tile(t, (1, rep)) for t in (c, a, b))


def _online_update(s, v, m_ref, l_ref, acc_ref):
    m_prev = m_ref[...]
    m_new = jnp.maximum(m_prev, jnp.max(s, axis=-1, keepdims=True))
    alpha = jnp.exp2(m_prev - m_new)
    p = jnp.exp2(s - m_new)
    l_ref[...] = alpha * l_ref[...] + jnp.sum(p, axis=-1, keepdims=True)
    acc_ref[...] = alpha * acc_ref[...] + jnp.dot(p.astype(v.dtype), v, preferred_element_type=jnp.float32)
    m_ref[...] = m_new


def _causal_steps(t, tq, tk):
    qs, ks = [], []
    for qi in range(t // tq):
        nk = _cdiv((qi + 1) * tq, tk)
        qs += [qi] * nk
        ks += list(range(nk))
    return np.asarray(qs, np.int32), np.asarray(ks, np.int32)


def _unshifted_ok(gq, gk, d, scale):
    bound = jnp.max(jnp.abs(gq)) * jnp.max(jnp.abs(gk)) * (d * scale * (1.0 + 2.0 ** -6))
    return (bound <= MAX_UNSHIFTED).astype(jnp.int32).reshape(1)


def _lambda(lam_ref, lam_init):
    lp = lam_ref[...]
    s1 = jnp.sum(lp[0:1] * lp[1:2], axis=-1, keepdims=True)
    s2 = jnp.sum(lp[2:3] * lp[3:4], axis=-1, keepdims=True)
    return jnp.exp(s1) - jnp.exp(s2) + lam_init


def _diff_finish(o0, o1, lam, sg, lam_init):
    o = o0 - lam * o1
    ms = jnp.mean(o * o, axis=-1, keepdims=True)
    return o * lax.rsqrt(ms + EPS) * sg * (1.0 - lam_init)


def _chunk_end(pos):
    return (pos // CHUNK + 1) * CHUNK - 1


def _diff_attn_kernel(qt_ref, kt_ref, ok_ref, q_ref, k_ref, v_ref, lam_ref, sg_ref, o_ref,
                      qs_ref, v1_ref, acc_ref, m_ref, l_ref, *, tq, tk, lam_init):
    step_id = pl.program_id(1)
    qi = qt_ref[step_id]
    kj = kt_ref[step_id]
    nk = _cdiv((qi + 1) * tq, tk)
    unshifted = ok_ref[0] == 1
    full = (kj + 1) * tk <= qi * tq + CHUNK
    rows2 = 2 * tq

    @pl.when(kj == 0)
    def _():
        q = q_ref[...]
        lo = lax.broadcasted_iota(jnp.int32, q.shape, 1) < DH_A
        zero = jnp.zeros_like(q)
        qs_ref[0:tq, :] = jnp.where(lo, q, zero)
        qs_ref[tq:rows2, :] = jnp.where(lo, zero, q)
        v1_ref[:, DV_A:] = jnp.ones((tk, LANES), v1_ref.dtype)
        m_ref[...] = jnp.full(m_ref.shape, NEG, jnp.float32)
        l_ref[...] = jnp.zeros(l_ref.shape, jnp.float32)
        acc_ref[...] = jnp.zeros(acc_ref.shape, jnp.float32)

    v1_ref[:, :DV_A] = v_ref[...]

    def visible(r0, nrows):
        row = r0 + lax.broadcasted_iota(jnp.int32, (nrows, tk), 0)
        row = jnp.where(row >= tq, row - tq, row)
        kpos = kj * tk + lax.broadcasted_iota(jnp.int32, (nrows, tk), 1)
        return kpos <= _chunk_end(qi * tq + row)

    def fast_step(masked):
        k = k_ref[...]
        v1 = v1_ref[...]
        for r0 in range(0, rows2, ROW_GROUP):
            rows = slice(r0, r0 + ROW_GROUP)
            s = lax.dot_general(qs_ref[rows, :], k, _NT, preferred_element_type=jnp.float32)
            if masked:
                s = jnp.where(visible(r0, ROW_GROUP), s, NEG)
            p = jnp.exp2(s).astype(v1.dtype)
            acc_ref[rows, :] += jnp.dot(p, v1, preferred_element_type=jnp.float32)

    def exact_step(masked):
        s = lax.dot_general(qs_ref[...], k_ref[...], _NT, preferred_element_type=jnp.float32)
        if masked:
            s = jnp.where(visible(0, rows2), s, NEG)
        _online_update(s, v_ref[...], m_ref, l_ref, acc_ref.at[:, :DV_A])

    for fast, step in ((True, fast_step), (False, exact_step)):
        mode = unshifted if fast else jnp.logical_not(unshifted)

        @pl.when(jnp.logical_and(mode, full))
        def _():
            step(False)

        @pl.when(jnp.logical_and(mode, jnp.logical_not(full)))
        def _():
            step(True)

    @pl.when(kj == nk - 1)
    def _():
        @pl.when(jnp.logical_not(unshifted))
        def _():
            acc_ref[:, DV_A:] = jnp.broadcast_to(l_ref[...], (rows2, LANES))

        o = acc_ref[:, :DV_A] / acc_ref[:, DV_A:]
        o_ref[...] = _diff_finish(o[0:tq], o[tq:rows2], _lambda(lam_ref, lam_init), sg_ref[...], lam_init)


def _diff_attn(q, k, v, ok, lam_p, sg, lam_init, tq, tk):
    _, t, _ = q.shape
    qt, kt = _causal_steps(t, tq, tk)
    kv_spec = pl.BlockSpec((None, tk, LANES), lambda h, s, qt, kt, ok: (h, kt[s], 0))
    const = lambda h, s, qt, kt, ok: (0, 0)
    grid_spec = pltpu.PrefetchScalarGridSpec(
        num_scalar_prefetch=3,
        grid=(H_A, len(qt)),
        in_specs=[pl.BlockSpec((None, tq, LANES), lambda h, s, qt, kt, ok: (h, qt[s], 0)),
                  kv_spec, kv_spec,
                  pl.BlockSpec((4, DH_A), const),
                  pl.BlockSpec((1, DV_A), const)],
        out_specs=pl.BlockSpec((tq, LANES), lambda h, s, qt, kt, ok: (qt[s], h)),
        scratch_shapes=[pltpu.VMEM((2 * tq, LANES), jnp.bfloat16),
                        pltpu.VMEM((tk, DV_A + LANES), jnp.bfloat16),
                        pltpu.VMEM((2 * tq, DV_A + LANES), jnp.float32),
                        pltpu.VMEM((2 * tq, 1), jnp.float32),
                        pltpu.VMEM((2 * tq, 1), jnp.float32)])
    return pl.pallas_call(
        functools.partial(_diff_attn_kernel, tq=tq, tk=tk, lam_init=lam_init),
        grid_spec=grid_spec,
        out_shape=jax.ShapeDtypeStruct((t, H_A * DV_A), jnp.float32),
        compiler_params=_params(("parallel", "arbitrary")),
        name="diff_attn",
    )(jnp.asarray(qt), jnp.asarray(kt), ok, q, k, v, lam_p, sg)


def _sort_key(score):
    bits = pltpu.bitcast(score, jnp.int32)
    key = jnp.where(bits < 0, bits ^ INT_MAX, bits)
    return jnp.where(key == -1, 0, key)


def _index_kernel(qt_ref, kt_ref, q_ref, w_ref, k_ref, bias_ref, qs_ref, key_ref, p_ref, *, tq, tk, t, topk):
    step_id = pl.program_id(0)
    qi = qt_ref[step_id]
    kj = kt_ref[step_id]
    nk = _cdiv((qi + 1) * tq, tk)
    nsub = tk // LANES

    @pl.when(kj == 0)
    def _():
        for p in range(GROUPS):
            q = q_ref[p]
            lo = lax.broadcasted_iota(jnp.int32, q.shape, 1) < IDX_DIM
            zero = jnp.zeros_like(q)
            qs_ref[(2 * p) * tq:(2 * p + 1) * tq, :] = jnp.where(lo, q, zero)
            qs_ref[(2 * p + 1) * tq:(2 * p + 2) * tq, :] = jnp.where(lo, zero, q)

    logits = lax.dot_general(qs_ref[...], k_ref[...], _NT, preferred_element_type=jnp.float32)
    w = w_ref[...]
    score = None
    for h in range(IDX_HEADS):
        term = jnp.maximum(logits[h * tq:(h + 1) * tq], 0.0) * w[:, h:h + 1]
        score = term if score is None else score + term
    qpos = qi * tq + lax.broadcasted_iota(jnp.int32, score.shape, 0)
    kpos = kj * tk + lax.broadcasted_iota(jnp.int32, score.shape, 1)
    key = jnp.where(kpos <= _chunk_end(qpos), _sort_key(score), INT_MIN)
    key_ref[:, pl.ds(pl.multiple_of(kj * tk, tk), tk)] = key

    @pl.when(kj == nk - 1)
    def _():
        lane = lax.broadcasted_iota(jnp.int32, (tq, LANES), 1)
        cend = _chunk_end(qi * tq + lax.broadcasted_iota(jnp.int32, (tq, LANES), 0))

        def rows(x):
            return jnp.broadcast_to(x, (tq, LANES))

        def sweep(fn, init):
            def body(j, acc):
                for c in range(nsub):
                    off = pl.multiple_of(j * tk + c * LANES, LANES)
                    acc = fn(acc, key_ref[:, pl.ds(off, LANES)], off + lane, c % 2)
                return acc
            return lax.fori_loop(0, nk, body, init)

        def count(pred):
            acc = sweep(lambda a, kt, idx, par: a + jnp.where(pred(kt, idx), 1.0, 0.0),
                        jnp.zeros((tq, LANES), jnp.float32))
            return rows(jnp.sum(acc, axis=1, keepdims=True))

        def key_score(key):
            f = pltpu.bitcast(jnp.where(key < 0, key ^ INT_MAX, key), jnp.float32)
            return jnp.where(key == INT_MIN, -jnp.inf, f)

        kf = float(topk)
        ninf = jnp.full((tq, LANES), -jnp.inf, jnp.float32)

        def group_max(acc, kt, idx, par):
            f = key_score(kt)
            return (jnp.maximum(acc[0], f), acc[1]) if par == 0 else (acc[0], jnp.maximum(acc[1], f))

        m0, m1 = sweep(group_max, (ninf, ninf))
        lo = _sort_key(rows(jnp.min(jnp.minimum(m0, m1), axis=1, keepdims=True)))
        hi = _sort_key(rows(jnp.max(jnp.maximum(m0, m1), axis=1, keepdims=True))) + 1
        c_lo = count(lambda kt, idx: kt >= lo)
        c_hi = jnp.zeros((tq, LANES), jnp.float32)
        all_sel = cend + 1 <= topk
        lo = jnp.where(all_sel, INT_MIN + 1, lo)
        done = jnp.where(jnp.logical_or(all_sel, c_lo == kf), 1.0, 0.0)

        def search_cond(st):
            return jnp.logical_and(st[0] < 80, st[6] < 0.5)

        def search_body(st):
            it, lo, hi, c_lo, c_hi, done, _ = st
            mid = (lo >> 1) + (hi >> 1) + (lo & hi & 1)
            lo_f, hi_f = key_score(lo), key_score(hi)
            frac = (jnp.log(c_lo) - math.log(kf)) / (jnp.log(c_lo) - jnp.log(jnp.maximum(c_hi, 0.5)))
            t_f = lo_f + (hi_f - lo_f) * frac
            interp = jnp.logical_and((it & 1) == 0, jnp.abs(t_f) < jnp.inf)
            t = jnp.where(interp, _sort_key(t_f), mid)
            t = jnp.minimum(jnp.maximum(t, lo + 1), hi - 1)
            c = count(lambda kt, idx: kt >= t)
            up = jnp.logical_and(done < 0.5, c >= kf)
            dn = jnp.logical_and(done < 0.5, c < kf)
            lo, c_lo = jnp.where(up, t, lo), jnp.where(up, c, c_lo)
            hi, c_hi = jnp.where(dn, t, hi), jnp.where(dn, c, c_hi)
            done = jnp.where(jnp.logical_or(c_lo == kf, hi == lo + 1), 1.0, done)
            return it + 1, lo, hi, c_lo, c_hi, done, jnp.min(done)

        st = (jnp.int32(0), lo, hi, c_lo, c_hi, done, jnp.min(done))
        _, thr, hi, c_lo, c_hi, _, _ = lax.while_loop(search_cond, search_body, st)
        tied = jnp.logical_and(jnp.logical_not(all_sel), c_lo > kf)
        need = jnp.where(tied, kf - c_hi, jnp.inf)
        p_ref[...] = jnp.full((tq, LANES), INT_MAX, jnp.int32)

        @pl.when(jnp.max(jnp.where(tied, 1.0, 0.0)) > 0.5)
        def _():
            nbits = max(1, (t - 1).bit_length())

            def pos_body(b, pos):
                cand = pos | lax.shift_left(jnp.int32(1), jnp.int32(nbits - 1) - b)
                g = count(lambda kt, idx: jnp.logical_and(kt == thr, idx < cand))
                return jnp.where(g < need, cand, pos)

            p_ref[...] = lax.fori_loop(0, nbits, pos_body, jnp.zeros((tq, LANES), jnp.int32))

        last = p_ref[...]

        def write_body(j, carry):
            for c in range(nsub):
                off = pl.multiple_of(j * tk + c * LANES, LANES)
                kt = key_ref[:, pl.ds(off, LANES)]
                idx = off + lane
                sel = jnp.logical_or(kt > thr, jnp.logical_and(kt == thr, idx <= last))
                sel = jnp.logical_and(sel, idx <= cend)
                bias_ref[:, pl.ds(off, LANES)] = jnp.where(sel, 0.0, NEG).astype(bias_ref.dtype)
            return carry

        lax.fori_loop(0, nk, write_body, 0)

        def fill_body(j, carry):
            off = pl.multiple_of(j * tk, tk)
            bias_ref[:, pl.ds(off, tk)] = jnp.full((tq, tk), NEG, bias_ref.dtype)
            return carry

        lax.fori_loop(nk, t // tk, fill_body, 0)


def _index_select(qi16, wi, ki2, topk, tq, tk):
    _, t, _ = qi16.shape
    qt, kt = _causal_steps(t, tq, tk)
    grid_spec = pltpu.PrefetchScalarGridSpec(
        num_scalar_prefetch=2,
        grid=(len(qt),),
        in_specs=[pl.BlockSpec((GROUPS, tq, LANES), lambda s, qt, kt: (0, qt[s], 0)),
                  pl.BlockSpec((tq, IDX_HEADS), lambda s, qt, kt: (qt[s], 0)),
                  pl.BlockSpec((tk, LANES), lambda s, qt, kt: (kt[s], 0))],
        out_specs=pl.BlockSpec((tq, t), lambda s, qt, kt: (qt[s], 0)),
        scratch_shapes=[pltpu.VMEM((IDX_HEADS * tq, LANES), jnp.bfloat16),
                        pltpu.VMEM((tq, t), jnp.int32),
                        pltpu.VMEM((tq, LANES), jnp.int32)])
    return pl.pallas_call(
        functools.partial(_index_kernel, tq=tq, tk=tk, t=t, topk=topk),
        grid_spec=grid_spec,
        out_shape=jax.ShapeDtypeStruct((t, t), jnp.bfloat16),
        compiler_params=_params(("arbitrary",)),
        name="index_select",
    )(jnp.asarray(qt), jnp.asarray(kt), qi16, wi, ki2)


def _dsa_attn_kernel(qt_ref, kt_ref, ok_ref, q_ref, k_ref, v_ref, bias_ref, o_ref,
                     v1_ref, acc_ref, m_ref, l_ref, *, tq, tk):
    step_id = pl.program_id(0)
    qi = qt_ref[step_id]
    kj = kt_ref[step_id]
    nk = _cdiv((qi + 1) * tq, tk)
    unshifted = ok_ref[0] == 1

    @pl.when(kj == 0)
    def _():
        v1_ref[:, :, DH_B:] = jnp.ones((H_B, tk, LANES), v1_ref.dtype)
        m_ref[...] = jnp.full(m_ref.shape, NEG, jnp.float32)
        l_ref[...] = jnp.zeros(l_ref.shape, jnp.float32)
        acc_ref[...] = jnp.zeros(acc_ref.shape, jnp.float32)

    @pl.when(unshifted)
    def _():
        v1_ref[:, :, :DH_B] = v_ref[...]
        for r0 in range(0, tq, ROW_GROUP):
            rows = slice(r0, r0 + ROW_GROUP)
            bias = bias_ref[rows, :].astype(jnp.float32)
            for h in range(H_B):
                s = lax.dot_general(q_ref[h, rows, :], k_ref[h], _NT, preferred_element_type=jnp.float32) + bias
                p = jnp.exp2(s).astype(v1_ref.dtype)
                acc_ref[h, rows, :] += jnp.dot(p, v1_ref[h], preferred_element_type=jnp.float32)

    @pl.when(jnp.logical_not(unshifted))
    def _():
        bias = bias_ref[...].astype(jnp.float32)
        for h in range(H_B):
            s = lax.dot_general(q_ref[h], k_ref[h], _NT, preferred_element_type=jnp.float32) + bias
            _online_update(s, v_ref[h], m_ref.at[h], l_ref.at[h], acc_ref.at[h, :, :DH_B])

    @pl.when(kj == nk - 1)
    def _():
        @pl.when(jnp.logical_not(unshifted))
        def _():
            acc_ref[:, :, DH_B:] = jnp.broadcast_to(l_ref[...], (H_B, tq, LANES))

        for h in range(H_B):
            o_ref[:, h * LANES:(h + 1) * LANES] = acc_ref[h, :, :DH_B] / acc_ref[h, :, DH_B:]


def _dsa_attn(q, k, v, bias, ok, tq, tk):
    _, t, _ = q.shape
    qt, kt = _causal_steps(t, tq, tk)
    kv_spec = pl.BlockSpec((H_B, tk, LANES), lambda s, qt, kt, ok: (0, kt[s], 0))
    grid_spec = pltpu.PrefetchScalarGridSpec(
        num_scalar_prefetch=3,
        grid=(len(qt),),
        in_specs=[pl.BlockSpec((H_B, tq, LANES), lambda s, qt, kt, ok: (0, qt[s], 0)),
                  kv_spec, kv_spec,
                  pl.BlockSpec((tq, tk), lambda s, qt, kt, ok: (qt[s], kt[s]))],
        out_specs=pl.BlockSpec((tq, H_B * DH_B), lambda s, qt, kt, ok: (qt[s], 0)),
        scratch_shapes=[pltpu.VMEM((H_B, tk, DH_B + LANES), jnp.bfloat16),
                        pltpu.VMEM((H_B, tq, DH_B + LANES), jnp.float32),
                        pltpu.VMEM((H_B, tq, 1), jnp.float32),
                        pltpu.VMEM((H_B, tq, 1), jnp.float32)])
    return pl.pallas_call(
        functools.partial(_dsa_attn_kernel, tq=tq, tk=tk),
        grid_spec=grid_spec,
        out_shape=jax.ShapeDtypeStruct((t, H_B * DH_B), jnp.float32),
        compiler_params=_params(("arbitrary",)),
        name="dsa_attn",
    )(jnp.asarray(qt), jnp.asarray(kt), ok, q, k, v, bias)


def _sample_kernel(qa_ref, qb_ref, qi_ref, wi_ref, ckd_ref, cvd_ref, ckb_ref, cvb_ref, cki_ref,
                   nkd_ref, nvd_ref, nkb_ref, nvb_ref, nki_ref, lam_ref, sg_ref,
                   oa_ref, ob_ref,
                   qda_ref, qdb_ref, bias_ref, key_ref, ma_ref, la_ref, acca_ref, mb_ref, lb_ref, accb_ref,
                   *, ts, past, tk, topk, lam_init):
    kj = pl.program_id(1)
    nkt = past // tk
    width = past + LANES
    ra = 2 * H_A * ts
    rb = H_B * ts

    def new_key_valid(rows):
        t_q = lax.broadcasted_iota(jnp.int32, (rows, LANES), 0) % ts
        lane = lax.broadcasted_iota(jnp.int32, (rows, LANES), 1)
        return jnp.logical_and(lane < ts, past + lane <= _chunk_end(past + t_q))

    @pl.when(kj == 0)
    def _():
        qa = qa_ref[...]
        grp = lax.broadcasted_iota(jnp.int32, qa.shape, 1) // DH_A
        for hc in range(2 * H_A):
            qda_ref[hc * ts:(hc + 1) * ts, :] = jnp.where(grp == hc, qa, jnp.zeros_like(qa))
        qb = qb_ref[...]
        grp = lax.broadcasted_iota(jnp.int32, qb.shape, 1) // DH_B
        for h in range(H_B):
            qdb_ref[h * ts:(h + 1) * ts, :] = jnp.where(grp == h, qb, jnp.zeros_like(qb))
        for ref in (ma_ref, mb_ref):
            ref[...] = jnp.full(ref.shape, NEG, jnp.float32)
        for ref in (la_ref, lb_ref, acca_ref, accb_ref):
            ref[...] = jnp.zeros(ref.shape, jnp.float32)

        qi = qi_ref[...]
        w = wi_ref[...]

        def scores(keys16):
            logits = lax.dot_general(qi, keys16, _NT, preferred_element_type=jnp.float32)
            score = None
            for h in range(IDX_HEADS):
                term = jnp.maximum(logits[h * ts:(h + 1) * ts], 0.0) * w[h * ts:(h + 1) * ts]
                score = term if score is None else score + term
            return score

        step = min(past, 1024)
        for c in range(past // step):
            keys16 = cki_ref[c * step:(c + 1) * step, :].astype(jnp.bfloat16)
            key_ref[:, c * step:(c + 1) * step] = _sort_key(scores(keys16))
        key_ref[:, past:width] = jnp.where(new_key_valid(ts), _sort_key(scores(nki_ref[...])), INT_MIN)

        kf = float(topk)

        def count(pred):
            idx = lax.broadcasted_iota(jnp.int32, (ts, width), 1)
            c = jnp.sum(jnp.where(pred(key_ref[...], idx), 1.0, 0.0), axis=1, keepdims=True)
            return jnp.broadcast_to(c, (ts, LANES))

        def wide(x):
            return x[:, 0:1]

        c0 = count(lambda kt, idx: kt >= 0)
        prefix = jnp.where(c0 >= kf, 0, INT_MIN).astype(jnp.int32)

        def bit_body(b, prefix):
            cand = prefix | lax.shift_left(jnp.int32(1), jnp.int32(30) - b)
            c = count(lambda kt, idx: kt >= wide(cand))
            return jnp.where(c >= kf, cand, prefix)

        thr = lax.fori_loop(0, 31, bit_body, prefix)
        n_gt = count(lambda kt, idx: kt > wide(thr))
        need = kf - n_gt
        nbits = max(1, (width - 1).bit_length())

        def pos_body(b, pos):
            cand = pos | lax.shift_left(jnp.int32(1), jnp.int32(nbits - 1) - b)
            g = count(lambda kt, idx: jnp.logical_and(kt == wide(thr), idx < wide(cand)))
            return jnp.where(g < need, cand, pos)

        last = lax.fori_loop(0, nbits, pos_body, jnp.zeros((ts, LANES), jnp.int32))
        kt = key_ref[...]
        idx = lax.broadcasted_iota(jnp.int32, (ts, width), 1)
        sel = jnp.logical_or(kt > wide(thr), jnp.logical_and(kt == wide(thr), idx <= wide(last)))
        sel = jnp.logical_and(sel, kt != INT_MIN)
        bias = jnp.where(sel, 0.0, NEG)
        for h in range(H_B):
            bias_ref[h * ts:(h + 1) * ts, :] = bias

    def head_rows(ref):
        return jnp.concatenate([ref[pl.ds(h, tk, stride=GROUPS), :].astype(jnp.bfloat16) for h in range(GROUPS)],
                               axis=1)

    sa = lax.dot_general(qda_ref[...], head_rows(ckd_ref), _NT, preferred_element_type=jnp.float32)
    _online_update(sa, head_rows(cvd_ref), ma_ref, la_ref, acca_ref)
    sb = lax.dot_general(qdb_ref[...], head_rows(ckb_ref), _NT, preferred_element_type=jnp.float32)
    sb = sb + bias_ref[:, pl.ds(pl.multiple_of(kj * tk, tk), tk)]
    _online_update(sb, head_rows(cvb_ref), mb_ref, lb_ref, accb_ref)

    @pl.when(kj == nkt - 1)
    def _():
        sa = lax.dot_general(qda_ref[...], nkd_ref[...], _NT, preferred_element_type=jnp.float32)
        sa = jnp.where(new_key_valid(ra), sa, NEG)
        _online_update(sa, nvd_ref[...], ma_ref, la_ref, acca_ref)
        sb = lax.dot_general(qdb_ref[...], nkb_ref[...], _NT, preferred_element_type=jnp.float32)
        sb = sb + bias_ref[:, past:width]
        _online_update(sb, nvb_ref[...], mb_ref, lb_ref, accb_ref)

        lam = _lambda(lam_ref, lam_init)
        oa = acca_ref[...] / la_ref[...]
        ob = accb_ref[...] / lb_ref[...]
        for h in range(H_A):
            cols = slice(h * DV_A, (h + 1) * DV_A)
            o0 = oa[(2 * h) * ts:(2 * h + 1) * ts, cols]
            o1 = oa[(2 * h + 1) * ts:(2 * h + 2) * ts, cols]
            oa_ref[:, cols] = _diff_finish(o0, o1, lam, sg_ref[...], lam_init)
        for h in range(H_B):
            cols = slice(h * DH_B, (h + 1) * DH_B)
            ob_ref[:, cols] = ob[h * ts:(h + 1) * ts, cols]


def _sample_attn(l, qa, qb, qi, wi, caches, news, lam_p, sg, lam_init, topk, tk):
    bsz, ts, _ = qa.shape
    ckd, cvd, ckb, cvb, cki = caches
    past = cki.shape[2]
    width = past + LANES
    ra, rb = 2 * H_A * ts, H_B * ts
    heads_spec = pl.BlockSpec((None, None, tk * GROUPS, LANES), lambda b, kj: (l, b, kj, 0))
    new_spec = pl.BlockSpec((None, LANES, SEC), lambda b, kj: (b, 0, 0))
    row_spec = pl.BlockSpec((None, ts, SEC), lambda b, kj: (b, 0, 0))
    return pl.pallas_call(
        functools.partial(_sample_kernel, ts=ts, past=past, tk=tk, topk=topk, lam_init=lam_init),
        grid=(bsz, past // tk),
        in_specs=[row_spec, row_spec,
                  pl.BlockSpec((None, IDX_HEADS * ts, IDX_DIM), lambda b, kj: (b, 0, 0)),
                  pl.BlockSpec((None, IDX_HEADS * ts, 1), lambda b, kj: (b, 0, 0)),
                  heads_spec, heads_spec, heads_spec, heads_spec,
                  pl.BlockSpec((None, None, past, IDX_DIM), lambda b, kj: (l, b, 0, 0)),
                  new_spec, new_spec, new_spec, new_spec,
                  pl.BlockSpec((None, LANES, IDX_DIM), lambda b, kj: (b, 0, 0)),
                  pl.BlockSpec((4, DH_A), lambda b, kj: (0, 0)),
                  pl.BlockSpec((1, DV_A), lambda b, kj: (0, 0))],
        out_specs=[row_spec, row_spec],
        out_shape=[jax.ShapeDtypeStruct((bsz, ts, SEC), jnp.float32),
                   jax.ShapeDtypeStruct((bsz, ts, SEC), jnp.float32)],
        scratch_shapes=[pltpu.VMEM((ra, SEC), jnp.bfloat16),
                        pltpu.VMEM((rb, SEC), jnp.bfloat16),
                        pltpu.VMEM((rb, width), jnp.float32),
                        pltpu.VMEM((ts, width), jnp.int32),
                        pltpu.VMEM((ra, 1), jnp.float32),
                        pltpu.VMEM((ra, 1), jnp.float32),
                        pltpu.VMEM((ra, SEC), jnp.float32),
                        pltpu.VMEM((rb, 1), jnp.float32),
                        pltpu.VMEM((rb, 1), jnp.float32),
                        pltpu.VMEM((rb, SEC), jnp.float32)],
        compiler_params=_params(("parallel", "arbitrary")),
        name="sample_attn",
    )(qa, qb, qi, wi, ckd, cvd, ckb, cvb, cki, *news, lam_p, sg)


def _silu(x):
    return x * jax.nn.sigmoid(x)


def _merge_kernel(oa_ref, ga_ref, ob_ref, gb_ref, mg_ref, wa_ref, wb_ref, u_ref):
    d = u_ref.shape[-1]
    ya = jnp.dot((oa_ref[...] * _silu(ga_ref[...])).astype(jnp.bfloat16), wa_ref[...],
                 preferred_element_type=jnp.float32)
    yb = jnp.dot((ob_ref[...] * _silu(gb_ref[...])).astype(jnp.bfloat16), wb_ref[...],
                 preferred_element_type=jnp.float32)
    mg = mg_ref[...]
    u = jax.nn.sigmoid(mg[:, :d]) * ya + jax.nn.sigmoid(mg[:, d:]) * yb
    u_ref[...] = u.astype(u_ref.dtype)


def _resid_kernel(x_ref, u_ref, wo_ref, g_ref, h_ref, hn_ref):
    h = x_ref[...] + jnp.dot(u_ref[...], wo_ref[...], preferred_element_type=jnp.float32)
    h_ref[...] = h
    ms = jnp.mean(h * h, axis=-1, keepdims=True)
    hn_ref[...] = (h * lax.rsqrt(ms + EPS) * g_ref[...]).astype(hn_ref.dtype)


def _ple_kernel(h_ref, hn_ref, p_ref, wg_ref, wp_ref, o_ref):
    gate = jax.nn.sigmoid(jnp.dot(hn_ref[...], wg_ref[...], preferred_element_type=jnp.float32))
    e = jnp.dot(p_ref[...].astype(jnp.bfloat16), wp_ref[...], preferred_element_type=jnp.float32)
    o_ref[...] = h_ref[...] + gate * e


def _layer_output(x, p, oa, ga, ob, gb, mg, w_ba, w_bb, w_out, ple_g, w_pg, w_ple, tm):
    r, d = x.shape
    wa = oa.shape[1]
    dp = p.shape[1]
    rows = lambda n: pl.BlockSpec((tm, n), lambda i: (i, 0))
    full = lambda a, b: pl.BlockSpec((a, b), lambda i: (0, 0))
    u = pl.pallas_call(
        _merge_kernel,
        grid=(r // tm,),
        in_specs=[rows(wa), rows(wa), rows(wa), rows(wa), rows(2 * d), full(wa, d), full(wa, d)],
        out_specs=rows(d),
        out_shape=jax.ShapeDtypeStruct((r, d), jnp.bfloat16),
        compiler_params=_params(("parallel",)),
        name="merge",
    )(oa, ga, ob, gb, mg, w_ba, w_bb)
    h, hn = pl.pallas_call(
        _resid_kernel,
        grid=(r // tm,),
        in_specs=[rows(d), rows(d), full(d, d), full(1, d)],
        out_specs=[rows(d), rows(d)],
        out_shape=[jax.ShapeDtypeStruct((r, d), jnp.float32), jax.ShapeDtypeStruct((r, d), jnp.bfloat16)],
        compiler_params=_params(("parallel",)),
        name="resid",
    )(x, u, w_out, ple_g)
    return pl.pallas_call(
        _ple_kernel,
        grid=(r // tm,),
        in_specs=[rows(d), rows(d), rows(dp), full(d, d), full(dp, d)],
        out_specs=rows(d),
        out_shape=jax.ShapeDtypeStruct((r, d), jnp.float32),
        compiler_params=_params(("parallel",)),
        name="ple",
    )(h, hn, p, w_pg, w_ple)


def _tile(n, pref):
    t = min(n, pref)
    while n % t:
        t //= 2
    return t


def _layer_inputs(x, pos, lw, tm):
    xn = _rmsnorm(x, lw["ln_g"], tm)
    t64 = _rope_tables(pos, DH_A)
    t128 = _rope_tables(pos, DH_B)
    none = t64
    one = jnp.ones((1, LANES), jnp.float32)
    w = lw["w_sec"]
    pj = functools.partial(_proj, xn, tm=tm)
    (qa16,) = pj(w["qa"], lw["qn_a"], t64, norm=DH_A, rope=DH_A, scale=DH_A ** -0.5 * LOG2E, kinds=["hm16"])
    ka, ka16 = pj(w["ka"], lw["kn_a"], t64, norm=DH_A, rope=DH_A, scale=1.0, kinds=["f32", "hm16"])
    va, va16 = pj(w["va"], one, none, norm=0, rope=0, scale=1.0, kinds=["f32", "hm16"])
    (ga,) = pj(w["ga"], one, none, norm=0, rope=0, scale=1.0, kinds=["f32"])
    (qb16,) = pj(w["qb"], lw["qn_b"], t128, norm=DH_B, rope=DH_B, scale=DH_B ** -0.5 * LOG2E, kinds=["hm16"])
    kb, kb16 = pj(w["kb"], lw["kn_b"], t128, norm=DH_B, rope=DH_B, scale=1.0, kinds=["f32", "hm16"])
    vb, vb16 = pj(w["vb"], one, none, norm=0, rope=0, scale=1.0, kinds=["f32", "hm16"])
    (gb,) = pj(w["gb"], one, none, norm=0, rope=0, scale=1.0, kinds=["f32"])
    (qi16,) = pj(w["qi"], one, t64, norm=0, rope=DH_A, scale=IDX_DIM ** -0.5, kinds=["hm16"])
    (mg,) = pj(w["mg"], one, none, norm=0, rope=0, scale=1.0, kinds=["f32"])
    small, ki2 = _small_proj(xn, w["small"], lw["kn_i"], t64, tm)
    ki = small[:, :IDX_DIM]
    wi = small[:, IDX_DIM:IDX_DIM + IDX_HEADS]
    return dict(qa16=qa16, ka=ka, ka16=ka16, va=va, va16=va16, ga=ga, qb16=qb16, kb=kb, kb16=kb16,
                vb=vb, vb16=vb16, gb=gb, qi16=qi16, mg=mg, ki=ki, ki2=ki2, wi=wi)


def _row_major(hm):
    g, r, n = hm.shape
    return jnp.transpose(hm, (1, 0, 2)).reshape(r, g * n)


def _pad_new(a, bsz, ts):
    a = a.reshape(bsz, ts, a.shape[-1]).astype(jnp.bfloat16)
    return jnp.pad(a, ((0, 0), (0, LANES - ts), (0, 0)))


def kernel(x_prompt, x_sample, p_prompt, p_sample, cache_diff_k, cache_diff_v, cache_dsa_k, cache_dsa_v, cache_idx_k, ln_g, w_in, q_norm_a, k_norm_a, lam_q1, lam_k1, lam_q2, lam_k2, subln_a, q_norm_b, k_norm_b, k_norm_idx, w_branch_a, w_branch_b, w_out, ple_norm, w_ple_gate, w_ple):
    depth = w_in.shape[0]
    bp, t_p, d_model = x_prompt.shape
    bsz, t_s, _ = x_sample.shape
    past = cache_diff_k.shape[2]
    assert bp == 1 and t_s <= LANES and t_p % LANES == 0 and past % LANES == 0
    topk_p = min(TOPK_MAX, t_p // 4)
    topk_s = min(TOPK_MAX, (past + t_s) // 4)
    pos_p = jnp.arange(t_p)
    pos_s = jnp.tile(past + jnp.arange(t_s), bsz)
    r_s = bsz * t_s

    tm_p = _tile(t_p, 512)
    tm_s = _tile(r_s, 512)
    tq_a, tk_a = _tile(t_p, 512), _tile(t_p, 1024)
    tq_i, tk_i = _tile(t_p, 128), _tile(t_p, 1024)
    tq_b, tk_b = _tile(t_p, 256), _tile(t_p, 1024)
    tk_s = _tile(past, 512)
    tm_o = _tile(t_p, 256)
    tm_os = _tile(r_s, 256)

    by_head = lambda c: c.reshape(depth, bsz, past * GROUPS, LANES)
    caches = (by_head(cache_diff_k), by_head(cache_diff_v), by_head(cache_dsa_k), by_head(cache_dsa_v), cache_idx_k)

    sizes = (SEC,) * 9 + (IDX_DIM, IDX_HEADS, 2 * d_model)
    names = ("qa", "ka", "va", "ga", "qb", "kb", "vb", "gb", "qi", "ki", "wi", "mg")
    offs = [0]
    for s in sizes:
        offs.append(offs[-1] + s)

    hp = x_prompt.reshape(t_p, d_model)
    hs = x_sample.reshape(r_s, d_model)
    outs_p = [[] for _ in range(5)]
    outs_s = [[] for _ in range(5)]
    for l in range(depth):
        lam_init = 0.8 - 0.6 * math.exp(-0.3 * l)
        wl = w_in[l].astype(jnp.bfloat16)
        w_sec = {n: wl[:, offs[i]:offs[i + 1]] for i, n in enumerate(names)}
        w_sec["small"] = jnp.pad(jnp.concatenate([w_sec.pop("ki"), w_sec.pop("wi")], axis=1),
                                 ((0, 0), (0, LANES - IDX_DIM - IDX_HEADS)))
        lane_gain = lambda g: jnp.tile(g.astype(jnp.float32), LANES // g.shape[0]).reshape(1, LANES)
        lw = dict(ln_g=ln_g[l], w_sec=w_sec, qn_a=lane_gain(q_norm_a[l]), kn_a=lane_gain(k_norm_a[l]),
                  qn_b=lane_gain(q_norm_b[l]), kn_b=lane_gain(k_norm_b[l]), kn_i=lane_gain(k_norm_idx[l]))
        ok_a = _unshifted_ok(q_norm_a[l], k_norm_a[l], DH_A, DH_A ** -0.5)
        ok_b = _unshifted_ok(q_norm_b[l], k_norm_b[l], DH_B, DH_B ** -0.5)
        lam_p = jnp.stack([lam_q1[l], lam_k1[l], lam_q2[l], lam_k2[l]]).astype(jnp.float32)
        sg = subln_a[l].reshape(1, DV_A).astype(jnp.float32)
        w_o = (w_branch_a[l].astype(jnp.bfloat16), w_branch_b[l].astype(jnp.bfloat16),
               w_out[l].astype(jnp.bfloat16), ple_norm[l].reshape(1, d_model),
               w_ple_gate[l].astype(jnp.bfloat16), w_ple[l].astype(jnp.bfloat16))

        a = _layer_inputs(hp, pos_p, lw, tm_p)
        oa = _diff_attn(a["qa16"], a["ka16"], a["va16"], ok_a, lam_p, sg, lam_init, tq_a, tk_a)
        bias = _index_select(a["qi16"], a["wi"], a["ki2"], topk_p, tq_i, tk_i)
        ob = _dsa_attn(a["qb16"], a["kb16"], a["vb16"], bias, ok_b, tq_b, tk_b)
        hp = _layer_output(hp, p_prompt[l].reshape(t_p, -1), oa, a["ga"], ob, a["gb"], a["mg"], *w_o, tm_o)
        for lst, key in zip(outs_p, ("ka", "va", "kb", "vb", "ki")):
            lst.append(a[key])

        s = _layer_inputs(hs, pos_s, lw, tm_s)
        qa_s = _row_major(s["qa16"]).reshape(bsz, t_s, SEC)
        qb_s = _row_major(s["qb16"]).reshape(bsz, t_s, SEC)
        qi_s = _row_major(s["qi16"]).reshape(bsz, t_s, IDX_HEADS, IDX_DIM)
        qi_s = jnp.transpose(qi_s, (0, 2, 1, 3)).reshape(bsz, IDX_HEADS * t_s, IDX_DIM)
        wi_s = jnp.transpose(s["wi"].reshape(bsz, t_s, IDX_HEADS), (0, 2, 1)).reshape(bsz, IDX_HEADS * t_s, 1)
        news = tuple(_pad_new(s[k], bsz, t_s) for k in ("ka", "va", "kb", "vb", "ki"))
        oa_s, ob_s = _sample_attn(l, qa_s, qb_s, qi_s, wi_s, caches, news, lam_p, sg, lam_init, topk_s, tk_s)
        hs = _layer_output(hs, p_sample[l].reshape(r_s, -1), oa_s.reshape(r_s, SEC), s["ga"],
                           ob_s.reshape(r_s, SEC), s["gb"], s["mg"], *w_o, tm_os)
        for lst, key in zip(outs_s, ("ka", "va", "kb", "vb", "ki")):
            lst.append(s[key])

    def stack(lst, shape):
        return jnp.stack(lst).reshape((depth,) + shape)

    return (hp.reshape(bp, t_p, d_model), hs.reshape(bsz, t_s, d_model),
            stack(outs_p[0], (bp, t_p, H_A, 2, DH_A)), stack(outs_p[1], (bp, t_p, H_A, DV_A)),
            stack(outs_p[2], (bp, t_p, H_B, DH_B)), stack(outs_p[3], (bp, t_p, H_B, DH_B)),
            stack(outs_p[4], (bp, t_p, IDX_DIM)),
            stack(outs_s[0], (bsz, t_s, H_A, 2, DH_A)), stack(outs_s[1], (bsz, t_s, H_A, DV_A)),
            stack(outs_s[2], (bsz, t_s, H_B, DH_B)), stack(outs_s[3], (bsz, t_s, H_B, DH_B)),
            stack(outs_s[4], (bsz, t_s, IDX_DIM)))
```

```python
import functools
import math

import jax
import jax.numpy as jnp
import numpy as np
from jax import lax
from jax.experimental import pallas as pl
from jax.experimental.pallas import tpu as pltpu

CHUNK = 64
ROPE_THETA = 500000.0
EPS = 1e-6
H_A = 8
DH_A = 64
DV_A = 128
H_B = 8
DH_B = 128
IDX_HEADS = 16
IDX_DIM = 64
TOPK_MAX = 256
LANES = 128
GROUPS = 8
SEC = GROUPS * LANES
ROW_GROUP = 128
NEG = -1e30
LOG2E = 1.4426950408889634
MAX_UNSHIFTED = 60.0
INT_MIN = -(2 ** 31)
INT_MAX = 2 ** 31 - 1
VMEM_LIMIT = 56 * 1024 * 1024

_NT = (((1,), (1,)), ((), ()))


def _cdiv(a, b):
    return (a + b - 1) // b


def _params(sem, vmem=VMEM_LIMIT):
    return pltpu.CompilerParams(dimension_semantics=sem, vmem_limit_bytes=vmem)


def _rmsnorm_kernel(x_ref, g_ref, o_ref):
    x = x_ref[...]
    ms = jnp.mean(x * x, axis=-1, keepdims=True)
    o_ref[...] = (x * lax.rsqrt(ms + EPS) * g_ref[...]).astype(o_ref.dtype)


def _rmsnorm(x, g, tm):
    r, d = x.shape
    return pl.pallas_call(
        _rmsnorm_kernel,
        grid=(r // tm,),
        in_specs=[pl.BlockSpec((tm, d), lambda i: (i, 0)),
                  pl.BlockSpec((1, d), lambda i: (0, 0))],
        out_specs=pl.BlockSpec((tm, d), lambda i: (i, 0)),
        out_shape=jax.ShapeDtypeStruct((r, d), jnp.bfloat16),
        compiler_params=_params(("parallel",)),
        name="rmsnorm",
    )(x, g.reshape(1, d))


def _head_norm(z, g, d):
    sq = z * z
    if d == LANES:
        r = lax.rsqrt(jnp.sum(sq, axis=-1, keepdims=True) * (1.0 / d) + EPS)
    else:
        lo = lax.broadcasted_iota(jnp.int32, z.shape, 1) < d
        s_lo = jnp.sum(jnp.where(lo, sq, 0.0), axis=-1, keepdims=True)
        s_hi = jnp.sum(jnp.where(lo, 0.0, sq), axis=-1, keepdims=True)
        r = jnp.where(lo, lax.rsqrt(s_lo * (1.0 / d) + EPS), lax.rsqrt(s_hi * (1.0 / d) + EPS))
    return z * r * g


def _rope(z, c, a, b, half):
    return z * c + pltpu.roll(z, LANES - half, 1) * a + pltpu.roll(z, half, 1) * b


def _proj_kernel(xn_ref, w_ref, g_ref, rc_ref, ra_ref, rb_ref, *out_refs, norm, rope, scale, kinds):
    z = jnp.dot(xn_ref[...], w_ref[...], preferred_element_type=jnp.float32)
    for c in range(GROUPS):
        zc = z[:, c * LANES:(c + 1) * LANES]
        if norm:
            zc = _head_norm(zc, g_ref[...], norm)
        if rope:
            zc = _rope(zc, rc_ref[...], ra_ref[...], rb_ref[...], rope // 8)
        for kind, o_ref in zip(kinds, out_refs):
            if kind == "f32":
                o_ref[:, c * LANES:(c + 1) * LANES] = zc
            else:
                o_ref[c] = (zc * scale).astype(o_ref.dtype)


def _proj(xn, w, gain, tables, *, norm, rope, scale, kinds, tm):
    r, d = xn.shape
    nj = w.shape[1] // SEC
    rc, ra, rb = tables
    out_shape, out_specs = [], []
    for kind in kinds:
        if kind == "f32":
            out_shape.append(jax.ShapeDtypeStruct((r, nj * SEC), jnp.float32))
            out_specs.append(pl.BlockSpec((tm, SEC), lambda j, i: (i, j)))
        else:
            assert nj == 1
            out_shape.append(jax.ShapeDtypeStruct((GROUPS, r, LANES), jnp.bfloat16))
            out_specs.append(pl.BlockSpec((GROUPS, tm, LANES), lambda j, i: (0, i, 0)))
    row = pl.BlockSpec((tm, LANES), lambda j, i: (i, 0))
    outs = pl.pallas_call(
        functools.partial(_proj_kernel, norm=norm, rope=rope, scale=scale, kinds=tuple(kinds)),
        grid=(nj, r // tm),
        in_specs=[pl.BlockSpec((tm, d), lambda j, i: (i, 0)),
                  pl.BlockSpec((d, SEC), lambda j, i: (0, j)),
                  pl.BlockSpec((1, LANES), lambda j, i: (0, 0)),
                  row, row, row],
        out_specs=out_specs,
        out_shape=out_shape,
        compiler_params=_params(("parallel", "parallel")),
        name="proj",
    )(xn, w, gain, rc, ra, rb)
    return outs


def _small_proj_kernel(xn_ref, w_ref, g_ref, rc_ref, ra_ref, rb_ref, f_ref, k2_ref, *, wscale):
    z = jnp.dot(xn_ref[...], w_ref[...], preferred_element_type=jnp.float32)
    lo = lax.broadcasted_iota(jnp.int32, z.shape, 1) < IDX_DIM
    ms = jnp.sum(jnp.where(lo, z * z, 0.0), axis=-1, keepdims=True) * (1.0 / IDX_DIM)
    zn = z * lax.rsqrt(ms + EPS) * g_ref[...]
    zr = _rope(zn, rc_ref[...], ra_ref[...], rb_ref[...], IDX_DIM // 8)
    f_ref[...] = jnp.where(lo, zr, z * wscale)
    k2_ref[...] = jnp.where(lo, zr, pltpu.roll(zr, IDX_DIM, 1)).astype(k2_ref.dtype)


def _small_proj(xn, w, gain, tables, tm):
    r, d = xn.shape
    rc, ra, rb = tables
    row = pl.BlockSpec((tm, LANES), lambda i: (i, 0))
    return pl.pallas_call(
        functools.partial(_small_proj_kernel, wscale=IDX_HEADS ** -0.5),
        grid=(r // tm,),
        in_specs=[pl.BlockSpec((tm, d), lambda i: (i, 0)),
                  pl.BlockSpec((d, LANES), lambda i: (0, 0)),
                  pl.BlockSpec((1, LANES), lambda i: (0, 0)),
                  row, row, row],
        out_specs=[row, row],
        out_shape=[jax.ShapeDtypeStruct((r, LANES), jnp.float32),
                   jax.ShapeDtypeStruct((r, LANES), jnp.bfloat16)],
        compiler_params=_params(("parallel",)),
        name="small_proj",
    )(xn, w, gain, rc, ra, rb)


def _rope_tables(pos, d):
    r = d // 4
    h = r // 2
    n = pos.shape[0]
    inv = ROPE_THETA ** (-jnp.arange(0, r, 2, dtype=jnp.float32) / r)
    ang = pos.astype(jnp.float32)[:, None] * inv[None, :]
    cos, sin = jnp.cos(ang), jnp.sin(ang)
    zh = jnp.zeros((n, h), jnp.float32)
    zr = jnp.zeros((n, d - r), jnp.float32)
    c = jnp.concatenate([cos, cos, jnp.ones((n, d - r), jnp.float32)], axis=-1)
    a = jnp.concatenate([-sin, zh, zr], axis=-1)
    b = jnp.concatenate([zh, sin, zr], axis=-1)
    rep = LANES // d
    return tuple(jnp.tile(t, (1, rep)) for t in (c, a, b))


def _online_update(s, v, m_ref, l_ref, acc_ref):
    m_prev = m_ref[...]
    m_new = jnp.maximum(m_prev, jnp.max(s, axis=-1, keepdims=True))
    alpha = jnp.exp2(m_prev - m_new)
    p = jnp.exp2(s - m_new)
    l_ref[...] = alpha * l_ref[...] + jnp.sum(p, axis=-1, keepdims=True)
    acc_ref[...] = alpha * acc_ref[...] + jnp.dot(p.astype(v.dtype), v, preferred_element_type=jnp.float32)
    m_ref[...] = m_new


def _causal_steps(t, tq, tk):
    qs, ks = [], []
    for qi in range(t // tq):
        nk = _cdiv((qi + 1) * tq, tk)
        qs += [qi] * nk
        ks += list(range(nk))
    return np.asarray(qs, np.int32), np.asarray(ks, np.int32)


def _unshifted_ok(gq, gk, d, scale):
    bound = jnp.max(jnp.abs(gq)) * jnp.max(jnp.abs(gk)) * (d * scale * (1.0 + 2.0 ** -6))
    return (bound <= MAX_UNSHIFTED).astype(jnp.int32).reshape(1)


def _lambda(lam_ref, lam_init):
    lp = lam_ref[...]
    s1 = jnp.sum(lp[0:1] * lp[1:2], axis=-1, keepdims=True)
    s2 = jnp.sum(lp[2:3] * lp[3:4], axis=-1, keepdims=True)
    return jnp.exp(s1) - jnp.exp(s2) + lam_init


def _diff_finish(o0, o1, lam, sg, lam_init):
    o = o0 - lam * o1
    ms = jnp.mean(o * o, axis=-1, keepdims=True)
    return o * lax.rsqrt(ms + EPS) * sg * (1.0 - lam_init)


def _chunk_end(pos):
    return (pos // CHUNK + 1) * CHUNK - 1


def _diff_attn_kernel(qt_ref, kt_ref, ok_ref, q_ref, k_ref, v_ref, lam_ref, sg_ref, o_ref,
                      qs_ref, v1_ref, acc_ref, m_ref, l_ref, *, tq, tk, lam_init):
    step_id = pl.program_id(1)
    qi = qt_ref[step_id]
    kj = kt_ref[step_id]
    nk = _cdiv((qi + 1) * tq, tk)
    unshifted = ok_ref[0] == 1
    full = (kj + 1) * tk <= qi * tq + CHUNK
    rows2 = 2 * tq

    @pl.when(kj == 0)
    def _():
        q = q_ref[...]
        lo = lax.broadcasted_iota(jnp.int32, q.shape, 1) < DH_A
        zero = jnp.zeros_like(q)
        qs_ref[0:tq, :] = jnp.where(lo, q, zero)
        qs_ref[tq:rows2, :] = jnp.where(lo, zero, q)
        v1_ref[:, DV_A:] = jnp.ones((tk, LANES), v1_ref.dtype)
        m_ref[...] = jnp.full(m_ref.shape, NEG, jnp.float32)
        l_ref[...] = jnp.zeros(l_ref.shape, jnp.float32)
        acc_ref[...] = jnp.zeros(acc_ref.shape, jnp.float32)

    v1_ref[:, :DV_A] = v_ref[...]

    def visible(r0, nrows):
        row = r0 + lax.broadcasted_iota(jnp.int32, (nrows, tk), 0)
        row = jnp.where(row >= tq, row - tq, row)
        kpos = kj * tk + lax.broadcasted_iota(jnp.int32, (nrows, tk), 1)
        return kpos <= _chunk_end(qi * tq + row)

    def fast_step(masked):
        k = k_ref[...]
        v1 = v1_ref[...]
        for r0 in range(0, rows2, ROW_GROUP):
            rows = slice(r0, r0 + ROW_GROUP)
            s = lax.dot_general(qs_ref[rows, :], k, _NT, preferred_element_type=jnp.float32)
            if masked:
                s = jnp.where(visible(r0, ROW_GROUP), s, NEG)
            p = jnp.exp2(s).astype(v1.dtype)
            acc_ref[rows, :] += jnp.dot(p, v1, preferred_element_type=jnp.float32)

    def exact_step(masked):
        s = lax.dot_general(qs_ref[...], k_ref[...], _NT, preferred_element_type=jnp.float32)
        if masked:
            s = jnp.where(visible(0, rows2), s, NEG)
        _online_update(s, v_ref[...], m_ref, l_ref, acc_ref.at[:, :DV_A])

    for fast, step in ((True, fast_step), (False, exact_step)):
        mode = unshifted if fast else jnp.logical_not(unshifted)

        @pl.when(jnp.logical_and(mode, full))
        def _():
            step(False)

        @pl.when(jnp.logical_and(mode, jnp.logical_not(full)))
        def _():
            step(True)

    @pl.when(kj == nk - 1)
    def _():
        @pl.when(jnp.logical_not(unshifted))
        def _():
            acc_ref[:, DV_A:] = jnp.broadcast_to(l_ref[...], (rows2, LANES))

        o = acc_ref[:, :DV_A] / acc_ref[:, DV_A:]
        o_ref[...] = _diff_finish(o[0:tq], o[tq:rows2], _lambda(lam_ref, lam_init), sg_ref[...], lam_init)


def _diff_attn(q, k, v, ok, lam_p, sg, lam_init, tq, tk):
    _, t, _ = q.shape
    qt, kt = _causal_steps(t, tq, tk)
    kv_spec = pl.BlockSpec((None, tk, LANES), lambda h, s, qt, kt, ok: (h, kt[s], 0))
    const = lambda h, s, qt, kt, ok: (0, 0)
    grid_spec = pltpu.PrefetchScalarGridSpec(
        num_scalar_prefetch=3,
        grid=(H_A, len(qt)),
        in_specs=[pl.BlockSpec((None, tq, LANES), lambda h, s, qt, kt, ok: (h, qt[s], 0)),
                  kv_spec, kv_spec,
                  pl.BlockSpec((4, DH_A), const),
                  pl.BlockSpec((1, DV_A), const)],
        out_specs=pl.BlockSpec((tq, LANES), lambda h, s, qt, kt, ok: (qt[s], h)),
        scratch_shapes=[pltpu.VMEM((2 * tq, LANES), jnp.bfloat16),
                        pltpu.VMEM((tk, DV_A + LANES), jnp.bfloat16),
                        pltpu.VMEM((2 * tq, DV_A + LANES), jnp.float32),
                        pltpu.VMEM((2 * tq, 1), jnp.float32),
                        pltpu.VMEM((2 * tq, 1), jnp.float32)])
    return pl.pallas_call(
        functools.partial(_diff_attn_kernel, tq=tq, tk=tk, lam_init=lam_init),
        grid_spec=grid_spec,
        out_shape=jax.ShapeDtypeStruct((t, H_A * DV_A), jnp.float32),
        compiler_params=_params(("parallel", "arbitrary")),
        name="diff_attn",
    )(jnp.asarray(qt), jnp.asarray(kt), ok, q, k, v, lam_p, sg)


def _sort_key(score):
    bits = pltpu.bitcast(score, jnp.int32)
    key = jnp.where(bits < 0, bits ^ INT_MAX, bits)
    return jnp.where(key == -1, 0, key)


def _index_kernel(qt_ref, kt_ref, q_ref, w_ref, k_ref, bias_ref, qs_ref, key_ref, p_ref, *, tq, tk, t, topk):
    step_id = pl.program_id(0)
    qi = qt_ref[step_id]
    kj = kt_ref[step_id]
    nk = _cdiv((qi + 1) * tq, tk)
    nsub = tk // LANES

    @pl.when(kj == 0)
    def _():
        for p in range(GROUPS):
            q = q_ref[p]
            lo = lax.broadcasted_iota(jnp.int32, q.shape, 1) < IDX_DIM
            zero = jnp.zeros_like(q)
            qs_ref[(2 * p) * tq:(2 * p + 1) * tq, :] = jnp.where(lo, q, zero)
            qs_ref[(2 * p + 1) * tq:(2 * p + 2) * tq, :] = jnp.where(lo, zero, q)

    logits = lax.dot_general(qs_ref[...], k_ref[...], _NT, preferred_element_type=jnp.float32)
    w = w_ref[...]
    score = None
    for h in range(IDX_HEADS):
        term = jnp.maximum(logits[h * tq:(h + 1) * tq], 0.0) * w[:, h:h + 1]
        score = term if score is None else score + term
    qpos = qi * tq + lax.broadcasted_iota(jnp.int32, score.shape, 0)
    kpos = kj * tk + lax.broadcasted_iota(jnp.int32, score.shape, 1)
    key = jnp.where(kpos <= _chunk_end(qpos), _sort_key(score), INT_MIN)
    key_ref[:, pl.ds(pl.multiple_of(kj * tk, tk), tk)] = key

    @pl.when(kj == nk - 1)
    def _():
        lane = lax.broadcasted_iota(jnp.int32, (tq, LANES), 1)
        cend = _chunk_end(qi * tq + lax.broadcasted_iota(jnp.int32, (tq, LANES), 0))

        def rows(x):
            return jnp.broadcast_to(x, (tq, LANES))

        def sweep(fn, init):
            def body(j, acc):
                for c in range(nsub):
                    off = pl.multiple_of(j * tk + c * LANES, LANES)
                    acc = fn(acc, key_ref[:, pl.ds(off, LANES)], off + lane, c % 2)
                return acc
            return lax.fori_loop(0, nk, body, init)

        def count(pred):
            acc = sweep(lambda a, kt, idx, par: a + jnp.where(pred(kt, idx), 1.0, 0.0),
                        jnp.zeros((tq, LANES), jnp.float32))
            return rows(jnp.sum(acc, axis=1, keepdims=True))

        def key_score(key):
            f = pltpu.bitcast(jnp.where(key < 0, key ^ INT_MAX, key), jnp.float32)
            return jnp.where(key == INT_MIN, -jnp.inf, f)

        kf = float(topk)
        ninf = jnp.full((tq, LANES), -jnp.inf, jnp.float32)

        def group_max(acc, kt, idx, par):
            f = key_score(kt)
            return (jnp.maximum(acc[0], f), acc[1]) if par == 0 else (acc[0], jnp.maximum(acc[1], f))

        m0, m1 = sweep(group_max, (ninf, ninf))
        lo = _sort_key(rows(jnp.min(jnp.minimum(m0, m1), axis=1, keepdims=True)))
        hi = _sort_key(rows(jnp.max(jnp.maximum(m0, m1), axis=1, keepdims=True))) + 1
        c_lo = count(lambda kt, idx: kt >= lo)
        c_hi = jnp.zeros((tq, LANES), jnp.float32)
        all_sel = cend + 1 <= topk
        lo = jnp.where(all_sel, INT_MIN + 1, lo)
        done = jnp.where(jnp.logical_or(all_sel, c_lo == kf), 1.0, 0.0)

        def search_cond(st):
            return jnp.logical_and(st[0] < 80, st[6] < 0.5)

        def search_body(st):
            it, lo, hi, c_lo, c_hi, done, _ = st
            mid = (lo >> 1) + (hi >> 1) + (lo & hi & 1)
            lo_f, hi_f = key_score(lo), key_score(hi)
            frac = (jnp.log(c_lo) - math.log(kf)) / (jnp.log(c_lo) - jnp.log(jnp.maximum(c_hi, 0.5)))
            t_f = lo_f + (hi_f - lo_f) * frac
            interp = jnp.logical_and((it & 1) == 0, jnp.abs(t_f) < jnp.inf)
            t = jnp.where(interp, _sort_key(t_f), mid)
            t = jnp.minimum(jnp.maximum(t, lo + 1), hi - 1)
            c = count(lambda kt, idx: kt >= t)
            up = jnp.logical_and(done < 0.5, c >= kf)
            dn = jnp.logical_and(done < 0.5, c < kf)
            lo, c_lo = jnp.where(up, t, lo), jnp.where(up, c, c_lo)
            hi, c_hi = jnp.where(dn, t, hi), jnp.where(dn, c, c_hi)
            done = jnp.where(jnp.logical_or(c_lo == kf, hi == lo + 1), 1.0, done)
            return it + 1, lo, hi, c_lo, c_hi, done, jnp.min(done)

        st = (jnp.int32(0), lo, hi, c_lo, c_hi, done, jnp.min(done))
        _, thr, hi, c_lo, c_hi, _, _ = lax.while_loop(search_cond, search_body, st)
        tied = jnp.logical_and(jnp.logical_not(all_sel), c_lo > kf)
        need = jnp.where(tied, kf - c_hi, jnp.inf)
        p_ref[...] = jnp.full((tq, LANES), INT_MAX, jnp.int32)

        @pl.when(jnp.max(jnp.where(tied, 1.0, 0.0)) > 0.5)
        def _():
            nbits = max(1, (t - 1).bit_length())

            def pos_body(b, pos):
                cand = pos | lax.shift_left(jnp.int32(1), jnp.int32(nbits - 1) - b)
                g = count(lambda kt, idx: jnp.logical_and(kt == thr, idx < cand))
                return jnp.where(g < need, cand, pos)

            p_ref[...] = lax.fori_loop(0, nbits, pos_body, jnp.zeros((tq, LANES), jnp.int32))

        last = p_ref[...]

        def write_body(j, carry):
            for c in range(nsub):
                off = pl.multiple_of(j * tk + c * LANES, LANES)
                kt = key_ref[:, pl.ds(off, LANES)]
                idx = off + lane
                sel = jnp.logical_or(kt > thr, jnp.logical_and(kt == thr, idx <= last))
                sel = jnp.logical_and(sel, idx <= cend)
                bias_ref[:, pl.ds(off, LANES)] = jnp.where(sel, 0.0, NEG).astype(bias_ref.dtype)
            return carry

        lax.fori_loop(0, nk, write_body, 0)

        def fill_body(j, carry):
            off = pl.multiple_of(j * tk, tk)
            bias_ref[:, pl.ds(off, tk)] = jnp.full((tq, tk), NEG, bias_ref.dtype)
            return carry

        lax.fori_loop(nk, t // tk, fill_body, 0)


def _index_select(qi16, wi, ki2, topk, tq, tk):
    _, t, _ = qi16.shape
    qt, kt = _causal_steps(t, tq, tk)
    grid_spec = pltpu.PrefetchScalarGridSpec(
        num_scalar_prefetch=2,
        grid=(len(qt),),
        in_specs=[pl.BlockSpec((GROUPS, tq, LANES), lambda s, qt, kt: (0, qt[s], 0)),
                  pl.BlockSpec((tq, IDX_HEADS), lambda s, qt, kt: (qt[s], 0)),
                  pl.BlockSpec((tk, LANES), lambda s, qt, kt: (kt[s], 0))],
        out_specs=pl.BlockSpec((tq, t), lambda s, qt, kt: (qt[s], 0)),
        scratch_shapes=[pltpu.VMEM((IDX_HEADS * tq, LANES), jnp.bfloat16),
                        pltpu.VMEM((tq, t), jnp.int32),
                        pltpu.VMEM((tq, LANES), jnp.int32)])
    return pl.pallas_call(
        functools.partial(_index_kernel, tq=tq, tk=tk, t=t, topk=topk),
        grid_spec=grid_spec,
        out_shape=jax.ShapeDtypeStruct((t, t), jnp.bfloat16),
        compiler_params=_params(("arbitrary",)),
        name="index_select",
    )(jnp.asarray(qt), jnp.asarray(kt), qi16, wi, ki2)


def _dsa_attn_kernel(qt_ref, kt_ref, ok_ref, q_ref, k_ref, v_ref, bias_ref, o_ref,
                     v1_ref, acc_ref, m_ref, l_ref, *, tq, tk):
    step_id = pl.program_id(0)
    qi = qt_ref[step_id]
    kj = kt_ref[step_id]
    nk = _cdiv((qi + 1) * tq, tk)
    unshifted = ok_ref[0] == 1

    @pl.when(kj == 0)
    def _():
        v1_ref[:, :, DH_B:] = jnp.ones((H_B, tk, LANES), v1_ref.dtype)
        m_ref[...] = jnp.full(m_ref.shape, NEG, jnp.float32)
        l_ref[...] = jnp.zeros(l_ref.shape, jnp.float32)
        acc_ref[...] = jnp.zeros(acc_ref.shape, jnp.float32)

    @pl.when(unshifted)
    def _():
        v1_ref[:, :, :DH_B] = v_ref[...]
        for r0 in range(0, tq, ROW_GROUP):
            rows = slice(r0, r0 + ROW_GROUP)
            bias = bias_ref[rows, :].astype(jnp.float32)
            for h in range(H_B):
                s = lax.dot_general(q_ref[h, rows, :], k_ref[h], _NT, preferred_element_type=jnp.float32) + bias
                p = jnp.exp2(s).astype(v1_ref.dtype)
                acc_ref[h, rows, :] += jnp.dot(p, v1_ref[h], preferred_element_type=jnp.float32)

    @pl.when(jnp.logical_not(unshifted))
    def _():
        bias = bias_ref[...].astype(jnp.float32)
        for h in range(H_B):
            s = lax.dot_general(q_ref[h], k_ref[h], _NT, preferred_element_type=jnp.float32) + bias
            _online_update(s, v_ref[h], m_ref.at[h], l_ref.at[h], acc_ref.at[h, :, :DH_B])

    @pl.when(kj == nk - 1)
    def _():
        @pl.when(jnp.logical_not(unshifted))
        def _():
            acc_ref[:, :, DH_B:] = jnp.broadcast_to(l_ref[...], (H_B, tq, LANES))

        for h in range(H_B):
            o_ref[:, h * LANES:(h + 1) * LANES] = acc_ref[h, :, :DH_B] / acc_ref[h, :, DH_B:]


def _dsa_attn(q, k, v, bias, ok, tq, tk):
    _, t, _ = q.shape
    qt, kt = _causal_steps(t, tq, tk)
    kv_spec = pl.BlockSpec((H_B, tk, LANES), lambda s, qt, kt, ok: (0, kt[s], 0))
    grid_spec = pltpu.PrefetchScalarGridSpec(
        num_scalar_prefetch=3,
        grid=(len(qt),),
        in_specs=[pl.BlockSpec((H_B, tq, LANES), lambda s, qt, kt, ok: (0, qt[s], 0)),
                  kv_spec, kv_spec,
                  pl.BlockSpec((tq, tk), lambda s, qt, kt, ok: (qt[s], kt[s]))],
        out_specs=pl.BlockSpec((tq, H_B * DH_B), lambda s, qt, kt, ok: (qt[s], 0)),
        scratch_shapes=[pltpu.VMEM((H_B, tk, DH_B + LANES), jnp.bfloat16),
                        pltpu.VMEM((H_B, tq, DH_B + LANES), jnp.float32),
                        pltpu.VMEM((H_B, tq, 1), jnp.float32),
                        pltpu.VMEM((H_B, tq, 1), jnp.float32)])
    return pl.pallas_call(
        functools.partial(_dsa_attn_kernel, tq=tq, tk=tk),
        grid_spec=grid_spec,
        out_shape=jax.ShapeDtypeStruct((t, H_B * DH_B), jnp.float32),
        compiler_params=_params(("arbitrary",)),
        name="dsa_attn",
    )(jnp.asarray(qt), jnp.asarray(kt), ok, q, k, v, bias)


def _sample_kernel(qa_ref, qb_ref, qi_ref, wi_ref, ckd_ref, cvd_ref, ckb_ref, cvb_ref, cki_ref,
                   nkd_ref, nvd_ref, nkb_ref, nvb_ref, nki_ref, lam_ref, sg_ref,
                   oa_ref, ob_ref,
                   qda_ref, qdb_ref, bias_ref, key_ref, ma_ref, la_ref, acca_ref, mb_ref, lb_ref, accb_ref,
                   *, ts, past, tk, topk, lam_init):
    kj = pl.program_id(1)
    nkt = past // tk
    width = past + LANES
    ra = 2 * H_A * ts
    rb = H_B * ts

    def new_key_valid(rows):
        t_q = lax.broadcasted_iota(jnp.int32, (rows, LANES), 0) % ts
        lane = lax.broadcasted_iota(jnp.int32, (rows, LANES), 1)
        return jnp.logical_and(lane < ts, past + lane <= _chunk_end(past + t_q))

    @pl.when(kj == 0)
    def _():
        qa = qa_ref[...]
        grp = lax.broadcasted_iota(jnp.int32, qa.shape, 1) // DH_A
        for hc in range(2 * H_A):
            qda_ref[hc * ts:(hc + 1) * ts, :] = jnp.where(grp == hc, qa, jnp.zeros_like(qa))
        qb = qb_ref[...]
        grp = lax.broadcasted_iota(jnp.int32, qb.shape, 1) // DH_B
        for h in range(H_B):
            qdb_ref[h * ts:(h + 1) * ts, :] = jnp.where(grp == h, qb, jnp.zeros_like(qb))
        for ref in (ma_ref, mb_ref):
            ref[...] = jnp.full(ref.shape, NEG, jnp.float32)
        for ref in (la_ref, lb_ref, acca_ref, accb_ref):
            ref[...] = jnp.zeros(ref.shape, jnp.float32)

        qi = qi_ref[...]
        w = wi_ref[...]

        def scores(keys16):
            logits = lax.dot_general(qi, keys16, _NT, preferred_element_type=jnp.float32)
            score = None
            for h in range(IDX_HEADS):
                term = jnp.maximum(logits[h * ts:(h + 1) * ts], 0.0) * w[h * ts:(h + 1) * ts]
                score = term if score is None else score + term
            return score

        step = min(past, 1024)
        for c in range(past // step):
            keys16 = cki_ref[c * step:(c + 1) * step, :].astype(jnp.bfloat16)
            key_ref[:, c * step:(c + 1) * step] = _sort_key(scores(keys16))
        key_ref[:, past:width] = jnp.where(new_key_valid(ts), _sort_key(scores(nki_ref[...])), INT_MIN)

        kf = float(topk)

        def count(pred):
            idx = lax.broadcasted_iota(jnp.int32, (ts, width), 1)
            c = jnp.sum(jnp.where(pred(key_ref[...], idx), 1.0, 0.0), axis=1, keepdims=True)
            return jnp.broadcast_to(c, (ts, LANES))

        def wide(x):
            return x[:, 0:1]

        c0 = count(lambda kt, idx: kt >= 0)
        prefix = jnp.where(c0 >= kf, 0, INT_MIN).astype(jnp.int32)

        def bit_body(b, prefix):
            cand = prefix | lax.shift_left(jnp.int32(1), jnp.int32(30) - b)
            c = count(lambda kt, idx: kt >= wide(cand))
            return jnp.where(c >= kf, cand, prefix)

        thr = lax.fori_loop(0, 31, bit_body, prefix)
        n_gt = count(lambda kt, idx: kt > wide(thr))
        need = kf - n_gt
        nbits = max(1, (width - 1).bit_length())

        def pos_body(b, pos):
            cand = pos | lax.shift_left(jnp.int32(1), jnp.int32(nbits - 1) - b)
            g = count(lambda kt, idx: jnp.logical_and(kt == wide(thr), idx < wide(cand)))
            return jnp.where(g < need, cand, pos)

        last = lax.fori_loop(0, nbits, pos_body, jnp.zeros((ts, LANES), jnp.int32))
        kt = key_ref[...]
        idx = lax.broadcasted_iota(jnp.int32, (ts, width), 1)
        sel = jnp.logical_or(kt > wide(thr), jnp.logical_and(kt == wide(thr), idx <= wide(last)))
        sel = jnp.logical_and(sel, kt != INT_MIN)
        bias = jnp.where(sel, 0.0, NEG)
        for h in range(H_B):
            bias_ref[h * ts:(h + 1) * ts, :] = bias

    def head_rows(ref):
        return jnp.concatenate([ref[pl.ds(h, tk, stride=GROUPS), :].astype(jnp.bfloat16) for h in range(GROUPS)],
                               axis=1)

    sa = lax.dot_general(qda_ref[...], ckd_ref[...].astype(jnp.bfloat16), _NT, preferred_element_type=jnp.float32)
    _online_update(sa, head_rows(cvd_ref), ma_ref, la_ref, acca_ref)
    sb = lax.dot_general(qdb_ref[...], head_rows(ckb_ref), _NT, preferred_element_type=jnp.float32)
    sb = sb + bias_ref[:, pl.ds(pl.multiple_of(kj * tk, tk), tk)]
    _online_update(sb, head_rows(cvb_ref), mb_ref, lb_ref, accb_ref)

    @pl.when(kj == nkt - 1)
    def _():
        sa = lax.dot_general(qda_ref[...], nkd_ref[...], _NT, preferred_element_type=jnp.float32)
        sa = jnp.where(new_key_valid(ra), sa, NEG)
        _online_update(sa, nvd_ref[...], ma_ref, la_ref, acca_ref)
        sb = lax.dot_general(qdb_ref[...], nkb_ref[...], _NT, preferred_element_type=jnp.float32)
        sb = sb + bias_ref[:, past:width]
        _online_update(sb, nvb_ref[...], mb_ref, lb_ref, accb_ref)

        lam = _lambda(lam_ref, lam_init)
        oa = acca_ref[...] / la_ref[...]
        ob = accb_ref[...] / lb_ref[...]
        for h in range(H_A):
            cols = slice(h * DV_A, (h + 1) * DV_A)
            o0 = oa[(2 * h) * ts:(2 * h + 1) * ts, cols]
            o1 = oa[(2 * h + 1) * ts:(2 * h + 2) * ts, cols]
            oa_ref[:, cols] = _diff_finish(o0, o1, lam, sg_ref[...], lam_init)
        for h in range(H_B):
            cols = slice(h * DH_B, (h + 1) * DH_B)
            ob_ref[:, cols] = ob[h * ts:(h + 1) * ts, cols]


def _sample_attn(l, qa, qb, qi, wi, caches, news, lam_p, sg, lam_init, topk, tk):
    bsz, ts, _ = qa.shape
    ckd, cvd, ckb, cvb, cki = caches
    past = cki.shape[2]
    width = past + LANES
    ra, rb = 2 * H_A * ts, H_B * ts
    cache_spec = pl.BlockSpec((None, None, tk, SEC), lambda b, kj: (l, b, kj, 0))
    heads_spec = pl.BlockSpec((None, None, tk * GROUPS, LANES), lambda b, kj: (l, b, kj, 0))
    new_spec = pl.BlockSpec((None, LANES, SEC), lambda b, kj: (b, 0, 0))
    row_spec = pl.BlockSpec((None, ts, SEC), lambda b, kj: (b, 0, 0))
    return pl.pallas_call(
        functools.partial(_sample_kernel, ts=ts, past=past, tk=tk, topk=topk, lam_init=lam_init),
        grid=(bsz, past // tk),
        in_specs=[row_spec, row_spec,
                  pl.BlockSpec((None, IDX_HEADS * ts, IDX_DIM), lambda b, kj: (b, 0, 0)),
                  pl.BlockSpec((None, IDX_HEADS * ts, 1), lambda b, kj: (b, 0, 0)),
                  cache_spec, heads_spec, heads_spec, heads_spec,
                  pl.BlockSpec((None, None, past, IDX_DIM), lambda b, kj: (l, b, 0, 0)),
                  new_spec, new_spec, new_spec, new_spec,
                  pl.BlockSpec((None, LANES, IDX_DIM), lambda b, kj: (b, 0, 0)),
                  pl.BlockSpec((4, DH_A), lambda b, kj: (0, 0)),
                  pl.BlockSpec((1, DV_A), lambda b, kj: (0, 0))],
        out_specs=[row_spec, row_spec],
        out_shape=[jax.ShapeDtypeStruct((bsz, ts, SEC), jnp.float32),
                   jax.ShapeDtypeStruct((bsz, ts, SEC), jnp.float32)],
        scratch_shapes=[pltpu.VMEM((ra, SEC), jnp.bfloat16),
                        pltpu.VMEM((rb, SEC), jnp.bfloat16),
                        pltpu.VMEM((rb, width), jnp.float32),
                        pltpu.VMEM((ts, width), jnp.int32),
                        pltpu.VMEM((ra, 1), jnp.float32),
                        pltpu.VMEM((ra, 1), jnp.float32),
                        pltpu.VMEM((ra, SEC), jnp.float32),
                        pltpu.VMEM((rb, 1), jnp.float32),
                        pltpu.VMEM((rb, 1), jnp.float32),
                        pltpu.VMEM((rb, SEC), jnp.float32)],
        compiler_params=_params(("parallel", "arbitrary")),
        name="sample_attn",
    )(qa, qb, qi, wi, ckd, cvd, ckb, cvb, cki, *news, lam_p, sg)


def _silu(x):
    return x * jax.nn.sigmoid(x)


def _merge_kernel(oa_ref, ga_ref, ob_ref, gb_ref, mg_ref, wa_ref, wb_ref, u_ref):
    d = u_ref.shape[-1]
    ya = jnp.dot((oa_ref[...] * _silu(ga_ref[...])).astype(jnp.bfloat16), wa_ref[...],
                 preferred_element_type=jnp.float32)
    yb = jnp.dot((ob_ref[...] * _silu(gb_ref[...])).astype(jnp.bfloat16), wb_ref[...],
                 preferred_element_type=jnp.float32)
    mg = mg_ref[...]
    u = jax.nn.sigmoid(mg[:, :d]) * ya + jax.nn.sigmoid(mg[:, d:]) * yb
    u_ref[...] = u.astype(u_ref.dtype)


def _resid_kernel(x_ref, u_ref, wo_ref, g_ref, h_ref, hn_ref):
    h = x_ref[...] + jnp.dot(u_ref[...], wo_ref[...], preferred_element_type=jnp.float32)
    h_ref[...] = h
    ms = jnp.mean(h * h, axis=-1, keepdims=True)
    hn_ref[...] = (h * lax.rsqrt(ms + EPS) * g_ref[...]).astype(hn_ref.dtype)


def _ple_kernel(h_ref, hn_ref, p_ref, wg_ref, wp_ref, o_ref):
    gate = jax.nn.sigmoid(jnp.dot(hn_ref[...], wg_ref[...], preferred_element_type=jnp.float32))
    e = jnp.dot(p_ref[...].astype(jnp.bfloat16), wp_ref[...], preferred_element_type=jnp.float32)
    o_ref[...] = h_ref[...] + gate * e


def _layer_output(x, p, oa, ga, ob, gb, mg, w_ba, w_bb, w_out, ple_g, w_pg, w_ple, tm):
    r, d = x.shape
    wa = oa.shape[1]
    dp = p.shape[1]
    rows = lambda n: pl.BlockSpec((tm, n), lambda i: (i, 0))
    full = lambda a, b: pl.BlockSpec((a, b), lambda i: (0, 0))
    u = pl.pallas_call(
        _merge_kernel,
        grid=(r // tm,),
        in_specs=[rows(wa), rows(wa), rows(wa), rows(wa), rows(2 * d), full(wa, d), full(wa, d)],
        out_specs=rows(d),
        out_shape=jax.ShapeDtypeStruct((r, d), jnp.bfloat16),
        compiler_params=_params(("parallel",)),
        name="merge",
    )(oa, ga, ob, gb, mg, w_ba, w_bb)
    h, hn = pl.pallas_call(
        _resid_kernel,
        grid=(r // tm,),
        in_specs=[rows(d), rows(d), full(d, d), full(1, d)],
        out_specs=[rows(d), rows(d)],
        out_shape=[jax.ShapeDtypeStruct((r, d), jnp.float32), jax.ShapeDtypeStruct((r, d), jnp.bfloat16)],
        compiler_params=_params(("parallel",)),
        name="resid",
    )(x, u, w_out, ple_g)
    return pl.pallas_call(
        _ple_kernel,
        grid=(r // tm,),
        in_specs=[rows(d), rows(d), rows(dp), full(d, d), full(dp, d)],
        out_specs=rows(d),
        out_shape=jax.ShapeDtypeStruct((r, d), jnp.float32),
        compiler_params=_params(("parallel",)),
        name="ple",
    )(h, hn, p, w_pg, w_ple)


def _tile(n, pref):
    t = min(n, pref)
    while n % t:
        t //= 2
    return t


def _layer_inputs(x, pos, lw, tm):
    xn = _rmsnorm(x, lw["ln_g"], tm)
    t64 = _rope_tables(pos, DH_A)
    t128 = _rope_tables(pos, DH_B)
    none = t64
    one = jnp.ones((1, LANES), jnp.float32)
    w = lw["w_sec"]
    pj = functools.partial(_proj, xn, tm=tm)
    (qa16,) = pj(w["qa"], lw["qn_a"], t64, norm=DH_A, rope=DH_A, scale=DH_A ** -0.5 * LOG2E, kinds=["hm16"])
    ka, ka16 = pj(w["ka"], lw["kn_a"], t64, norm=DH_A, rope=DH_A, scale=1.0, kinds=["f32", "hm16"])
    va, va16 = pj(w["va"], one, none, norm=0, rope=0, scale=1.0, kinds=["f32", "hm16"])
    (ga,) = pj(w["ga"], one, none, norm=0, rope=0, scale=1.0, kinds=["f32"])
    (qb16,) = pj(w["qb"], lw["qn_b"], t128, norm=DH_B, rope=DH_B, scale=DH_B ** -0.5 * LOG2E, kinds=["hm16"])
    kb, kb16 = pj(w["kb"], lw["kn_b"], t128, norm=DH_B, rope=DH_B, scale=1.0, kinds=["f32", "hm16"])
    vb, vb16 = pj(w["vb"], one, none, norm=0, rope=0, scale=1.0, kinds=["f32", "hm16"])
    (gb,) = pj(w["gb"], one, none, norm=0, rope=0, scale=1.0, kinds=["f32"])
    (qi16,) = pj(w["qi"], one, t64, norm=0, rope=DH_A, scale=IDX_DIM ** -0.5, kinds=["hm16"])
    (mg,) = pj(w["mg"], one, none, norm=0, rope=0, scale=1.0, kinds=["f32"])
    small, ki2 = _small_proj(xn, w["small"], lw["kn_i"], t64, tm)
    ki = small[:, :IDX_DIM]
    wi = small[:, IDX_DIM:IDX_DIM + IDX_HEADS]
    return dict(qa16=qa16, ka=ka, ka16=ka16, va=va, va16=va16, ga=ga, qb16=qb16, kb=kb, kb16=kb16,
                vb=vb, vb16=vb16, gb=gb, qi16=qi16, mg=mg, ki=ki, ki2=ki2, wi=wi)


def _row_major(hm):
    g, r, n = hm.shape
    return jnp.transpose(hm, (1, 0, 2)).reshape(r, g * n)


def _pad_new(a, bsz, ts):
    a = a.reshape(bsz, ts, a.shape[-1]).astype(jnp.bfloat16)
    return jnp.pad(a, ((0, 0), (0, LANES - ts), (0, 0)))


def kernel(x_prompt, x_sample, p_prompt, p_sample, cache_diff_k, cache_diff_v, cache_dsa_k, cache_dsa_v, cache_idx_k, ln_g, w_in, q_norm_a, k_norm_a, lam_q1, lam_k1, lam_q2, lam_k2, subln_a, q_norm_b, k_norm_b, k_norm_idx, w_branch_a, w_branch_b, w_out, ple_norm, w_ple_gate, w_ple):
    depth = w_in.shape[0]
    bp, t_p, d_model = x_prompt.shape
    bsz, t_s, _ = x_sample.shape
    past = cache_diff_k.shape[2]
    assert bp == 1 and t_s <= LANES and t_p % LANES == 0 and past % LANES == 0
    topk_p = min(TOPK_MAX, t_p // 4)
    topk_s = min(TOPK_MAX, (past + t_s) // 4)
    pos_p = jnp.arange(t_p)
    pos_s = jnp.tile(past + jnp.arange(t_s), bsz)
    r_s = bsz * t_s

    tm_p = _tile(t_p, 512)
    tm_s = _tile(r_s, 512)
    tq_a, tk_a = _tile(t_p, 512), _tile(t_p, 1024)
    tq_i, tk_i = _tile(t_p, 128), _tile(t_p, 1024)
    tq_b, tk_b = _tile(t_p, 512), _tile(t_p, 1024)
    tk_s = _tile(past, 512)
    tm_o = _tile(t_p, 256)
    tm_os = _tile(r_s, 256)

    by_head = lambda c: c.reshape(depth, bsz, past * GROUPS, LANES)
    caches = (cache_diff_k.reshape(depth, bsz, past, SEC), by_head(cache_diff_v),
              by_head(cache_dsa_k), by_head(cache_dsa_v), cache_idx_k)

    sizes = (SEC,) * 9 + (IDX_DIM, IDX_HEADS, 2 * d_model)
    names = ("qa", "ka", "va", "ga", "qb", "kb", "vb", "gb", "qi", "ki", "wi", "mg")
    offs = [0]
    for s in sizes:
        offs.append(offs[-1] + s)

    hp = x_prompt.reshape(t_p, d_model)
    hs = x_sample.reshape(r_s, d_model)
    outs_p = [[] for _ in range(5)]
    outs_s = [[] for _ in range(5)]
    for l in range(depth):
        lam_init = 0.8 - 0.6 * math.exp(-0.3 * l)
        wl = w_in[l].astype(jnp.bfloat16)
        w_sec = {n: wl[:, offs[i]:offs[i + 1]] for i, n in enumerate(names)}
        w_sec["small"] = jnp.pad(jnp.concatenate([w_sec.pop("ki"), w_sec.pop("wi")], axis=1),
                                 ((0, 0), (0, LANES - IDX_DIM - IDX_HEADS)))
        lane_gain = lambda g: jnp.tile(g.astype(jnp.float32), LANES // g.shape[0]).reshape(1, LANES)
        lw = dict(ln_g=ln_g[l], w_sec=w_sec, qn_a=lane_gain(q_norm_a[l]), kn_a=lane_gain(k_norm_a[l]),
                  qn_b=lane_gain(q_norm_b[l]), kn_b=lane_gain(k_norm_b[l]), kn_i=lane_gain(k_norm_idx[l]))
        ok_a = _unshifted_ok(q_norm_a[l], k_norm_a[l], DH_A, DH_A ** -0.5)
        ok_b = _unshifted_ok(q_norm_b[l], k_norm_b[l], DH_B, DH_B ** -0.5)
        lam_p = jnp.stack([lam_q1[l], lam_k1[l], lam_q2[l], lam_k2[l]]).astype(jnp.float32)
        sg = subln_a[l].reshape(1, DV_A).astype(jnp.float32)
        w_o = (w_branch_a[l].astype(jnp.bfloat16), w_branch_b[l].astype(jnp.bfloat16),
               w_out[l].astype(jnp.bfloat16), ple_norm[l].reshape(1, d_model),
               w_ple_gate[l].astype(jnp.bfloat16), w_ple[l].astype(jnp.bfloat16))

        a = _layer_inputs(hp, pos_p, lw, tm_p)
        oa = _diff_attn(a["qa16"], a["ka16"], a["va16"], ok_a, lam_p, sg, lam_init, tq_a, tk_a)
        bias = _index_select(a["qi16"], a["wi"], a["ki2"], topk_p, tq_i, tk_i)
        ob = _dsa_attn(a["qb16"], a["kb16"], a["vb16"], bias, ok_b, tq_b, tk_b)
        hp = _layer_output(hp, p_prompt[l].reshape(t_p, -1), oa, a["ga"], ob, a["gb"], a["mg"], *w_o, tm_o)
        for lst, key in zip(outs_p, ("ka", "va", "kb", "vb", "ki")):
            lst.append(a[key])

        s = _layer_inputs(hs, pos_s, lw, tm_s)
        qa_s = _row_major(s["qa16"]).reshape(bsz, t_s, SEC)
        qb_s = _row_major(s["qb16"]).reshape(bsz, t_s, SEC)
        qi_s = _row_major(s["qi16"]).reshape(bsz, t_s, IDX_HEADS, IDX_DIM)
        qi_s = jnp.transpose(qi_s, (0, 2, 1, 3)).reshape(bsz, IDX_HEADS * t_s, IDX_DIM)
        wi_s = jnp.transpose(s["wi"].reshape(bsz, t_s, IDX_HEADS), (0, 2, 1)).reshape(bsz, IDX_HEADS * t_s, 1)
        news = tuple(_pad_new(s[k], bsz, t_s) for k in ("ka", "va", "kb", "vb", "ki"))
        oa_s, ob_s = _sample_attn(l, qa_s, qb_s, qi_s, wi_s, caches, news, lam_p, sg, lam_init, topk_s, tk_s)
        hs = _layer_output(hs, p_sample[l].reshape(r_s, -1), oa_s.reshape(r_s, SEC), s["ga"],
                           ob_s.reshape(r_s, SEC), s["gb"], s["mg"], *w_o, tm_os)
        for lst, key in zip(outs_s, ("ka", "va", "kb", "vb", "ki")):
            lst.append(s[key])

    def stack(lst, shape):
        return jnp.stack(lst).reshape((depth,) + shape)

    return (hp.reshape(bp, t_p, d_model), hs.reshape(bsz, t_s, d_model),
            stack(outs_p[0], (bp, t_p, H_A, 2, DH_A)), stack(outs_p[1], (bp, t_p, H_A, DV_A)),
            stack(outs_p[2], (bp, t_p, H_B, DH_B)), stack(outs_p[3], (bp, t_p, H_B, DH_B)),
            stack(outs_p[4], (bp, t_p, IDX_DIM)),
            stack(outs_s[0], (bsz, t_s, H_A, 2, DH_A)), stack(outs_s[1], (bsz, t_s, H_A, DV_A)),
            stack(outs_s[2], (bsz, t_s, H_B, DH_B)), stack(outs_s[3], (bsz, t_s, H_B, DH_B)),
            stack(outs_s[4], (bsz, t_s, IDX_DIM)))
```

```python
import functools
import math

import jax
import jax.numpy as jnp
import numpy as np
from jax import lax
from jax.experimental import pallas as pl
from jax.experimental.pallas import tpu as pltpu

CHUNK = 64
ROPE_THETA = 500000.0
EPS = 1e-6
H_A = 8
DH_A = 64
DV_A = 128
H_B = 8
DH_B = 128
IDX_HEADS = 16
IDX_DIM = 64
TOPK_MAX = 256
LANES = 128
GROUPS = 8
SEC = GROUPS * LANES
ROW_GROUP = 128
NEG = -1e30
LOG2E = 1.4426950408889634
MAX_UNSHIFTED = 60.0
INT_MIN = -(2 ** 31)
INT_MAX = 2 ** 31 - 1
VMEM_LIMIT = 56 * 1024 * 1024

_NT = (((1,), (1,)), ((), ()))


def _cdiv(a, b):
    return (a + b - 1) // b


def _params(sem, vmem=VMEM_LIMIT):
    return pltpu.CompilerParams(dimension_semantics=sem, vmem_limit_bytes=vmem)


def _rmsnorm_kernel(x_ref, g_ref, o_ref):
    x = x_ref[...]
    ms = jnp.mean(x * x, axis=-1, keepdims=True)
    o_ref[...] = (x * lax.rsqrt(ms + EPS) * g_ref[...]).astype(o_ref.dtype)


def _rmsnorm(x, g, tm):
    r, d = x.shape
    return pl.pallas_call(
        _rmsnorm_kernel,
        grid=(r // tm,),
        in_specs=[pl.BlockSpec((tm, d), lambda i: (i, 0)),
                  pl.BlockSpec((1, d), lambda i: (0, 0))],
        out_specs=pl.BlockSpec((tm, d), lambda i: (i, 0)),
        out_shape=jax.ShapeDtypeStruct((r, d), jnp.bfloat16),
        compiler_params=_params(("parallel",)),
        name="rmsnorm",
    )(x, g.reshape(1, d))


def _head_norm(z, g, d):
    sq = z * z
    if d == LANES:
        r = lax.rsqrt(jnp.sum(sq, axis=-1, keepdims=True) * (1.0 / d) + EPS)
    else:
        lo = lax.broadcasted_iota(jnp.int32, z.shape, 1) < d
        s_lo = jnp.sum(jnp.where(lo, sq, 0.0), axis=-1, keepdims=True)
        s_hi = jnp.sum(jnp.where(lo, 0.0, sq), axis=-1, keepdims=True)
        r = jnp.where(lo, lax.rsqrt(s_lo * (1.0 / d) + EPS), lax.rsqrt(s_hi * (1.0 / d) + EPS))
    return z * r * g


def _rope(z, c, a, b, half):
    return z * c + pltpu.roll(z, LANES - half, 1) * a + pltpu.roll(z, half, 1) * b


def _proj_kernel(xn_ref, w_ref, g_ref, rc_ref, ra_ref, rb_ref, *out_refs, norm, rope, scale, kinds):
    tm = xn_ref.shape[0]
    chunk = tm // 2 if (norm or rope) and tm % 512 == 0 else tm
    for r0 in range(0, tm, chunk):
        rows = slice(r0, r0 + chunk)
        z = jnp.dot(xn_ref[rows, :], w_ref[...], preferred_element_type=jnp.float32)
        for c in range(GROUPS):
            zc = z[:, c * LANES:(c + 1) * LANES]
            if norm:
                zc = _head_norm(zc, g_ref[...], norm)
            if rope:
                zc = _rope(zc, rc_ref[rows, :], ra_ref[rows, :], rb_ref[rows, :], rope // 8)
            for kind, o_ref in zip(kinds, out_refs):
                if kind == "f32":
                    o_ref[rows, c * LANES:(c + 1) * LANES] = zc
                else:
                    o_ref[c, rows, :] = (zc * scale).astype(o_ref.dtype)


def _proj(xn, w, gain, tables, *, norm, rope, scale, kinds, tm):
    r, d = xn.shape
    nj = w.shape[1] // SEC
    rc, ra, rb = tables
    out_shape, out_specs = [], []
    for kind in kinds:
        if kind == "f32":
            out_shape.append(jax.ShapeDtypeStruct((r, nj * SEC), jnp.float32))
            out_specs.append(pl.BlockSpec((tm, SEC), lambda j, i: (i, j)))
        else:
            assert nj == 1
            out_shape.append(jax.ShapeDtypeStruct((GROUPS, r, LANES), jnp.bfloat16))
            out_specs.append(pl.BlockSpec((GROUPS, tm, LANES), lambda j, i: (0, i, 0)))
    row = pl.BlockSpec((tm, LANES), lambda j, i: (i, 0))
    outs = pl.pallas_call(
        functools.partial(_proj_kernel, norm=norm, rope=rope, scale=scale, kinds=tuple(kinds)),
        grid=(nj, r // tm),
        in_specs=[pl.BlockSpec((tm, d), lambda j, i: (i, 0)),
                  pl.BlockSpec((d, SEC), lambda j, i: (0, j)),
                  pl.BlockSpec((1, LANES), lambda j, i: (0, 0)),
                  row, row, row],
        out_specs=out_specs,
        out_shape=out_shape,
        compiler_params=_params(("parallel", "parallel")),
        name="proj",
    )(xn, w, gain, rc, ra, rb)
    return outs


def _small_proj_kernel(xn_ref, w_ref, g_ref, rc_ref, ra_ref, rb_ref, f_ref, k2_ref, *, wscale):
    z = jnp.dot(xn_ref[...], w_ref[...], preferred_element_type=jnp.float32)
    lo = lax.broadcasted_iota(jnp.int32, z.shape, 1) < IDX_DIM
    ms = jnp.sum(jnp.where(lo, z * z, 0.0), axis=-1, keepdims=True) * (1.0 / IDX_DIM)
    zn = z * lax.rsqrt(ms + EPS) * g_ref[...]
    zr = _rope(zn, rc_ref[...], ra_ref[...], rb_ref[...], IDX_DIM // 8)
    f_ref[...] = jnp.where(lo, zr, z * wscale)
    k2_ref[...] = jnp.where(lo, zr, pltpu.roll(zr, IDX_DIM, 1)).astype(k2_ref.dtype)


def _small_proj(xn, w, gain, tables, tm):
    r, d = xn.shape
    rc, ra, rb = tables
    row = pl.BlockSpec((tm, LANES), lambda i: (i, 0))
    return pl.pallas_call(
        functools.partial(_small_proj_kernel, wscale=IDX_HEADS ** -0.5),
        grid=(r // tm,),
        in_specs=[pl.BlockSpec((tm, d), lambda i: (i, 0)),
                  pl.BlockSpec((d, LANES), lambda i: (0, 0)),
                  pl.BlockSpec((1, LANES), lambda i: (0, 0)),
                  row, row, row],
        out_specs=[row, row],
        out_shape=[jax.ShapeDtypeStruct((r, LANES), jnp.float32),
                   jax.ShapeDtypeStruct((r, LANES), jnp.bfloat16)],
        compiler_params=_params(("parallel",)),
        name="small_proj",
    )(xn, w, gain, rc, ra, rb)


def _rope_tables(pos, d):
    r = d // 4
    h = r // 2
    n = pos.shape[0]
    inv = ROPE_THETA ** (-jnp.arange(0, r, 2, dtype=jnp.float32) / r)
    ang = pos.astype(jnp.float32)[:, None] * inv[None, :]
    cos, sin = jnp.cos(ang), jnp.sin(ang)
    zh = jnp.zeros((n, h), jnp.float32)
    zr = jnp.zeros((n, d - r), jnp.float32)
    c = jnp.concatenate([cos, cos, jnp.ones((n, d - r), jnp.float32)], axis=-1)
    a = jnp.concatenate([-sin, zh, zr], axis=-1)
    b = jnp.concatenate([zh, sin, zr], axis=-1)
    rep = LANES // d
    return tuple(jnp.tile(t, (1, rep)) for t in (c, a, b))


def _online_update(s, v, m_ref, l_ref, acc_ref):
    m_prev = m_ref[...]
    m_new = jnp.maximum(m_prev, jnp.max(s, axis=-1, keepdims=True))
    alpha = jnp.exp2(m_prev - m_new)
    p = jnp.exp2(s - m_new)
    l_ref[...] = alpha * l_ref[...] + jnp.sum(p, axis=-1, keepdims=True)
    acc_ref[...] = alpha * acc_ref[...] + jnp.dot(p.astype(v.dtype), v, preferred_element_type=jnp.float32)
    m_ref[...] = m_new


def _causal_steps(t, tq, tk):
    qs, ks = [], []
    for qi in range(t // tq):
        nk = _cdiv((qi + 1) * tq, tk)
        qs += [qi] * nk
        ks += list(range(nk))
    return np.asarray(qs, np.int32), np.asarray(ks, np.int32)


def _unshifted_ok(gq, gk, d, scale):
    bound = jnp.max(jnp.abs(gq)) * jnp.max(jnp.abs(gk)) * (d * scale * (1.0 + 2.0 ** -6))
    return (bound <= MAX_UNSHIFTED).astype(jnp.int32).reshape(1)


def _lambda(lam_ref, lam_init):
    lp = lam_ref[...]
    s1 = jnp.sum(lp[0:1] * lp[1:2], axis=-1, keepdims=True)
    s2 = jnp.sum(lp[2:3] * lp[3:4], axis=-1, keepdims=True)
    return jnp.exp(s1) - jnp.exp(s2) + lam_init


def _diff_finish(o0, o1, lam, sg, lam_init):
    o = o0 - lam * o1
    ms = jnp.mean(o * o, axis=-1, keepdims=True)
    return o * lax.rsqrt(ms + EPS) * sg * (1.0 - lam_init)


def _chunk_end(pos):
    return (pos // CHUNK + 1) * CHUNK - 1


def _diff_attn_kernel(qt_ref, kt_ref, ok_ref, q_ref, k_ref, v_ref, lam_ref, sg_ref, o_ref,
                      qs_ref, v1_ref, acc_ref, m_ref, l_ref, *, tq, tk, lam_init):
    step_id = pl.program_id(1)
    qi = qt_ref[step_id]
    kj = kt_ref[step_id]
    nk = _cdiv((qi + 1) * tq, tk)
    unshifted = ok_ref[0] == 1
    full = (kj + 1) * tk <= qi * tq + CHUNK
    rows2 = 2 * tq

    @pl.when(kj == 0)
    def _():
        q = q_ref[...]
        lo = lax.broadcasted_iota(jnp.int32, q.shape, 1) < DH_A
        zero = jnp.zeros_like(q)
        qs_ref[0:tq, :] = jnp.where(lo, q, zero)
        qs_ref[tq:rows2, :] = jnp.where(lo, zero, q)
        v1_ref[:, DV_A:] = jnp.ones((tk, LANES), v1_ref.dtype)
        m_ref[...] = jnp.full(m_ref.shape, NEG, jnp.float32)
        l_ref[...] = jnp.zeros(l_ref.shape, jnp.float32)
        acc_ref[...] = jnp.zeros(acc_ref.shape, jnp.float32)

    v1_ref[:, :DV_A] = v_ref[...]

    def visible(r0, nrows):
        row = r0 + lax.broadcasted_iota(jnp.int32, (nrows, tk), 0)
        row = jnp.where(row >= tq, row - tq, row)
        kpos = kj * tk + lax.broadcasted_iota(jnp.int32, (nrows, tk), 1)
        return kpos <= _chunk_end(qi * tq + row)

    def fast_step(masked):
        k = k_ref[...]
        v1 = v1_ref[...]
        for r0 in range(0, rows2, ROW_GROUP):
            rows = slice(r0, r0 + ROW_GROUP)
            s = lax.dot_general(qs_ref[rows, :], k, _NT, preferred_element_type=jnp.float32)
            if masked:
                s = jnp.where(visible(r0, ROW_GROUP), s, NEG)
            p = jnp.exp2(s).astype(v1.dtype)
            acc_ref[rows, :] += jnp.dot(p, v1, preferred_element_type=jnp.float32)

    def exact_step(masked):
        s = lax.dot_general(qs_ref[...], k_ref[...], _NT, preferred_element_type=jnp.float32)
        if masked:
            s = jnp.where(visible(0, rows2), s, NEG)
        _online_update(s, v_ref[...], m_ref, l_ref, acc_ref.at[:, :DV_A])

    for fast, step in ((True, fast_step), (False, exact_step)):
        mode = unshifted if fast else jnp.logical_not(unshifted)

        @pl.when(jnp.logical_and(mode, full))
        def _():
            step(False)

        @pl.when(jnp.logical_and(mode, jnp.logical_not(full)))
        def _():
            step(True)

    @pl.when(kj == nk - 1)
    def _():
        @pl.when(jnp.logical_not(unshifted))
        def _():
            acc_ref[:, DV_A:] = jnp.broadcast_to(l_ref[...], (rows2, LANES))

        o = acc_ref[:, :DV_A] / acc_ref[:, DV_A:]
        o_ref[...] = _diff_finish(o[0:tq], o[tq:rows2], _lambda(lam_ref, lam_init), sg_ref[...], lam_init)


def _diff_attn(q, k, v, ok, lam_p, sg, lam_init, tq, tk):
    _, t, _ = q.shape
    qt, kt = _causal_steps(t, tq, tk)
    kv_spec = pl.BlockSpec((None, tk, LANES), lambda h, s, qt, kt, ok: (h, kt[s], 0))
    const = lambda h, s, qt, kt, ok: (0, 0)
    grid_spec = pltpu.PrefetchScalarGridSpec(
        num_scalar_prefetch=3,
        grid=(H_A, len(qt)),
        in_specs=[pl.BlockSpec((None, tq, LANES), lambda h, s, qt, kt, ok: (h, qt[s], 0)),
                  kv_spec, kv_spec,
                  pl.BlockSpec((4, DH_A), const),
                  pl.BlockSpec((1, DV_A), const)],
        out_specs=pl.BlockSpec((tq, LANES), lambda h, s, qt, kt, ok: (qt[s], h)),
        scratch_shapes=[pltpu.VMEM((2 * tq, LANES), jnp.bfloat16),
                        pltpu.VMEM((tk, DV_A + LANES), jnp.bfloat16),
                        pltpu.VMEM((2 * tq, DV_A + LANES), jnp.float32),
                        pltpu.VMEM((2 * tq, 1), jnp.float32),
                        pltpu.VMEM((2 * tq, 1), jnp.float32)])
    return pl.pallas_call(
        functools.partial(_diff_attn_kernel, tq=tq, tk=tk, lam_init=lam_init),
        grid_spec=grid_spec,
        out_shape=jax.ShapeDtypeStruct((t, H_A * DV_A), jnp.float32),
        compiler_params=_params(("parallel", "arbitrary")),
        name="diff_attn",
    )(jnp.asarray(qt), jnp.asarray(kt), ok, q, k, v, lam_p, sg)


def _sort_key(score):
    bits = pltpu.bitcast(score, jnp.int32)
    key = jnp.where(bits < 0, bits ^ INT_MAX, bits)
    return jnp.where(key == -1, 0, key)


def _index_kernel(qt_ref, kt_ref, q_ref, w_ref, k_ref, bias_ref, qs_ref, key_ref, *, tq, tk, t, topk):
    step_id = pl.program_id(0)
    qi = qt_ref[step_id]
    kj = kt_ref[step_id]
    nk = _cdiv((qi + 1) * tq, tk)
    nsub = tk // LANES

    @pl.when(kj == 0)
    def _():
        for p in range(GROUPS):
            q = q_ref[p]
            lo = lax.broadcasted_iota(jnp.int32, q.shape, 1) < IDX_DIM
            zero = jnp.zeros_like(q)
            qs_ref[(2 * p) * tq:(2 * p + 1) * tq, :] = jnp.where(lo, q, zero)
            qs_ref[(2 * p + 1) * tq:(2 * p + 2) * tq, :] = jnp.where(lo, zero, q)

    logits = lax.dot_general(qs_ref[...], k_ref[...], _NT, preferred_element_type=jnp.float32)
    w = w_ref[...]
    score = None
    for h in range(IDX_HEADS):
        term = jnp.maximum(logits[h * tq:(h + 1) * tq], 0.0) * w[:, h:h + 1]
        score = term if score is None else score + term
    qpos = qi * tq + lax.broadcasted_iota(jnp.int32, score.shape, 0)
    kpos = kj * tk + lax.broadcasted_iota(jnp.int32, score.shape, 1)
    key = jnp.where(kpos <= _chunk_end(qpos), _sort_key(score), INT_MIN)
    key_ref[:, pl.ds(pl.multiple_of(kj * tk, tk), tk)] = key

    @pl.when(kj == nk - 1)
    def _():
        lane = lax.broadcasted_iota(jnp.int32, (tq, LANES), 1)
        cend = _chunk_end(qi * tq + lax.broadcasted_iota(jnp.int32, (tq, LANES), 0))

        def rows(x):
            return jnp.broadcast_to(x, (tq, LANES))

        def sweep(fn, init):
            def body(j, acc):
                for c in range(nsub):
                    off = pl.multiple_of(j * tk + c * LANES, LANES)
                    acc = fn(acc, key_ref[:, pl.ds(off, LANES)], off + lane, c % 2)
                return acc
            acc = lax.fori_loop(0, nk // 2, lambda j, a: body(2 * j + 1, body(2 * j, a)), init)
            return lax.fori_loop(nk - nk % 2, nk, body, acc)

        def count(pred):
            acc = sweep(lambda a, kt, idx, par: a + jnp.where(pred(kt, idx), 1.0, 0.0),
                        jnp.zeros((tq, LANES), jnp.float32))
            return rows(jnp.sum(acc, axis=1, keepdims=True))

        def key_score(key):
            f = pltpu.bitcast(jnp.where(key < 0, key ^ INT_MAX, key), jnp.float32)
            return jnp.where(key == INT_MIN, -jnp.inf, f)

        kf = float(topk)
        imin = jnp.full((tq, LANES), INT_MIN, jnp.int32)

        def group_max(acc, kt, idx, par):
            return (jnp.maximum(acc[0], kt), acc[1]) if par == 0 else (acc[0], jnp.maximum(acc[1], kt))

        m0, m1 = sweep(group_max, (imin, imin))
        m0, m1 = key_score(m0), key_score(m1)
        lo = _sort_key(rows(jnp.min(jnp.minimum(m0, m1), axis=1, keepdims=True)))
        hi = _sort_key(rows(jnp.max(jnp.maximum(m0, m1), axis=1, keepdims=True))) + 1
        c_lo = count(lambda kt, idx: kt >= lo)
        c_hi = jnp.zeros((tq, LANES), jnp.float32)
        all_sel = cend + 1 <= topk
        lo = jnp.where(all_sel, INT_MIN + 1, lo)
        done = jnp.where(jnp.logical_or(all_sel, c_lo == kf), 1.0, 0.0)

        def search_cond(st):
            return jnp.logical_and(st[0] < 80, st[6] < 0.5)

        def search_body(st):
            it, lo, hi, c_lo, c_hi, done, _ = st
            mid = (lo >> 1) + (hi >> 1) + (lo & hi & 1)
            lo_f, hi_f = key_score(lo), key_score(hi)
            frac = (jnp.log(c_lo) - math.log(kf)) / (jnp.log(c_lo) - jnp.log(jnp.maximum(c_hi, 0.5)))
            t_f = lo_f + (hi_f - lo_f) * frac
            interp = jnp.logical_and((it & 1) == 0, jnp.abs(t_f) < jnp.inf)
            t = jnp.where(interp, _sort_key(t_f), mid)
            t = jnp.minimum(jnp.maximum(t, lo + 1), hi - 1)
            c = count(lambda kt, idx: kt >= t)
            up = jnp.logical_and(done < 0.5, c >= kf)
            dn = jnp.logical_and(done < 0.5, c < kf)
            lo, c_lo = jnp.where(up, t, lo), jnp.where(up, c, c_lo)
            hi, c_hi = jnp.where(dn, t, hi), jnp.where(dn, c, c_hi)
            done = jnp.where(jnp.logical_or(c_lo == kf, hi == lo + 1), 1.0, done)
            return it + 1, lo, hi, c_lo, c_hi, done, jnp.min(done)

        st = (jnp.int32(0), lo, hi, c_lo, c_hi, done, jnp.min(done))
        _, thr, hi, c_lo, c_hi, _, _ = lax.while_loop(search_cond, search_body, st)
        tied = jnp.logical_and(jnp.logical_not(all_sel), c_lo > kf)
        need = jnp.where(tied, kf - c_hi, jnp.inf)

        def write(select):
            def body(j, carry):
                for c in range(nsub):
                    off = pl.multiple_of(j * tk + c * LANES, LANES)
                    sel = select(key_ref[:, pl.ds(off, LANES)], off + lane)
                    bias_ref[:, pl.ds(off, LANES)] = jnp.where(sel, 0.0, NEG).astype(bias_ref.dtype)
                return carry
            lax.fori_loop(0, nk, body, 0)

        any_tied = jnp.max(jnp.where(tied, 1.0, 0.0)) > 0.5

        @pl.when(jnp.logical_not(any_tied))
        def _():
            write(lambda kt, idx: kt >= thr)

        @pl.when(any_tied)
        def _():
            nbits = max(1, (t - 1).bit_length())

            def pos_body(b, pos):
                cand = pos | lax.shift_left(jnp.int32(1), jnp.int32(nbits - 1) - b)
                g = count(lambda kt, idx: jnp.logical_and(kt == thr, idx < cand))
                return jnp.where(g < need, cand, pos)

            last = lax.fori_loop(0, nbits, pos_body, jnp.zeros((tq, LANES), jnp.int32))
            write(lambda kt, idx: jnp.logical_or(kt > thr, jnp.logical_and(kt == thr, idx <= last)))

        def fill_body(j, carry):
            off = pl.multiple_of(j * tk, tk)
            bias_ref[:, pl.ds(off, tk)] = jnp.full((tq, tk), NEG, bias_ref.dtype)
            return carry

        lax.fori_loop(nk, t // tk, fill_body, 0)


def _index_select(qi16, wi, ki2, topk, tq, tk):
    _, t, _ = qi16.shape
    qt, kt = _causal_steps(t, tq, tk)
    grid_spec = pltpu.PrefetchScalarGridSpec(
        num_scalar_prefetch=2,
        grid=(len(qt),),
        in_specs=[pl.BlockSpec((GROUPS, tq, LANES), lambda s, qt, kt: (0, qt[s], 0)),
                  pl.BlockSpec((tq, IDX_HEADS), lambda s, qt, kt: (qt[s], 0)),
                  pl.BlockSpec((tk, LANES), lambda s, qt, kt: (kt[s], 0))],
        out_specs=pl.BlockSpec((tq, t), lambda s, qt, kt: (qt[s], 0)),
        scratch_shapes=[pltpu.VMEM((IDX_HEADS * tq, LANES), jnp.bfloat16),
                        pltpu.VMEM((tq, t), jnp.int32)])
    return pl.pallas_call(
        functools.partial(_index_kernel, tq=tq, tk=tk, t=t, topk=topk),
        grid_spec=grid_spec,
        out_shape=jax.ShapeDtypeStruct((t, t), jnp.bfloat16),
        compiler_params=_params(("arbitrary",)),
        name="index_select",
    )(jnp.asarray(qt), jnp.asarray(kt), qi16, wi, ki2)


def _dsa_attn_kernel(qt_ref, kt_ref, ok_ref, q_ref, k_ref, v_ref, bias_ref, o_ref,
                     v1_ref, acc_ref, m_ref, l_ref, *, tq, tk):
    step_id = pl.program_id(0)
    qi = qt_ref[step_id]
    kj = kt_ref[step_id]
    nk = _cdiv((qi + 1) * tq, tk)
    unshifted = ok_ref[0] == 1

    @pl.when(kj == 0)
    def _():
        v1_ref[:, :, DH_B:] = jnp.ones((H_B, tk, LANES), v1_ref.dtype)
        m_ref[...] = jnp.full(m_ref.shape, NEG, jnp.float32)
        l_ref[...] = jnp.zeros(l_ref.shape, jnp.float32)
        acc_ref[...] = jnp.zeros(acc_ref.shape, jnp.float32)

    @pl.when(unshifted)
    def _():
        v1_ref[:, :, :DH_B] = v_ref[...]
        for r0 in range(0, tq, ROW_GROUP):
            rows = slice(r0, r0 + ROW_GROUP)
            bias = bias_ref[rows, :].astype(jnp.float32)
            for h in range(H_B):
                s = lax.dot_general(q_ref[h, rows, :], k_ref[h], _NT, preferred_element_type=jnp.float32) + bias
                p = jnp.exp2(s).astype(v1_ref.dtype)
                acc_ref[h, rows, :] += jnp.dot(p, v1_ref[h], preferred_element_type=jnp.float32)

    @pl.when(jnp.logical_not(unshifted))
    def _():
        bias = bias_ref[...].astype(jnp.float32)
        for h in range(H_B):
            s = lax.dot_general(q_ref[h], k_ref[h], _NT, preferred_element_type=jnp.float32) + bias
            _online_update(s, v_ref[h], m_ref.at[h], l_ref.at[h], acc_ref.at[h, :, :DH_B])

    @pl.when(kj == nk - 1)
    def _():
        @pl.when(jnp.logical_not(unshifted))
        def _():
            acc_ref[:, :, DH_B:] = jnp.broadcast_to(l_ref[...], (H_B, tq, LANES))

        for h in range(H_B):
            o_ref[:, h * LANES:(h + 1) * LANES] = acc_ref[h, :, :DH_B] / acc_ref[h, :, DH_B:]


def _dsa_attn(q, k, v, bias, ok, tq, tk):
    _, t, _ = q.shape
    qt, kt = _causal_steps(t, tq, tk)
    kv_spec = pl.BlockSpec((H_B, tk, LANES), lambda s, qt, kt, ok: (0, kt[s], 0))
    grid_spec = pltpu.PrefetchScalarGridSpec(
        num_scalar_prefetch=3,
        grid=(len(qt),),
        in_specs=[pl.BlockSpec((H_B, tq, LANES), lambda s, qt, kt, ok: (0, qt[s], 0)),
                  kv_spec, kv_spec,
                  pl.BlockSpec((tq, tk), lambda s, qt, kt, ok: (qt[s], kt[s]))],
        out_specs=pl.BlockSpec((tq, H_B * DH_B), lambda s, qt, kt, ok: (qt[s], 0)),
        scratch_shapes=[pltpu.VMEM((H_B, tk, DH_B + LANES), jnp.bfloat16),
                        pltpu.VMEM((H_B, tq, DH_B + LANES), jnp.float32),
                        pltpu.VMEM((H_B, tq, 1), jnp.float32),
                        pltpu.VMEM((H_B, tq, 1), jnp.float32)])
    return pl.pallas_call(
        functools.partial(_dsa_attn_kernel, tq=tq, tk=tk),
        grid_spec=grid_spec,
        out_shape=jax.ShapeDtypeStruct((t, H_B * DH_B), jnp.float32),
        compiler_params=_params(("arbitrary",)),
        name="dsa_attn",
    )(jnp.asarray(qt), jnp.asarray(kt), ok, q, k, v, bias)


def _sample_kernel(qa_ref, qb_ref, qi_ref, wi_ref, ckd_ref, cvd_ref, ckb_ref, cvb_ref, cki_ref,
                   nkd_ref, nvd_ref, nkb_ref, nvb_ref, nki_ref, lam_ref, sg_ref,
                   oa_ref, ob_ref,
                   qda_ref, qdb_ref, bias_ref, key_ref, ma_ref, la_ref, acca_ref, mb_ref, lb_ref, accb_ref,
                   *, ts, past, tk, topk, lam_init):
    kj = pl.program_id(1)
    nkt = past // tk
    width = past + LANES
    ra = 2 * H_A * ts
    rb = H_B * ts

    def new_key_valid(rows):
        t_q = lax.broadcasted_iota(jnp.int32, (rows, LANES), 0) % ts
        lane = lax.broadcasted_iota(jnp.int32, (rows, LANES), 1)
        return jnp.logical_and(lane < ts, past + lane <= _chunk_end(past + t_q))

    @pl.when(kj == 0)
    def _():
        qa = qa_ref[...]
        grp = lax.broadcasted_iota(jnp.int32, qa.shape, 1) // DH_A
        for hc in range(2 * H_A):
            qda_ref[hc * ts:(hc + 1) * ts, :] = jnp.where(grp == hc, qa, jnp.zeros_like(qa))
        qb = qb_ref[...]
        grp = lax.broadcasted_iota(jnp.int32, qb.shape, 1) // DH_B
        for h in range(H_B):
            qdb_ref[h * ts:(h + 1) * ts, :] = jnp.where(grp == h, qb, jnp.zeros_like(qb))
        for ref in (ma_ref, mb_ref):
            ref[...] = jnp.full(ref.shape, NEG, jnp.float32)
        for ref in (la_ref, lb_ref, acca_ref, accb_ref):
            ref[...] = jnp.zeros(ref.shape, jnp.float32)

        qi = qi_ref[...]
        w = wi_ref[...]

        def scores(keys16):
            logits = lax.dot_general(qi, keys16, _NT, preferred_element_type=jnp.float32)
            score = None
            for h in range(IDX_HEADS):
                term = jnp.maximum(logits[h * ts:(h + 1) * ts], 0.0) * w[h * ts:(h + 1) * ts]
                score = term if score is None else score + term
            return score

        step = min(past, 1024)
        for c in range(past // step):
            keys16 = cki_ref[c * step:(c + 1) * step, :].astype(jnp.bfloat16)
            key_ref[:, c * step:(c + 1) * step] = _sort_key(scores(keys16))
        key_ref[:, past:width] = jnp.where(new_key_valid(ts), _sort_key(scores(nki_ref[...])), INT_MIN)

        kf = float(topk)

        def count(pred):
            idx = lax.broadcasted_iota(jnp.int32, (ts, width), 1)
            c = jnp.sum(jnp.where(pred(key_ref[...], idx), 1.0, 0.0), axis=1, keepdims=True)
            return jnp.broadcast_to(c, (ts, LANES))

        def wide(x):
            return x[:, 0:1]

        c0 = count(lambda kt, idx: kt >= 0)
        prefix = jnp.where(c0 >= kf, 0, INT_MIN).astype(jnp.int32)

        def bit_body(b, prefix):
            cand = prefix | lax.shift_left(jnp.int32(1), jnp.int32(30) - b)
            c = count(lambda kt, idx: kt >= wide(cand))
            return jnp.where(c >= kf, cand, prefix)

        thr = lax.fori_loop(0, 31, bit_body, prefix)
        n_gt = count(lambda kt, idx: kt > wide(thr))
        need = kf - n_gt
        nbits = max(1, (width - 1).bit_length())

        def pos_body(b, pos):
            cand = pos | lax.shift_left(jnp.int32(1), jnp.int32(nbits - 1) - b)
            g = count(lambda kt, idx: jnp.logical_and(kt == wide(thr), idx < wide(cand)))
            return jnp.where(g < need, cand, pos)

        last = lax.fori_loop(0, nbits, pos_body, jnp.zeros((ts, LANES), jnp.int32))
        kt = key_ref[...]
        idx = lax.broadcasted_iota(jnp.int32, (ts, width), 1)
        sel = jnp.logical_or(kt > wide(thr), jnp.logical_and(kt == wide(thr), idx <= wide(last)))
        sel = jnp.logical_and(sel, kt != INT_MIN)
        bias = jnp.where(sel, 0.0, NEG)
        for h in range(H_B):
            bias_ref[h * ts:(h + 1) * ts, :] = bias

    def head_rows(ref):
        return jnp.concatenate([ref[pl.ds(h, tk, stride=GROUPS), :].astype(jnp.bfloat16) for h in range(GROUPS)],
                               axis=1)

    sa = lax.dot_general(qda_ref[...], ckd_ref[...].astype(jnp.bfloat16), _NT, preferred_element_type=jnp.float32)
    _online_update(sa, head_rows(cvd_ref), ma_ref, la_ref, acca_ref)
    sb = lax.dot_general(qdb_ref[...], head_rows(ckb_ref), _NT, preferred_element_type=jnp.float32)
    sb = sb + bias_ref[:, pl.ds(pl.multiple_of(kj * tk, tk), tk)]
    _online_update(sb, head_rows(cvb_ref), mb_ref, lb_ref, accb_ref)

    @pl.when(kj == nkt - 1)
    def _():
        sa = lax.dot_general(qda_ref[...], nkd_ref[...], _NT, preferred_element_type=jnp.float32)
        sa = jnp.where(new_key_valid(ra), sa, NEG)
        _online_update(sa, nvd_ref[...], ma_ref, la_ref, acca_ref)
        sb = lax.dot_general(qdb_ref[...], nkb_ref[...], _NT, preferred_element_type=jnp.float32)
        sb = sb + bias_ref[:, past:width]
        _online_update(sb, nvb_ref[...], mb_ref, lb_ref, accb_ref)

        lam = _lambda(lam_ref, lam_init)
        oa = acca_ref[...] / la_ref[...]
        ob = accb_ref[...] / lb_ref[...]
        for h in range(H_A):
            cols = slice(h * DV_A, (h + 1) * DV_A)
            o0 = oa[(2 * h) * ts:(2 * h + 1) * ts, cols]
            o1 = oa[(2 * h + 1) * ts:(2 * h + 2) * ts, cols]
            oa_ref[:, cols] = _diff_finish(o0, o1, lam, sg_ref[...], lam_init)
        for h in range(H_B):
            cols = slice(h * DH_B, (h + 1) * DH_B)
            ob_ref[:, cols] = ob[h * ts:(h + 1) * ts, cols]


def _sample_attn(l, qa, qb, qi, wi, caches, news, lam_p, sg, lam_init, topk, tk):
    bsz, ts, _ = qa.shape
    ckd, cvd, ckb, cvb, cki = caches
    past = cki.shape[2]
    width = past + LANES
    ra, rb = 2 * H_A * ts, H_B * ts
    cache_spec = pl.BlockSpec((None, None, tk, SEC), lambda b, kj: (l, b, kj, 0))
    heads_spec = pl.BlockSpec((None, None, tk * GROUPS, LANES), lambda b, kj: (l, b, kj, 0))
    new_spec = pl.BlockSpec((None, LANES, SEC), lambda b, kj: (b, 0, 0))
    row_spec = pl.BlockSpec((None, ts, SEC), lambda b, kj: (b, 0, 0))
    return pl.pallas_call(
        functools.partial(_sample_kernel, ts=ts, past=past, tk=tk, topk=topk, lam_init=lam_init),
        grid=(bsz, past // tk),
        in_specs=[row_spec, row_spec,
                  pl.BlockSpec((None, IDX_HEADS * ts, IDX_DIM), lambda b, kj: (b, 0, 0)),
                  pl.BlockSpec((None, IDX_HEADS * ts, 1), lambda b, kj: (b, 0, 0)),
                  cache_spec, heads_spec, heads_spec, heads_spec,
                  pl.BlockSpec((None, None, past, IDX_DIM), lambda b, kj: (l, b, 0, 0)),
                  new_spec, new_spec, new_spec, new_spec,
                  pl.BlockSpec((None, LANES, IDX_DIM), lambda b, kj: (b, 0, 0)),
                  pl.BlockSpec((4, DH_A), lambda b, kj: (0, 0)),
                  pl.BlockSpec((1, DV_A), lambda b, kj: (0, 0))],
        out_specs=[row_spec, row_spec],
        out_shape=[jax.ShapeDtypeStruct((bsz, ts, SEC), jnp.float32),
                   jax.ShapeDtypeStruct((bsz, ts, SEC), jnp.float32)],
        scratch_shapes=[pltpu.VMEM((ra, SEC), jnp.bfloat16),
                        pltpu.VMEM((rb, SEC), jnp.bfloat16),
                        pltpu.VMEM((rb, width), jnp.float32),
                        pltpu.VMEM((ts, width), jnp.int32),
                        pltpu.VMEM((ra, 1), jnp.float32),
                        pltpu.VMEM((ra, 1), jnp.float32),
                        pltpu.VMEM((ra, SEC), jnp.float32),
                        pltpu.VMEM((rb, 1), jnp.float32),
                        pltpu.VMEM((rb, 1), jnp.float32),
                        pltpu.VMEM((rb, SEC), jnp.float32)],
        compiler_params=_params(("parallel", "arbitrary")),
        name="sample_attn",
    )(qa, qb, qi, wi, ckd, cvd, ckb, cvb, cki, *news, lam_p, sg)


def _silu(x):
    return x * jax.nn.sigmoid(x)


def _merge_kernel(oa_ref, ga_ref, ob_ref, gb_ref, mg_ref, wa_ref, wb_ref, u_ref):
    d = u_ref.shape[-1]
    ya = jnp.dot((oa_ref[...] * _silu(ga_ref[...])).astype(jnp.bfloat16), wa_ref[...],
                 preferred_element_type=jnp.float32)
    yb = jnp.dot((ob_ref[...] * _silu(gb_ref[...])).astype(jnp.bfloat16), wb_ref[...],
                 preferred_element_type=jnp.float32)
    mg = mg_ref[...]
    u = jax.nn.sigmoid(mg[:, :d]) * ya + jax.nn.sigmoid(mg[:, d:]) * yb
    u_ref[...] = u.astype(u_ref.dtype)


def _resid_kernel(x_ref, u_ref, wo_ref, g_ref, h_ref, hn_ref):
    h = x_ref[...] + jnp.dot(u_ref[...], wo_ref[...], preferred_element_type=jnp.float32)
    h_ref[...] = h
    ms = jnp.mean(h * h, axis=-1, keepdims=True)
    hn_ref[...] = (h * lax.rsqrt(ms + EPS) * g_ref[...]).astype(hn_ref.dtype)


def _ple_kernel(h_ref, hn_ref, p_ref, wg_ref, wp_ref, o_ref):
    gate = jax.nn.sigmoid(jnp.dot(hn_ref[...], wg_ref[...], preferred_element_type=jnp.float32))
    e = jnp.dot(p_ref[...].astype(jnp.bfloat16), wp_ref[...], preferred_element_type=jnp.float32)
    o_ref[...] = h_ref[...] + gate * e


def _layer_output(x, p, oa, ga, ob, gb, mg, w_ba, w_bb, w_out, ple_g, w_pg, w_ple, tm):
    r, d = x.shape
    wa = oa.shape[1]
    dp = p.shape[1]
    rows = lambda n: pl.BlockSpec((tm, n), lambda i: (i, 0))
    full = lambda a, b: pl.BlockSpec((a, b), lambda i: (0, 0))
    u = pl.pallas_call(
        _merge_kernel,
        grid=(r // tm,),
        in_specs=[rows(wa), rows(wa), rows(wa), rows(wa), rows(2 * d), full(wa, d), full(wa, d)],
        out_specs=rows(d),
        out_shape=jax.ShapeDtypeStruct((r, d), jnp.bfloat16),
        compiler_params=_params(("parallel",)),
        name="merge",
    )(oa, ga, ob, gb, mg, w_ba, w_bb)
    h, hn = pl.pallas_call(
        _resid_kernel,
        grid=(r // tm,),
        in_specs=[rows(d), rows(d), full(d, d), full(1, d)],
        out_specs=[rows(d), rows(d)],
        out_shape=[jax.ShapeDtypeStruct((r, d), jnp.float32), jax.ShapeDtypeStruct((r, d), jnp.bfloat16)],
        compiler_params=_params(("parallel",)),
        name="resid",
    )(x, u, w_out, ple_g)
    return pl.pallas_call(
        _ple_kernel,
        grid=(r // tm,),
        in_specs=[rows(d), rows(d), rows(dp), full(d, d), full(dp, d)],
        out_specs=rows(d),
        out_shape=jax.ShapeDtypeStruct((r, d), jnp.float32),
        compiler_params=_params(("parallel",)),
        name="ple",
    )(h, hn, p, w_pg, w_ple)


def _tile(n, pref):
    t = min(n, pref)
    while n % t:
        t //= 2
    return t


def _layer_inputs(x, pos, lw, tm):
    xn = _rmsnorm(x, lw["ln_g"], tm)
    t64 = _rope_tables(pos, DH_A)
    t128 = _rope_tables(pos, DH_B)
    none = t64
    one = jnp.ones((1, LANES), jnp.float32)
    w = lw["w_sec"]
    pj = functools.partial(_proj, xn, tm=tm)
    (qa16,) = pj(w["qa"], lw["qn_a"], t64, norm=DH_A, rope=DH_A, scale=DH_A ** -0.5 * LOG2E, kinds=["hm16"])
    ka, ka16 = pj(w["ka"], lw["kn_a"], t64, norm=DH_A, rope=DH_A, scale=1.0, kinds=["f32", "hm16"])
    va, va16 = pj(w["va"], one, none, norm=0, rope=0, scale=1.0, kinds=["f32", "hm16"])
    (ga,) = pj(w["ga"], one, none, norm=0, rope=0, scale=1.0, kinds=["f32"])
    (qb16,) = pj(w["qb"], lw["qn_b"], t128, norm=DH_B, rope=DH_B, scale=DH_B ** -0.5 * LOG2E, kinds=["hm16"])
    kb, kb16 = pj(w["kb"], lw["kn_b"], t128, norm=DH_B, rope=DH_B, scale=1.0, kinds=["f32", "hm16"])
    vb, vb16 = pj(w["vb"], one, none, norm=0, rope=0, scale=1.0, kinds=["f32", "hm16"])
    (gb,) = pj(w["gb"], one, none, norm=0, rope=0, scale=1.0, kinds=["f32"])
    (qi16,) = pj(w["qi"], one, t64, norm=0, rope=DH_A, scale=IDX_DIM ** -0.5, kinds=["hm16"])
    (mg,) = pj(w["mg"], one, none, norm=0, rope=0, scale=1.0, kinds=["f32"])
    small, ki2 = _small_proj(xn, w["small"], lw["kn_i"], t64, tm)
    ki = small[:, :IDX_DIM]
    wi = small[:, IDX_DIM:IDX_DIM + IDX_HEADS]
    return dict(qa16=qa16, ka=ka, ka16=ka16, va=va, va16=va16, ga=ga, qb16=qb16, kb=kb, kb16=kb16,
                vb=vb, vb16=vb16, gb=gb, qi16=qi16, mg=mg, ki=ki, ki2=ki2, wi=wi)


def _row_major(hm):
    g, r, n = hm.shape
    return jnp.transpose(hm, (1, 0, 2)).reshape(r, g * n)


def _pad_new(a, bsz, ts):
    a = a.reshape(bsz, ts, a.shape[-1]).astype(jnp.bfloat16)
    return jnp.pad(a, ((0, 0), (0, LANES - ts), (0, 0)))


def kernel(x_prompt, x_sample, p_prompt, p_sample, cache_diff_k, cache_diff_v, cache_dsa_k, cache_dsa_v, cache_idx_k, ln_g, w_in, q_norm_a, k_norm_a, lam_q1, lam_k1, lam_q2, lam_k2, subln_a, q_norm_b, k_norm_b, k_norm_idx, w_branch_a, w_branch_b, w_out, ple_norm, w_ple_gate, w_ple):
    depth = w_in.shape[0]
    bp, t_p, d_model = x_prompt.shape
    bsz, t_s, _ = x_sample.shape
    past = cache_diff_k.shape[2]
    assert bp == 1 and t_s <= LANES and t_p % LANES == 0 and past % LANES == 0
    topk_p = min(TOPK_MAX, t_p // 4)
    topk_s = min(TOPK_MAX, (past + t_s) // 4)
    pos_p = jnp.arange(t_p)
    pos_s = jnp.tile(past + jnp.arange(t_s), bsz)
    r_s = bsz * t_s

    tm_p = _tile(t_p, 512)
    tm_s = _tile(r_s, 512)
    tq_a, tk_a = _tile(t_p, 512), _tile(t_p, 1024)
    tq_i, tk_i = _tile(t_p, 128), _tile(t_p, 1024)
    tq_b, tk_b = _tile(t_p, 512), _tile(t_p, 1024)
    tk_s = _tile(past, 512)
    tm_o = _tile(t_p, 256)
    tm_os = _tile(r_s, 256)

    by_head = lambda c: c.reshape(depth, bsz, past * GROUPS, LANES)
    caches = (cache_diff_k.reshape(depth, bsz, past, SEC), by_head(cache_diff_v),
              by_head(cache_dsa_k), by_head(cache_dsa_v), cache_idx_k)

    sizes = (SEC,) * 9 + (IDX_DIM, IDX_HEADS, 2 * d_model)
    names = ("qa", "ka", "va", "ga", "qb", "kb", "vb", "gb", "qi", "ki", "wi", "mg")
    offs = [0]
    for s in sizes:
        offs.append(offs[-1] + s)

    hp = x_prompt.reshape(t_p, d_model)
    hs = x_sample.reshape(r_s, d_model)
    outs_p = [[] for _ in range(5)]
    outs_s = [[] for _ in range(5)]
    for l in range(depth):
        lam_init = 0.8 - 0.6 * math.exp(-0.3 * l)
        wl = w_in[l].astype(jnp.bfloat16)
        w_sec = {n: wl[:, offs[i]:offs[i + 1]] for i, n in enumerate(names)}
        w_sec["small"] = jnp.pad(jnp.concatenate([w_sec.pop("ki"), w_sec.pop("wi")], axis=1),
                                 ((0, 0), (0, LANES - IDX_DIM - IDX_HEADS)))
        lane_gain = lambda g: jnp.tile(g.astype(jnp.float32), LANES // g.shape[0]).reshape(1, LANES)
        lw = dict(ln_g=ln_g[l], w_sec=w_sec, qn_a=lane_gain(q_norm_a[l]), kn_a=lane_gain(k_norm_a[l]),
                  qn_b=lane_gain(q_norm_b[l]), kn_b=lane_gain(k_norm_b[l]), kn_i=lane_gain(k_norm_idx[l]))
        ok_a = _unshifted_ok(q_norm_a[l], k_norm_a[l], DH_A, DH_A ** -0.5)
        ok_b = _unshifted_ok(q_norm_b[l], k_norm_b[l], DH_B, DH_B ** -0.5)
        lam_p = jnp.stack([lam_q1[l], lam_k1[l], lam_q2[l], lam_k2[l]]).astype(jnp.float32)
        sg = subln_a[l].reshape(1, DV_A).astype(jnp.float32)
        w_o = (w_branch_a[l].astype(jnp.bfloat16), w_branch_b[l].astype(jnp.bfloat16),
               w_out[l].astype(jnp.bfloat16), ple_norm[l].reshape(1, d_model),
               w_ple_gate[l].astype(jnp.bfloat16), w_ple[l].astype(jnp.bfloat16))

        a = _layer_inputs(hp, pos_p, lw, tm_p)
        oa = _diff_attn(a["qa16"], a["ka16"], a["va16"], ok_a, lam_p, sg, lam_init, tq_a, tk_a)
        bias = _index_select(a["qi16"], a["wi"], a["ki2"], topk_p, tq_i, tk_i)
        ob = _dsa_attn(a["qb16"], a["kb16"], a["vb16"], bias, ok_b, tq_b, tk_b)
        hp = _layer_output(hp, p_prompt[l].reshape(t_p, -1), oa, a["ga"], ob, a["gb"], a["mg"], *w_o, tm_o)
        for lst, key in zip(outs_p, ("ka", "va", "kb", "vb", "ki")):
            lst.append(a[key])

        s = _layer_inputs(hs, pos_s, lw, tm_s)
        qa_s = _row_major(s["qa16"]).reshape(bsz, t_s, SEC)
        qb_s = _row_major(s["qb16"]).reshape(bsz, t_s, SEC)
        qi_s = _row_major(s["qi16"]).reshape(bsz, t_s, IDX_HEADS, IDX_DIM)
        qi_s = jnp.transpose(qi_s, (0, 2, 1, 3)).reshape(bsz, IDX_HEADS * t_s, IDX_DIM)
        wi_s = jnp.transpose(s["wi"].reshape(bsz, t_s, IDX_HEADS), (0, 2, 1)).reshape(bsz, IDX_HEADS * t_s, 1)
        news = tuple(_pad_new(s[k], bsz, t_s) for k in ("ka", "va", "kb", "vb", "ki"))
        oa_s, ob_s = _sample_attn(l, qa_s, qb_s, qi_s, wi_s, caches, news, lam_p, sg, lam_init, topk_s, tk_s)
        hs = _layer_output(hs, p_sample[l].reshape(r_s, -1), oa_s.reshape(r_s, SEC), s["ga"],
                           ob_s.reshape(r_s, SEC), s["gb"], s["mg"], *w_o, tm_os)
        for lst, key in zip(outs_s, ("ka", "va", "kb", "vb", "ki")):
            lst.append(s[key])

    def stack(lst, shape):
        return jnp.stack(lst).reshape((depth,) + shape)

    return (hp.reshape(bp, t_p, d_model), hs.reshape(bsz, t_s, d_model),
            stack(outs_p[0], (bp, t_p, H_A, 2, DH_A)), stack(outs_p[1], (bp, t_p, H_A, DV_A)),
            stack(outs_p[2], (bp, t_p, H_B, DH_B)), stack(outs_p[3], (bp, t_p, H_B, DH_B)),
            stack(outs_p[4], (bp, t_p, IDX_DIM)),
            stack(outs_s[0], (bsz, t_s, H_A, 2, DH_A)), stack(outs_s[1], (bsz, t_s, H_A, DV_A)),
            stack(outs_s[2], (bsz, t_s, H_B, DH_B)), stack(outs_s[3], (bsz, t_s, H_B, DH_B)),
            stack(outs_s[4], (bsz, t_s, IDX_DIM)))
```

```python
import functools
import math

import jax
import jax.numpy as jnp
import numpy as np
from jax import lax
from jax.experimental import pallas as pl
from jax.experimental.pallas import tpu as pltpu

CHUNK = 64
ROPE_THETA = 500000.0
EPS = 1e-6
H_A = 8
DH_A = 64
DV_A = 128
H_B = 8
DH_B = 128
IDX_HEADS = 16
IDX_DIM = 64
TOPK_MAX = 256
LANES = 128
GROUPS = 8
SEC = GROUPS * LANES
ROW_GROUP = 128
NEG = -1e30
LOG2E = 1.4426950408889634
MAX_UNSHIFTED = 60.0
INT_MIN = -(2 ** 31)
INT_MAX = 2 ** 31 - 1
VMEM_LIMIT = 56 * 1024 * 1024

_NT = (((1,), (1,)), ((), ()))


def _cdiv(a, b):
    return (a + b - 1) // b


def _params(sem, vmem=VMEM_LIMIT):
    return pltpu.CompilerParams(dimension_semantics=sem, vmem_limit_bytes=vmem)


def _rmsnorm_kernel(x_ref, g_ref, o_ref):
    x = x_ref[...]
    ms = jnp.mean(x * x, axis=-1, keepdims=True)
    o_ref[...] = (x * lax.rsqrt(ms + EPS) * g_ref[...]).astype(o_ref.dtype)


def _rmsnorm(x, g, tm):
    r, d = x.shape
    return pl.pallas_call(
        _rmsnorm_kernel,
        grid=(r // tm,),
        in_specs=[pl.BlockSpec((tm, d), lambda i: (i, 0)),
                  pl.BlockSpec((1, d), lambda i: (0, 0))],
        out_specs=pl.BlockSpec((tm, d), lambda i: (i, 0)),
        out_shape=jax.ShapeDtypeStruct((r, d), jnp.bfloat16),
        compiler_params=_params(("parallel",)),
        name="rmsnorm",
    )(x, g.reshape(1, d))


def _head_norm(z, g, d):
    sq = z * z
    if d == LANES:
        r = lax.rsqrt(jnp.sum(sq, axis=-1, keepdims=True) * (1.0 / d) + EPS)
    else:
        lo = lax.broadcasted_iota(jnp.int32, z.shape, 1) < d
        s_lo = jnp.sum(jnp.where(lo, sq, 0.0), axis=-1, keepdims=True)
        s_hi = jnp.sum(jnp.where(lo, 0.0, sq), axis=-1, keepdims=True)
        r = jnp.where(lo, lax.rsqrt(s_lo * (1.0 / d) + EPS), lax.rsqrt(s_hi * (1.0 / d) + EPS))
    return z * r * g


def _rope(z, c, a, b, half):
    return z * c + pltpu.roll(z, LANES - half, 1) * a + pltpu.roll(z, half, 1) * b


def _proj_kernel(xn_ref, w_ref, g_ref, rc_ref, ra_ref, rb_ref, *out_refs, norm, rope, scale, kinds):
    tm = xn_ref.shape[0]
    chunk = tm // 2 if (norm or rope) and tm % 512 == 0 else tm
    for r0 in range(0, tm, chunk):
        rows = slice(r0, r0 + chunk)
        z = jnp.dot(xn_ref[rows, :], w_ref[...], preferred_element_type=jnp.float32)
        for c in range(GROUPS):
            zc = z[:, c * LANES:(c + 1) * LANES]
            if norm:
                zc = _head_norm(zc, g_ref[...], norm)
            if rope:
                zc = _rope(zc, rc_ref[rows, :], ra_ref[rows, :], rb_ref[rows, :], rope // 8)
            for kind, o_ref in zip(kinds, out_refs):
                if kind == "f32":
                    o_ref[rows, c * LANES:(c + 1) * LANES] = zc
                else:
                    o_ref[c, rows, :] = (zc * scale).astype(o_ref.dtype)


def _proj(xn, w, gain, tables, *, norm, rope, scale, kinds, tm):
    r, d = xn.shape
    nj = w.shape[1] // SEC
    rc, ra, rb = tables
    out_shape, out_specs = [], []
    for kind in kinds:
        if kind == "f32":
            out_shape.append(jax.ShapeDtypeStruct((r, nj * SEC), jnp.float32))
            out_specs.append(pl.BlockSpec((tm, SEC), lambda j, i: (i, j)))
        else:
            assert nj == 1
            out_shape.append(jax.ShapeDtypeStruct((GROUPS, r, LANES), jnp.bfloat16))
            out_specs.append(pl.BlockSpec((GROUPS, tm, LANES), lambda j, i: (0, i, 0)))
    row = pl.BlockSpec((tm, LANES), lambda j, i: (i, 0))
    outs = pl.pallas_call(
        functools.partial(_proj_kernel, norm=norm, rope=rope, scale=scale, kinds=tuple(kinds)),
        grid=(nj, r // tm),
        in_specs=[pl.BlockSpec((tm, d), lambda j, i: (i, 0)),
                  pl.BlockSpec((d, SEC), lambda j, i: (0, j)),
                  pl.BlockSpec((1, LANES), lambda j, i: (0, 0)),
                  row, row, row],
        out_specs=out_specs,
        out_shape=out_shape,
        compiler_params=_params(("parallel", "parallel")),
        name="proj",
    )(xn, w, gain, rc, ra, rb)
    return outs


def _small_proj_kernel(xn_ref, w_ref, g_ref, rc_ref, ra_ref, rb_ref, f_ref, k2_ref, *, wscale):
    z = jnp.dot(xn_ref[...], w_ref[...], preferred_element_type=jnp.float32)
    lo = lax.broadcasted_iota(jnp.int32, z.shape, 1) < IDX_DIM
    ms = jnp.sum(jnp.where(lo, z * z, 0.0), axis=-1, keepdims=True) * (1.0 / IDX_DIM)
    zn = z * lax.rsqrt(ms + EPS) * g_ref[...]
    zr = _rope(zn, rc_ref[...], ra_ref[...], rb_ref[...], IDX_DIM // 8)
    f_ref[...] = jnp.where(lo, zr, z * wscale)
    k2_ref[...] = jnp.where(lo, zr, pltpu.roll(zr, IDX_DIM, 1)).astype(k2_ref.dtype)


def _small_proj(xn, w, gain, tables, tm):
    r, d = xn.shape
    rc, ra, rb = tables
    row = pl.BlockSpec((tm, LANES), lambda i: (i, 0))
    return pl.pallas_call(
        functools.partial(_small_proj_kernel, wscale=IDX_HEADS ** -0.5),
        grid=(r // tm,),
        in_specs=[pl.BlockSpec((tm, d), lambda i: (i, 0)),
                  pl.BlockSpec((d, LANES), lambda i: (0, 0)),
                  pl.BlockSpec((1, LANES), lambda i: (0, 0)),
                  row, row, row],
        out_specs=[row, row],
        out_shape=[jax.ShapeDtypeStruct((r, LANES), jnp.float32),
                   jax.ShapeDtypeStruct((r, LANES), jnp.bfloat16)],
        compiler_params=_params(("parallel",)),
        name="small_proj",
    )(xn, w, gain, rc, ra, rb)


def _rope_tables(pos, d):
    r = d // 4
    h = r // 2
    n = pos.shape[0]
    inv = ROPE_THETA ** (-jnp.arange(0, r, 2, dtype=jnp.float32) / r)
    ang = pos.astype(jnp.float32)[:, None] * inv[None, :]
    cos, sin = jnp.cos(ang), jnp.sin(ang)
    zh = jnp.zeros((n, h), jnp.float32)
    zr = jnp.zeros((n, d - r), jnp.float32)
    c = jnp.concatenate([cos, cos, jnp.ones((n, d - r), jnp.float32)], axis=-1)
    a = jnp.concatenate([-sin, zh, zr], axis=-1)
    b = jnp.concatenate([zh, sin, zr], axis=-1)
    rep = LANES // d
    return tuple(jnp.tile(t, (1, rep)) for t in (c, a, b))


def _online_update(s, v, m_ref, l_ref, acc_ref):
    m_prev = m_ref[...]
    m_new = jnp.maximum(m_prev, jnp.max(s, axis=-1, keepdims=True))
    alpha = jnp.exp2(m_prev - m_new)
    p = jnp.exp2(s - m_new)
    l_ref[...] = alpha * l_ref[...] + jnp.sum(p, axis=-1, keepdims=True)
    acc_ref[...] = alpha * acc_ref[...] + jnp.dot(p.astype(v.dtype), v, preferred_element_type=jnp.float32)
    m_ref[...] = m_new


def _causal_steps(t, tq, tk):
    qs, ks = [], []
    for qi in range(t // tq):
        nk = _cdiv((qi + 1) * tq, tk)
        qs += [qi] * nk
        ks += list(range(nk))
    return np.asarray(qs, np.int32), np.asarray(ks, np.int32)


def _unshifted_ok(gq, gk, d, scale):
    bound = jnp.max(jnp.abs(gq)) * jnp.max(jnp.abs(gk)) * (d * scale * (1.0 + 2.0 ** -6))
    return (bound <= MAX_UNSHIFTED).astype(jnp.int32).reshape(1)


def _lambda(lam_ref, lam_init):
    lp = lam_ref[...]
    s1 = jnp.sum(lp[0:1] * lp[1:2], axis=-1, keepdims=True)
    s2 = jnp.sum(lp[2:3] * lp[3:4], axis=-1, keepdims=True)
    return jnp.exp(s1) - jnp.exp(s2) + lam_init


def _diff_finish(o0, o1, lam, sg, lam_init):
    o = o0 - lam * o1
    ms = jnp.mean(o * o, axis=-1, keepdims=True)
    return o * lax.rsqrt(ms + EPS) * sg * (1.0 - lam_init)


def _chunk_end(pos):
    return (pos // CHUNK + 1) * CHUNK - 1


def _diff_attn_kernel(qt_ref, kt_ref, ok_ref, q_ref, k_ref, v_ref, lam_ref, sg_ref, o_ref,
                      qs_ref, v1_ref, acc_ref, m_ref, l_ref, *, tq, tk, lam_init):
    step_id = pl.program_id(1)
    qi = qt_ref[step_id]
    kj = kt_ref[step_id]
    nk = _cdiv((qi + 1) * tq, tk)
    unshifted = ok_ref[0] == 1
    full = (kj + 1) * tk <= qi * tq + CHUNK
    rows2 = 2 * tq

    @pl.when(kj == 0)
    def _():
        q = q_ref[...]
        lo = lax.broadcasted_iota(jnp.int32, q.shape, 1) < DH_A
        zero = jnp.zeros_like(q)
        qs_ref[0:tq, :] = jnp.where(lo, q, zero)
        qs_ref[tq:rows2, :] = jnp.where(lo, zero, q)
        v1_ref[:, DV_A:] = jnp.ones((tk, LANES), v1_ref.dtype)
        m_ref[...] = jnp.full(m_ref.shape, NEG, jnp.float32)
        l_ref[...] = jnp.zeros(l_ref.shape, jnp.float32)
        acc_ref[...] = jnp.zeros(acc_ref.shape, jnp.float32)

    v1_ref[:, :DV_A] = v_ref[...]

    def visible(r0, nrows):
        row = r0 + lax.broadcasted_iota(jnp.int32, (nrows, tk), 0)
        row = jnp.where(row >= tq, row - tq, row)
        kpos = kj * tk + lax.broadcasted_iota(jnp.int32, (nrows, tk), 1)
        return kpos <= _chunk_end(qi * tq + row)

    def fast_step(masked):
        k = k_ref[...]
        v1 = v1_ref[...]
        for r0 in range(0, rows2, ROW_GROUP):
            rows = slice(r0, r0 + ROW_GROUP)
            s = lax.dot_general(qs_ref[rows, :], k, _NT, preferred_element_type=jnp.float32)
            if masked:
                s = jnp.where(visible(r0, ROW_GROUP), s, NEG)
            p = jnp.exp2(s).astype(v1.dtype)
            acc_ref[rows, :] += jnp.dot(p, v1, preferred_element_type=jnp.float32)

    def exact_step(masked):
        s = lax.dot_general(qs_ref[...], k_ref[...], _NT, preferred_element_type=jnp.float32)
        if masked:
            s = jnp.where(visible(0, rows2), s, NEG)
        _online_update(s, v_ref[...], m_ref, l_ref, acc_ref.at[:, :DV_A])

    for fast, step in ((True, fast_step), (False, exact_step)):
        mode = unshifted if fast else jnp.logical_not(unshifted)

        @pl.when(jnp.logical_and(mode, full))
        def _():
            step(False)

        @pl.when(jnp.logical_and(mode, jnp.logical_not(full)))
        def _():
            step(True)

    @pl.when(kj == nk - 1)
    def _():
        @pl.when(jnp.logical_not(unshifted))
        def _():
            acc_ref[:, DV_A:] = jnp.broadcast_to(l_ref[...], (rows2, LANES))

        o = acc_ref[:, :DV_A] / acc_ref[:, DV_A:]
        o_ref[...] = _diff_finish(o[0:tq], o[tq:rows2], _lambda(lam_ref, lam_init), sg_ref[...], lam_init)


def _diff_attn(q, k, v, ok, lam_p, sg, lam_init, tq, tk):
    _, t, _ = q.shape
    qt, kt = _causal_steps(t, tq, tk)
    kv_spec = pl.BlockSpec((None, tk, LANES), lambda h, s, qt, kt, ok: (h, kt[s], 0))
    const = lambda h, s, qt, kt, ok: (0, 0)
    grid_spec = pltpu.PrefetchScalarGridSpec(
        num_scalar_prefetch=3,
        grid=(H_A, len(qt)),
        in_specs=[pl.BlockSpec((None, tq, LANES), lambda h, s, qt, kt, ok: (h, qt[s], 0)),
                  kv_spec, kv_spec,
                  pl.BlockSpec((4, DH_A), const),
                  pl.BlockSpec((1, DV_A), const)],
        out_specs=pl.BlockSpec((tq, LANES), lambda h, s, qt, kt, ok: (qt[s], h)),
        scratch_shapes=[pltpu.VMEM((2 * tq, LANES), jnp.bfloat16),
                        pltpu.VMEM((tk, DV_A + LANES), jnp.bfloat16),
                        pltpu.VMEM((2 * tq, DV_A + LANES), jnp.float32),
                        pltpu.VMEM((2 * tq, 1), jnp.float32),
                        pltpu.VMEM((2 * tq, 1), jnp.float32)])
    return pl.pallas_call(
        functools.partial(_diff_attn_kernel, tq=tq, tk=tk, lam_init=lam_init),
        grid_spec=grid_spec,
        out_shape=jax.ShapeDtypeStruct((t, H_A * DV_A), jnp.float32),
        compiler_params=_params(("parallel", "arbitrary")),
        name="diff_attn",
    )(jnp.asarray(qt), jnp.asarray(kt), ok, q, k, v, lam_p, sg)


def _sort_key(score):
    bits = pltpu.bitcast(score, jnp.int32)
    key = jnp.where(bits < 0, bits ^ INT_MAX, bits)
    return jnp.where(key == -1, 0, key)


def _index_kernel(qt_ref, kt_ref, q_ref, w_ref, k_ref, bias_ref, qs_ref, key_ref, *, tq, tk, t, topk):
    step_id = pl.program_id(0)
    qi = qt_ref[step_id]
    kj = kt_ref[step_id]
    nk = _cdiv((qi + 1) * tq, tk)
    nsub = tk // LANES

    @pl.when(kj == 0)
    def _():
        for p in range(GROUPS):
            q = q_ref[p]
            lo = lax.broadcasted_iota(jnp.int32, q.shape, 1) < IDX_DIM
            zero = jnp.zeros_like(q)
            qs_ref[(2 * p) * tq:(2 * p + 1) * tq, :] = jnp.where(lo, q, zero)
            qs_ref[(2 * p + 1) * tq:(2 * p + 2) * tq, :] = jnp.where(lo, zero, q)

    logits = lax.dot_general(qs_ref[...], k_ref[...], _NT, preferred_element_type=jnp.float32)
    w = w_ref[...]
    score = None
    for h in range(IDX_HEADS):
        term = jnp.maximum(logits[h * tq:(h + 1) * tq], 0.0) * w[:, h:h + 1]
        score = term if score is None else score + term
    qpos = qi * tq + lax.broadcasted_iota(jnp.int32, score.shape, 0)
    kpos = kj * tk + lax.broadcasted_iota(jnp.int32, score.shape, 1)
    key = jnp.where(kpos <= _chunk_end(qpos), _sort_key(score), INT_MIN)
    key_ref[:, pl.ds(pl.multiple_of(kj * tk, tk), tk)] = key

    @pl.when(kj == nk - 1)
    def _():
        lane = lax.broadcasted_iota(jnp.int32, (tq, LANES), 1)
        cend = _chunk_end(qi * tq + lax.broadcasted_iota(jnp.int32, (tq, LANES), 0))

        def rows(x):
            return jnp.broadcast_to(x, (tq, LANES))

        def sweep(fn, init):
            def body(j, acc):
                for c in range(nsub):
                    off = pl.multiple_of(j * tk + c * LANES, LANES)
                    acc = fn(acc, key_ref[:, pl.ds(off, LANES)], off + lane, c % 2)
                return acc
            acc = lax.fori_loop(0, nk // 2, lambda j, a: body(2 * j + 1, body(2 * j, a)), init)
            return lax.fori_loop(nk - nk % 2, nk, body, acc)

        def count(pred):
            acc = sweep(lambda a, kt, idx, par: a + jnp.where(pred(kt, idx), 1.0, 0.0),
                        jnp.zeros((tq, LANES), jnp.float32))
            return rows(jnp.sum(acc, axis=1, keepdims=True))

        def key_score(key):
            f = pltpu.bitcast(jnp.where(key < 0, key ^ INT_MAX, key), jnp.float32)
            return jnp.where(key == INT_MIN, -jnp.inf, f)

        kf = float(topk)
        imin = jnp.full((tq, LANES), INT_MIN, jnp.int32)

        def group_max(acc, kt, idx, par):
            return (jnp.maximum(acc[0], kt), acc[1]) if par == 0 else (acc[0], jnp.maximum(acc[1], kt))

        m0, m1 = sweep(group_max, (imin, imin))
        m0, m1 = key_score(m0), key_score(m1)
        lo = _sort_key(rows(jnp.min(jnp.minimum(m0, m1), axis=1, keepdims=True)))
        hi = _sort_key(rows(jnp.max(jnp.maximum(m0, m1), axis=1, keepdims=True))) + 1
        c_lo = count(lambda kt, idx: kt >= lo)
        c_hi = jnp.zeros((tq, LANES), jnp.float32)
        all_sel = cend + 1 <= topk
        lo = jnp.where(all_sel, INT_MIN + 1, lo)
        done = jnp.where(jnp.logical_or(all_sel, c_lo == kf), 1.0, 0.0)

        def search_cond(st):
            return jnp.logical_and(st[0] < 80, st[6] < 0.5)

        def search_body(st):
            it, lo, hi, c_lo, c_hi, done, _ = st
            mid = (lo >> 1) + (hi >> 1) + (lo & hi & 1)
            lo_f, hi_f = key_score(lo), key_score(hi)
            frac = (jnp.log(c_lo) - math.log(kf)) / (jnp.log(c_lo) - jnp.log(jnp.maximum(c_hi, 0.5)))
            t_f = lo_f + (hi_f - lo_f) * frac
            interp = jnp.logical_and((it & 1) == 0, jnp.abs(t_f) < jnp.inf)
            t = jnp.where(interp, _sort_key(t_f), mid)
            t = jnp.minimum(jnp.maximum(t, lo + 1), hi - 1)
            c = count(lambda kt, idx: kt >= t)
            up = jnp.logical_and(done < 0.5, c >= kf)
            dn = jnp.logical_and(done < 0.5, c < kf)
            lo, c_lo = jnp.where(up, t, lo), jnp.where(up, c, c_lo)
            hi, c_hi = jnp.where(dn, t, hi), jnp.where(dn, c, c_hi)
            done = jnp.where(jnp.logical_or(c_lo == kf, hi == lo + 1), 1.0, done)
            return it + 1, lo, hi, c_lo, c_hi, done, jnp.min(done)

        st = (jnp.int32(0), lo, hi, c_lo, c_hi, done, jnp.min(done))
        _, thr, hi, c_lo, c_hi, _, _ = lax.while_loop(search_cond, search_body, st)
        tied = jnp.logical_and(jnp.logical_not(all_sel), c_lo > kf)
        need = jnp.where(tied, kf - c_hi, jnp.inf)

        def write(select):
            def body(j, carry):
                for c in range(nsub):
                    off = pl.multiple_of(j * tk + c * LANES, LANES)
                    sel = select(key_ref[:, pl.ds(off, LANES)], off + lane)
                    bias_ref[:, pl.ds(off, LANES)] = jnp.where(sel, 0.0, NEG).astype(bias_ref.dtype)
                return carry
            lax.fori_loop(0, nk, body, 0)

        any_tied = jnp.max(jnp.where(tied, 1.0, 0.0)) > 0.5

        @pl.when(jnp.logical_not(any_tied))
        def _():
            write(lambda kt, idx: kt >= thr)

        @pl.when(any_tied)
        def _():
            nbits = max(1, (t - 1).bit_length())

            def pos_body(b, pos):
                cand = pos | lax.shift_left(jnp.int32(1), jnp.int32(nbits - 1) - b)
                g = count(lambda kt, idx: jnp.logical_and(kt == thr, idx < cand))
                return jnp.where(g < need, cand, pos)

            last = lax.fori_loop(0, nbits, pos_body, jnp.zeros((tq, LANES), jnp.int32))
            write(lambda kt, idx: jnp.logical_or(kt > thr, jnp.logical_and(kt == thr, idx <= last)))

        def fill_body(j, carry):
            off = pl.multiple_of(j * tk, tk)
            bias_ref[:, pl.ds(off, tk)] = jnp.full((tq, tk), NEG, bias_ref.dtype)
            return carry

        lax.fori_loop(nk, t // tk, fill_body, 0)


def _index_select(qi16, wi, ki2, topk, tq, tk):
    _, t, _ = qi16.shape
    qt, kt = _causal_steps(t, tq, tk)
    grid_spec = pltpu.PrefetchScalarGridSpec(
        num_scalar_prefetch=2,
        grid=(len(qt),),
        in_specs=[pl.BlockSpec((GROUPS, tq, LANES), lambda s, qt, kt: (0, qt[s], 0)),
                  pl.BlockSpec((tq, IDX_HEADS), lambda s, qt, kt: (qt[s], 0)),
                  pl.BlockSpec((tk, LANES), lambda s, qt, kt: (kt[s], 0))],
        out_specs=pl.BlockSpec((tq, t), lambda s, qt, kt: (qt[s], 0)),
        scratch_shapes=[pltpu.VMEM((IDX_HEADS * tq, LANES), jnp.bfloat16),
                        pltpu.VMEM((tq, t), jnp.int32)])
    return pl.pallas_call(
        functools.partial(_index_kernel, tq=tq, tk=tk, t=t, topk=topk),
        grid_spec=grid_spec,
        out_shape=jax.ShapeDtypeStruct((t, t), jnp.bfloat16),
        compiler_params=_params(("arbitrary",)),
        name="index_select",
    )(jnp.asarray(qt), jnp.asarray(kt), qi16, wi, ki2)


def _dsa_attn_kernel(qt_ref, kt_ref, ok_ref, q_ref, k_ref, v_ref, bias_ref, o_ref,
                     v1_ref, acc_ref, m_ref, l_ref, *, tq, tk):
    step_id = pl.program_id(0)
    qi = qt_ref[step_id]
    kj = kt_ref[step_id]
    nk = _cdiv((qi + 1) * tq, tk)
    unshifted = ok_ref[0] == 1

    @pl.when(kj == 0)
    def _():
        v1_ref[:, :, DH_B:] = jnp.ones((H_B, tk, LANES), v1_ref.dtype)
        m_ref[...] = jnp.full(m_ref.shape, NEG, jnp.float32)
        l_ref[...] = jnp.zeros(l_ref.shape, jnp.float32)
        acc_ref[...] = jnp.zeros(acc_ref.shape, jnp.float32)

    @pl.when(unshifted)
    def _():
        v1_ref[:, :, :DH_B] = v_ref[...]
        for r0 in range(0, tq, ROW_GROUP):
            rows = slice(r0, r0 + ROW_GROUP)
            bias = bias_ref[rows, :].astype(jnp.float32)
            for h in range(H_B):
                s = lax.dot_general(q_ref[h, rows, :], k_ref[h], _NT, preferred_element_type=jnp.float32) + bias
                p = jnp.exp2(s).astype(v1_ref.dtype)
                acc_ref[h, rows, :] += jnp.dot(p, v1_ref[h], preferred_element_type=jnp.float32)

    @pl.when(jnp.logical_not(unshifted))
    def _():
        bias = bias_ref[...].astype(jnp.float32)
        for h in range(H_B):
            s = lax.dot_general(q_ref[h], k_ref[h], _NT, preferred_element_type=jnp.float32) + bias
            _online_update(s, v_ref[h], m_ref.at[h], l_ref.at[h], acc_ref.at[h, :, :DH_B])

    @pl.when(kj == nk - 1)
    def _():
        @pl.when(jnp.logical_not(unshifted))
        def _():
            acc_ref[:, :, DH_B:] = jnp.broadcast_to(l_ref[...], (H_B, tq, LANES))

        for h in range(H_B):
            o_ref[:, h * LANES:(h + 1) * LANES] = acc_ref[h, :, :DH_B] / acc_ref[h, :, DH_B:]


def _dsa_attn(q, k, v, bias, ok, tq, tk):
    _, t, _ = q.shape
    qt, kt = _causal_steps(t, tq, tk)
    kv_spec = pl.BlockSpec((H_B, tk, LANES), lambda s, qt, kt, ok: (0, kt[s], 0))
    grid_spec = pltpu.PrefetchScalarGridSpec(
        num_scalar_prefetch=3,
        grid=(len(qt),),
        in_specs=[pl.BlockSpec((H_B, tq, LANES), lambda s, qt, kt, ok: (0, qt[s], 0)),
                  kv_spec, kv_spec,
                  pl.BlockSpec((tq, tk), lambda s, qt, kt, ok: (qt[s], kt[s]))],
        out_specs=pl.BlockSpec((tq, H_B * DH_B), lambda s, qt, kt, ok: (qt[s], 0)),
        scratch_shapes=[pltpu.VMEM((H_B, tk, DH_B + LANES), jnp.bfloat16),
                        pltpu.VMEM((H_B, tq, DH_B + LANES), jnp.float32),
                        pltpu.VMEM((H_B, tq, 1), jnp.float32),
                        pltpu.VMEM((H_B, tq, 1), jnp.float32)])
    return pl.pallas_call(
        functools.partial(_dsa_attn_kernel, tq=tq, tk=tk),
        grid_spec=grid_spec,
        out_shape=jax.ShapeDtypeStruct((t, H_B * DH_B), jnp.float32),
        compiler_params=_params(("arbitrary",)),
        name="dsa_attn",
    )(jnp.asarray(qt), jnp.asarray(kt), ok, q, k, v, bias)


def _sample_kernel(qa_ref, qb_ref, qi_ref, wi_ref, ckd_ref, cvd_ref, ckb_ref, cvb_ref, cki_ref,
                   nkd_ref, nvd_ref, nkb_ref, nvb_ref, nki_ref, lam_ref, sg_ref,
                   oa_ref, ob_ref,
                   qda_ref, qdb_ref, bias_ref, key_ref, ma_ref, la_ref, acca_ref, mb_ref, lb_ref, accb_ref,
                   *, ts, past, tk, topk, lam_init):
    kj = pl.program_id(1)
    nkt = past // tk
    width = past + LANES
    ra = 2 * H_A * ts
    rb = H_B * ts

    def new_key_valid(rows):
        t_q = lax.broadcasted_iota(jnp.int32, (rows, LANES), 0) % ts
        lane = lax.broadcasted_iota(jnp.int32, (rows, LANES), 1)
        return jnp.logical_and(lane < ts, past + lane <= _chunk_end(past + t_q))

    @pl.when(kj == 0)
    def _():
        qa = qa_ref[...]
        grp = lax.broadcasted_iota(jnp.int32, qa.shape, 1) // DH_A
        for hc in range(2 * H_A):
            qda_ref[hc * ts:(hc + 1) * ts, :] = jnp.where(grp == hc, qa, jnp.zeros_like(qa))
        qb = qb_ref[...]
        grp = lax.broadcasted_iota(jnp.int32, qb.shape, 1) // DH_B
        for h in range(H_B):
            qdb_ref[h * ts:(h + 1) * ts, :] = jnp.where(grp == h, qb, jnp.zeros_like(qb))
        for ref in (ma_ref, mb_ref):
            ref[...] = jnp.full(ref.shape, NEG, jnp.float32)
        for ref in (la_ref, lb_ref, acca_ref, accb_ref):
            ref[...] = jnp.zeros(ref.shape, jnp.float32)

        qi = qi_ref[...]
        w = wi_ref[...]

        def scores(keys16):
            logits = lax.dot_general(qi, keys16, _NT, preferred_element_type=jnp.float32)
            score = None
            for h in range(IDX_HEADS):
                term = jnp.maximum(logits[h * ts:(h + 1) * ts], 0.0) * w[h * ts:(h + 1) * ts]
                score = term if score is None else score + term
            return score

        step = min(past, 1024)
        for c in range(past // step):
            keys16 = cki_ref[c * step:(c + 1) * step, :].astype(jnp.bfloat16)
            key_ref[:, c * step:(c + 1) * step] = _sort_key(scores(keys16))
        key_ref[:, past:width] = jnp.where(new_key_valid(ts), _sort_key(scores(nki_ref[...])), INT_MIN)

        kf = float(topk)

        def count(pred):
            idx = lax.broadcasted_iota(jnp.int32, (ts, width), 1)
            c = jnp.sum(jnp.where(pred(key_ref[...], idx), 1.0, 0.0), axis=1, keepdims=True)
            return jnp.broadcast_to(c, (ts, LANES))

        def wide(x):
            return x[:, 0:1]

        c0 = count(lambda kt, idx: kt >= 0)
        prefix = jnp.where(c0 >= kf, 0, INT_MIN).astype(jnp.int32)

        def bit_body(b, prefix):
            cand = prefix | lax.shift_left(jnp.int32(1), jnp.int32(30) - b)
            c = count(lambda kt, idx: kt >= wide(cand))
            return jnp.where(c >= kf, cand, prefix)

        thr = lax.fori_loop(0, 31, bit_body, prefix)
        n_ge = count(lambda kt, idx: kt >= wide(thr))

        def write(select):
            kt = key_ref[...]
            sel = jnp.logical_and(select(kt, lax.broadcasted_iota(jnp.int32, (ts, width), 1)), kt != INT_MIN)
            bias = jnp.where(sel, 0.0, NEG)
            for h in range(H_B):
                bias_ref[h * ts:(h + 1) * ts, :] = bias

        any_tied = jnp.max(n_ge) > kf

        @pl.when(jnp.logical_not(any_tied))
        def _():
            write(lambda kt, idx: kt >= wide(thr))

        @pl.when(any_tied)
        def _():
            need = kf - count(lambda kt, idx: kt > wide(thr))
            nbits = max(1, (width - 1).bit_length())

            def pos_body(b, pos):
                cand = pos | lax.shift_left(jnp.int32(1), jnp.int32(nbits - 1) - b)
                g = count(lambda kt, idx: jnp.logical_and(kt == wide(thr), idx < wide(cand)))
                return jnp.where(g < need, cand, pos)

            last = lax.fori_loop(0, nbits, pos_body, jnp.zeros((ts, LANES), jnp.int32))
            write(lambda kt, idx: jnp.logical_or(kt > wide(thr),
                                                 jnp.logical_and(kt == wide(thr), idx <= wide(last))))

    def head_rows(ref):
        return jnp.concatenate([ref[pl.ds(h, tk, stride=GROUPS), :].astype(jnp.bfloat16) for h in range(GROUPS)],
                               axis=1)

    sa = lax.dot_general(qda_ref[...], ckd_ref[...].astype(jnp.bfloat16), _NT, preferred_element_type=jnp.float32)
    _online_update(sa, head_rows(cvd_ref), ma_ref, la_ref, acca_ref)
    sb = lax.dot_general(qdb_ref[...], head_rows(ckb_ref), _NT, preferred_element_type=jnp.float32)
    sb = sb + bias_ref[:, pl.ds(pl.multiple_of(kj * tk, tk), tk)]
    _online_update(sb, head_rows(cvb_ref), mb_ref, lb_ref, accb_ref)

    @pl.when(kj == nkt - 1)
    def _():
        sa = lax.dot_general(qda_ref[...], nkd_ref[...], _NT, preferred_element_type=jnp.float32)
        sa = jnp.where(new_key_valid(ra), sa, NEG)
        _online_update(sa, nvd_ref[...], ma_ref, la_ref, acca_ref)
        sb = lax.dot_general(qdb_ref[...], nkb_ref[...], _NT, preferred_element_type=jnp.float32)
        sb = sb + bias_ref[:, past:width]
        _online_update(sb, nvb_ref[...], mb_ref, lb_ref, accb_ref)

        lam = _lambda(lam_ref, lam_init)
        oa = acca_ref[...] / la_ref[...]
        ob = accb_ref[...] / lb_ref[...]
        for h in range(H_A):
            cols = slice(h * DV_A, (h + 1) * DV_A)
            o0 = oa[(2 * h) * ts:(2 * h + 1) * ts, cols]
            o1 = oa[(2 * h + 1) * ts:(2 * h + 2) * ts, cols]
            oa_ref[:, cols] = _diff_finish(o0, o1, lam, sg_ref[...], lam_init)
        for h in range(H_B):
            cols = slice(h * DH_B, (h + 1) * DH_B)
            ob_ref[:, cols] = ob[h * ts:(h + 1) * ts, cols]


def _sample_attn(l, qa, qb, qi, wi, caches, news, lam_p, sg, lam_init, topk, tk):
    bsz, ts, _ = qa.shape
    ckd, cvd, ckb, cvb, cki = caches
    past = cki.shape[2]
    width = past + LANES
    ra, rb = 2 * H_A * ts, H_B * ts
    cache_spec = pl.BlockSpec((None, None, tk, SEC), lambda b, kj: (l, b, kj, 0))
    heads_spec = pl.BlockSpec((None, None, tk * GROUPS, LANES), lambda b, kj: (l, b, kj, 0))
    new_spec = pl.BlockSpec((None, LANES, SEC), lambda b, kj: (b, 0, 0))
    row_spec = pl.BlockSpec((None, ts, SEC), lambda b, kj: (b, 0, 0))
    return pl.pallas_call(
        functools.partial(_sample_kernel, ts=ts, past=past, tk=tk, topk=topk, lam_init=lam_init),
        grid=(bsz, past // tk),
        in_specs=[row_spec, row_spec,
                  pl.BlockSpec((None, IDX_HEADS * ts, IDX_DIM), lambda b, kj: (b, 0, 0)),
                  pl.BlockSpec((None, IDX_HEADS * ts, 1), lambda b, kj: (b, 0, 0)),
                  cache_spec, heads_spec, heads_spec, heads_spec,
                  pl.BlockSpec((None, None, past, IDX_DIM), lambda b, kj: (l, b, 0, 0)),
                  new_spec, new_spec, new_spec, new_spec,
                  pl.BlockSpec((None, LANES, IDX_DIM), lambda b, kj: (b, 0, 0)),
                  pl.BlockSpec((4, DH_A), lambda b, kj: (0, 0)),
                  pl.BlockSpec((1, DV_A), lambda b, kj: (0, 0))],
        out_specs=[row_spec, row_spec],
        out_shape=[jax.ShapeDtypeStruct((bsz, ts, SEC), jnp.float32),
                   jax.ShapeDtypeStruct((bsz, ts, SEC), jnp.float32)],
        scratch_shapes=[pltpu.VMEM((ra, SEC), jnp.bfloat16),
                        pltpu.VMEM((rb, SEC), jnp.bfloat16),
                        pltpu.VMEM((rb, width), jnp.float32),
                        pltpu.VMEM((ts, width), jnp.int32),
                        pltpu.VMEM((ra, 1), jnp.float32),
                        pltpu.VMEM((ra, 1), jnp.float32),
                        pltpu.VMEM((ra, SEC), jnp.float32),
                        pltpu.VMEM((rb, 1), jnp.float32),
                        pltpu.VMEM((rb, 1), jnp.float32),
                        pltpu.VMEM((rb, SEC), jnp.float32)],
        compiler_params=_params(("parallel", "arbitrary")),
        name="sample_attn",
    )(qa, qb, qi, wi, ckd, cvd, ckb, cvb, cki, *news, lam_p, sg)


def _silu(x):
    return x * jax.nn.sigmoid(x)


def _merge_kernel(oa_ref, ga_ref, ob_ref, gb_ref, mg_ref, wa_ref, wb_ref, u_ref):
    d = u_ref.shape[-1]
    ya = jnp.dot((oa_ref[...] * _silu(ga_ref[...])).astype(jnp.bfloat16), wa_ref[...],
                 preferred_element_type=jnp.float32)
    yb = jnp.dot((ob_ref[...] * _silu(gb_ref[...])).astype(jnp.bfloat16), wb_ref[...],
                 preferred_element_type=jnp.float32)
    mg = mg_ref[...]
    u = jax.nn.sigmoid(mg[:, :d]) * ya + jax.nn.sigmoid(mg[:, d:]) * yb
    u_ref[...] = u.astype(u_ref.dtype)


def _resid_kernel(x_ref, u_ref, wo_ref, g_ref, h_ref, hn_ref):
    h = x_ref[...] + jnp.dot(u_ref[...], wo_ref[...], preferred_element_type=jnp.float32)
    h_ref[...] = h
    ms = jnp.mean(h * h, axis=-1, keepdims=True)
    hn_ref[...] = (h * lax.rsqrt(ms + EPS) * g_ref[...]).astype(hn_ref.dtype)


def _ple_kernel(h_ref, hn_ref, p_ref, wg_ref, wp_ref, o_ref):
    gate = jax.nn.sigmoid(jnp.dot(hn_ref[...], wg_ref[...], preferred_element_type=jnp.float32))
    e = jnp.dot(p_ref[...].astype(jnp.bfloat16), wp_ref[...], preferred_element_type=jnp.float32)
    o_ref[...] = h_ref[...] + gate * e


def _layer_output(x, p, oa, ga, ob, gb, mg, w_ba, w_bb, w_out, ple_g, w_pg, w_ple, tm):
    r, d = x.shape
    wa = oa.shape[1]
    dp = p.shape[1]
    rows = lambda n: pl.BlockSpec((tm, n), lambda i: (i, 0))
    full = lambda a, b: pl.BlockSpec((a, b), lambda i: (0, 0))
    u = pl.pallas_call(
        _merge_kernel,
        grid=(r // tm,),
        in_specs=[rows(wa), rows(wa), rows(wa), rows(wa), rows(2 * d), full(wa, d), full(wa, d)],
        out_specs=rows(d),
        out_shape=jax.ShapeDtypeStruct((r, d), jnp.bfloat16),
        compiler_params=_params(("parallel",)),
        name="merge",
    )(oa, ga, ob, gb, mg, w_ba, w_bb)
    h, hn = pl.pallas_call(
        _resid_kernel,
        grid=(r // tm,),
        in_specs=[rows(d), rows(d), full(d, d), full(1, d)],
        out_specs=[rows(d), rows(d)],
        out_shape=[jax.ShapeDtypeStruct((r, d), jnp.float32), jax.ShapeDtypeStruct((r, d), jnp.bfloat16)],
        compiler_params=_params(("parallel",)),
        name="resid",
    )(x, u, w_out, ple_g)
    return pl.pallas_call(
        _ple_kernel,
        grid=(r // tm,),
        in_specs=[rows(d), rows(d), rows(dp), full(d, d), full(dp, d)],
        out_specs=rows(d),
        out_shape=jax.ShapeDtypeStruct((r, d), jnp.float32),
        compiler_params=_params(("parallel",)),
        name="ple",
    )(h, hn, p, w_pg, w_ple)


def _tile(n, pref):
    t = min(n, pref)
    while n % t:
        t //= 2
    return t


def _layer_inputs(x, pos, lw, tm):
    xn = _rmsnorm(x, lw["ln_g"], tm)
    t64 = _rope_tables(pos, DH_A)
    t128 = _rope_tables(pos, DH_B)
    none = t64
    one = jnp.ones((1, LANES), jnp.float32)
    w = lw["w_sec"]
    pj = functools.partial(_proj, xn, tm=tm)
    (qa16,) = pj(w["qa"], lw["qn_a"], t64, norm=DH_A, rope=DH_A, scale=DH_A ** -0.5 * LOG2E, kinds=["hm16"])
    ka, ka16 = pj(w["ka"], lw["kn_a"], t64, norm=DH_A, rope=DH_A, scale=1.0, kinds=["f32", "hm16"])
    va, va16 = pj(w["va"], one, none, norm=0, rope=0, scale=1.0, kinds=["f32", "hm16"])
    (ga,) = pj(w["ga"], one, none, norm=0, rope=0, scale=1.0, kinds=["f32"])
    (qb16,) = pj(w["qb"], lw["qn_b"], t128, norm=DH_B, rope=DH_B, scale=DH_B ** -0.5 * LOG2E, kinds=["hm16"])
    kb, kb16 = pj(w["kb"], lw["kn_b"], t128, norm=DH_B, rope=DH_B, scale=1.0, kinds=["f32", "hm16"])
    vb, vb16 = pj(w["vb"], one, none, norm=0, rope=0, scale=1.0, kinds=["f32", "hm16"])
    (gb,) = pj(w["gb"], one, none, norm=0, rope=0, scale=1.0, kinds=["f32"])
    (qi16,) = pj(w["qi"], one, t64, norm=0, rope=DH_A, scale=IDX_DIM ** -0.5, kinds=["hm16"])
    (mg,) = pj(w["mg"], one, none, norm=0, rope=0, scale=1.0, kinds=["f32"])
    small, ki2 = _small_proj(xn, w["small"], lw["kn_i"], t64, tm)
    ki = small[:, :IDX_DIM]
    wi = small[:, IDX_DIM:IDX_DIM + IDX_HEADS]
    return dict(qa16=qa16, ka=ka, ka16=ka16, va=va, va16=va16, ga=ga, qb16=qb16, kb=kb, kb16=kb16,
                vb=vb, vb16=vb16, gb=gb, qi16=qi16, mg=mg, ki=ki, ki2=ki2, wi=wi)


def _row_major(hm):
    g, r, n = hm.shape
    return jnp.transpose(hm, (1, 0, 2)).reshape(r, g * n)


def _pad_new(a, bsz, ts):
    a = a.reshape(bsz, ts, a.shape[-1]).astype(jnp.bfloat16)
    return jnp.pad(a, ((0, 0), (0, LANES - ts), (0, 0)))


def kernel(x_prompt, x_sample, p_prompt, p_sample, cache_diff_k, cache_diff_v, cache_dsa_k, cache_dsa_v, cache_idx_k, ln_g, w_in, q_norm_a, k_norm_a, lam_q1, lam_k1, lam_q2, lam_k2, subln_a, q_norm_b, k_norm_b, k_norm_idx, w_branch_a, w_branch_b, w_out, ple_norm, w_ple_gate, w_ple):
    depth = w_in.shape[0]
    bp, t_p, d_model = x_prompt.shape
    bsz, t_s, _ = x_sample.shape
    past = cache_diff_k.shape[2]
    assert bp == 1 and t_s <= LANES and t_p % LANES == 0 and past % LANES == 0
    topk_p = min(TOPK_MAX, t_p // 4)
    topk_s = min(TOPK_MAX, (past + t_s) // 4)
    pos_p = jnp.arange(t_p)
    pos_s = jnp.tile(past + jnp.arange(t_s), bsz)
    r_s = bsz * t_s

    tm_p = _tile(t_p, 512)
    tm_s = _tile(r_s, 512)
    tq_a, tk_a = _tile(t_p, 1024), _tile(t_p, 1024)
    tq_i, tk_i = _tile(t_p, 128), _tile(t_p, 1024)
    tq_b, tk_b = _tile(t_p, 512), _tile(t_p, 1024)
    tk_s = _tile(past, 512)
    tm_o = _tile(t_p, 256)
    tm_os = _tile(r_s, 256)

    by_head = lambda c: c.reshape(depth, bsz, past * GROUPS, LANES)
    caches = (cache_diff_k.reshape(depth, bsz, past, SEC), by_head(cache_diff_v),
              by_head(cache_dsa_k), by_head(cache_dsa_v), cache_idx_k)

    sizes = (SEC,) * 9 + (IDX_DIM, IDX_HEADS, 2 * d_model)
    names = ("qa", "ka", "va", "ga", "qb", "kb", "vb", "gb", "qi", "ki", "wi", "mg")
    offs = [0]
    for s in sizes:
        offs.append(offs[-1] + s)

    hp = x_prompt.reshape(t_p, d_model)
    hs = x_sample.reshape(r_s, d_model)
    outs_p = [[] for _ in range(5)]
    outs_s = [[] for _ in range(5)]
    for l in range(depth):
        lam_init = 0.8 - 0.6 * math.exp(-0.3 * l)
        wl = w_in[l].astype(jnp.bfloat16)
        w_sec = {n: wl[:, offs[i]:offs[i + 1]] for i, n in enumerate(names)}
        w_sec["small"] = jnp.pad(jnp.concatenate([w_sec.pop("ki"), w_sec.pop("wi")], axis=1),
                                 ((0, 0), (0, LANES - IDX_DIM - IDX_HEADS)))
        lane_gain = lambda g: jnp.tile(g.astype(jnp.float32), LANES // g.shape[0]).reshape(1, LANES)
        lw = dict(ln_g=ln_g[l], w_sec=w_sec, qn_a=lane_gain(q_norm_a[l]), kn_a=lane_gain(k_norm_a[l]),
                  qn_b=lane_gain(q_norm_b[l]), kn_b=lane_gain(k_norm_b[l]), kn_i=lane_gain(k_norm_idx[l]))
        ok_a = _unshifted_ok(q_norm_a[l], k_norm_a[l], DH_A, DH_A ** -0.5)
        ok_b = _unshifted_ok(q_norm_b[l], k_norm_b[l], DH_B, DH_B ** -0.5)
        lam_p = jnp.stack([lam_q1[l], lam_k1[l], lam_q2[l], lam_k2[l]]).astype(jnp.float32)
        sg = subln_a[l].reshape(1, DV_A).astype(jnp.float32)
        w_o = (w_branch_a[l].astype(jnp.bfloat16), w_branch_b[l].astype(jnp.bfloat16),
               w_out[l].astype(jnp.bfloat16), ple_norm[l].reshape(1, d_model),
               w_ple_gate[l].astype(jnp.bfloat16), w_ple[l].astype(jnp.bfloat16))

        a = _layer_inputs(hp, pos_p, lw, tm_p)
        oa = _diff_attn(a["qa16"], a["ka16"], a["va16"], ok_a, lam_p, sg, lam_init, tq_a, tk_a)
        bias = _index_select(a["qi16"], a["wi"], a["ki2"], topk_p, tq_i, tk_i)
        ob = _dsa_attn(a["qb16"], a["kb16"], a["vb16"], bias, ok_b, tq_b, tk_b)
        hp = _layer_output(hp, p_prompt[l].reshape(t_p, -1), oa, a["ga"], ob, a["gb"], a["mg"], *w_o, tm_o)
        for lst, key in zip(outs_p, ("ka", "va", "kb", "vb", "ki")):
            lst.append(a[key])

        s = _layer_inputs(hs, pos_s, lw, tm_s)
        qa_s = _row_major(s["qa16"]).reshape(bsz, t_s, SEC)
        qb_s = _row_major(s["qb16"]).reshape(bsz, t_s, SEC)
        qi_s = _row_major(s["qi16"]).reshape(bsz, t_s, IDX_HEADS, IDX_DIM)
        qi_s = jnp.transpose(qi_s, (0, 2, 1, 3)).reshape(bsz, IDX_HEADS * t_s, IDX_DIM)
        wi_s = jnp.transpose(s["wi"].reshape(bsz, t_s, IDX_HEADS), (0, 2, 1)).reshape(bsz, IDX_HEADS * t_s, 1)
        news = tuple(_pad_new(s[k], bsz, t_s) for k in ("ka", "va", "kb", "vb", "ki"))
        oa_s, ob_s = _sample_attn(l, qa_s, qb_s, qi_s, wi_s, caches, news, lam_p, sg, lam_init, topk_s, tk_s)
        hs = _layer_output(hs, p_sample[l].reshape(r_s, -1), oa_s.reshape(r_s, SEC), s["ga"],
                           ob_s.reshape(r_s, SEC), s["gb"], s["mg"], *w_o, tm_os)
        for lst, key in zip(outs_s, ("ka", "va", "kb", "vb", "ki")):
            lst.append(s[key])

    def stack(lst, shape):
        return jnp.stack(lst).reshape((depth,) + shape)

    return (hp.reshape(bp, t_p, d_model), hs.reshape(bsz, t_s, d_model),
            stack(outs_p[0], (bp, t_p, H_A, 2, DH_A)), stack(outs_p[1], (bp, t_p, H_A, DV_A)),
            stack(outs_p[2], (bp, t_p, H_B, DH_B)), stack(outs_p[3], (bp, t_p, H_B, DH_B)),
            stack(outs_p[4], (bp, t_p, IDX_DIM)),
            stack(outs_s[0], (bsz, t_s, H_A, 2, DH_A)), stack(outs_s[1], (bsz, t_s, H_A, DV_A)),
            stack(outs_s[2], (bsz, t_s, H_B, DH_B)), stack(outs_s[3], (bsz, t_s, H_B, DH_B)),
            stack(outs_s[4], (bsz, t_s, IDX_DIM)))
```

```python
import functools
import math

import jax
import jax.numpy as jnp
import numpy as np
from jax import lax
from jax.experimental import pallas as pl
from jax.experimental.pallas import tpu as pltpu

CHUNK = 64
ROPE_THETA = 500000.0
EPS = 1e-6
H_A = 8
DH_A = 64
DV_A = 128
H_B = 8
DH_B = 128
IDX_HEADS = 16
IDX_DIM = 64
TOPK_MAX = 256
LANES = 128
GROUPS = 8
SEC = GROUPS * LANES
ROW_GROUP = 128
NEG = -1e30
LOG2E = 1.4426950408889634
MAX_UNSHIFTED = 60.0
INT_MIN = -(2 ** 31)
INT_MAX = 2 ** 31 - 1
VMEM_LIMIT = 56 * 1024 * 1024

_NT = (((1,), (1,)), ((), ()))


def _cdiv(a, b):
    return (a + b - 1) // b


def _params(sem, vmem=VMEM_LIMIT):
    return pltpu.CompilerParams(dimension_semantics=sem, vmem_limit_bytes=vmem)


def _rmsnorm_kernel(x_ref, g_ref, o_ref):
    x = x_ref[...]
    ms = jnp.mean(x * x, axis=-1, keepdims=True)
    o_ref[...] = (x * lax.rsqrt(ms + EPS) * g_ref[...]).astype(o_ref.dtype)


def _rmsnorm(x, g, tm):
    r, d = x.shape
    return pl.pallas_call(
        _rmsnorm_kernel,
        grid=(r // tm,),
        in_specs=[pl.BlockSpec((tm, d), lambda i: (i, 0)),
                  pl.BlockSpec((1, d), lambda i: (0, 0))],
        out_specs=pl.BlockSpec((tm, d), lambda i: (i, 0)),
        out_shape=jax.ShapeDtypeStruct((r, d), jnp.bfloat16),
        compiler_params=_params(("parallel",)),
        name="rmsnorm",
    )(x, g.reshape(1, d))


def _head_norm(z, g, d):
    sq = z * z
    if d == LANES:
        r = lax.rsqrt(jnp.sum(sq, axis=-1, keepdims=True) * (1.0 / d) + EPS)
    else:
        lo = lax.broadcasted_iota(jnp.int32, z.shape, 1) < d
        s_lo = jnp.sum(jnp.where(lo, sq, 0.0), axis=-1, keepdims=True)
        s_hi = jnp.sum(jnp.where(lo, 0.0, sq), axis=-1, keepdims=True)
        r = jnp.where(lo, lax.rsqrt(s_lo * (1.0 / d) + EPS), lax.rsqrt(s_hi * (1.0 / d) + EPS))
    return z * r * g


def _rope(z, c, a, b, half):
    return z * c + pltpu.roll(z, LANES - half, 1) * a + pltpu.roll(z, half, 1) * b


def _proj_kernel(xn_ref, w_ref, g_ref, rc_ref, ra_ref, rb_ref, *out_refs, norm, rope, scale, kinds):
    tm = xn_ref.shape[0]
    chunk = tm // 2 if (norm or rope) and tm % 512 == 0 else tm
    for r0 in range(0, tm, chunk):
        rows = slice(r0, r0 + chunk)
        z = jnp.dot(xn_ref[rows, :], w_ref[...], preferred_element_type=jnp.float32)
        for c in range(GROUPS):
            zc = z[:, c * LANES:(c + 1) * LANES]
            if norm:
                zc = _head_norm(zc, g_ref[...], norm)
            if rope:
                zc = _rope(zc, rc_ref[rows, :], ra_ref[rows, :], rb_ref[rows, :], rope // 8)
            for kind, o_ref in zip(kinds, out_refs):
                if kind == "f32":
                    o_ref[rows, c * LANES:(c + 1) * LANES] = zc
                else:
                    o_ref[c, rows, :] = (zc * scale).astype(o_ref.dtype)


def _proj(xn, w, gain, tables, *, norm, rope, scale, kinds, tm):
    r, d = xn.shape
    nj = w.shape[1] // SEC
    rc, ra, rb = tables
    out_shape, out_specs = [], []
    for kind in kinds:
        if kind == "f32":
            out_shape.append(jax.ShapeDtypeStruct((r, nj * SEC), jnp.float32))
            out_specs.append(pl.BlockSpec((tm, SEC), lambda j, i: (i, j)))
        else:
            assert nj == 1
            out_shape.append(jax.ShapeDtypeStruct((GROUPS, r, LANES), jnp.bfloat16))
            out_specs.append(pl.BlockSpec((GROUPS, tm, LANES), lambda j, i: (0, i, 0)))
    row = pl.BlockSpec((tm, LANES), lambda j, i: (i, 0))
    outs = pl.pallas_call(
        functools.partial(_proj_kernel, norm=norm, rope=rope, scale=scale, kinds=tuple(kinds)),
        grid=(nj, r // tm),
        in_specs=[pl.BlockSpec((tm, d), lambda j, i: (i, 0)),
                  pl.BlockSpec((d, SEC), lambda j, i: (0, j)),
                  pl.BlockSpec((1, LANES), lambda j, i: (0, 0)),
                  row, row, row],
        out_specs=out_specs,
        out_shape=out_shape,
        compiler_params=_params(("parallel", "parallel")),
        name="proj",
    )(xn, w, gain, rc, ra, rb)
    return outs


def _small_proj_kernel(xn_ref, w_ref, g_ref, rc_ref, ra_ref, rb_ref, f_ref, k2_ref, *, wscale):
    z = jnp.dot(xn_ref[...], w_ref[...], preferred_element_type=jnp.float32)
    lo = lax.broadcasted_iota(jnp.int32, z.shape, 1) < IDX_DIM
    ms = jnp.sum(jnp.where(lo, z * z, 0.0), axis=-1, keepdims=True) * (1.0 / IDX_DIM)
    zn = z * lax.rsqrt(ms + EPS) * g_ref[...]
    zr = _rope(zn, rc_ref[...], ra_ref[...], rb_ref[...], IDX_DIM // 8)
    f_ref[...] = jnp.where(lo, zr, z * wscale)
    k2_ref[...] = jnp.where(lo, zr, pltpu.roll(zr, IDX_DIM, 1)).astype(k2_ref.dtype)


def _small_proj(xn, w, gain, tables, tm):
    r, d = xn.shape
    rc, ra, rb = tables
    row = pl.BlockSpec((tm, LANES), lambda i: (i, 0))
    return pl.pallas_call(
        functools.partial(_small_proj_kernel, wscale=IDX_HEADS ** -0.5),
        grid=(r // tm,),
        in_specs=[pl.BlockSpec((tm, d), lambda i: (i, 0)),
                  pl.BlockSpec((d, LANES), lambda i: (0, 0)),
                  pl.BlockSpec((1, LANES), lambda i: (0, 0)),
                  row, row, row],
        out_specs=[row, row],
        out_shape=[jax.ShapeDtypeStruct((r, LANES), jnp.float32),
                   jax.ShapeDtypeStruct((r, LANES), jnp.bfloat16)],
        compiler_params=_params(("parallel",)),
        name="small_proj",
    )(xn, w, gain, rc, ra, rb)


def _rope_tables(pos, d):
    r = d // 4
    h = r // 2
    n = pos.shape[0]
    inv = ROPE_THETA ** (-jnp.arange(0, r, 2, dtype=jnp.float32) / r)
    ang = pos.astype(jnp.float32)[:, None] * inv[None, :]
    cos, sin = jnp.cos(ang), jnp.sin(ang)
    zh = jnp.zeros((n, h), jnp.float32)
    zr = jnp.zeros((n, d - r), jnp.float32)
    c = jnp.concatenate([cos, cos, jnp.ones((n, d - r), jnp.float32)], axis=-1)
    a = jnp.concatenate([-sin, zh, zr], axis=-1)
    b = jnp.concatenate([zh, sin, zr], axis=-1)
    rep = LANES // d
    return tuple(jnp.tile(t, (1, rep)) for t in (c, a, b))


def _online_update(s, v, m_ref, l_ref, acc_ref):
    m_prev = m_ref[...]
    m_new = jnp.maximum(m_prev, jnp.max(s, axis=-1, keepdims=True))
    alpha = jnp.exp2(m_prev - m_new)
    p = jnp.exp2(s - m_new)
    l_ref[...] = alpha * l_ref[...] + jnp.sum(p, axis=-1, keepdims=True)
    acc_ref[...] = alpha * acc_ref[...] + jnp.dot(p.astype(v.dtype), v, preferred_element_type=jnp.float32)
    m_ref[...] = m_new


def _causal_steps(t, tq, tk):
    qs, ks = [], []
    for qi in range(t // tq):
        nk = _cdiv((qi + 1) * tq, tk)
        qs += [qi] * nk
        ks += list(range(nk))
    return np.asarray(qs, np.int32), np.asarray(ks, np.int32)


def _unshifted_ok(gq, gk, d, scale):
    bound = jnp.max(jnp.abs(gq)) * jnp.max(jnp.abs(gk)) * (d * scale * (1.0 + 2.0 ** -6))
    return (bound <= MAX_UNSHIFTED).astype(jnp.int32).reshape(1)


def _lambda(lam_ref, lam_init):
    lp = lam_ref[...]
    s1 = jnp.sum(lp[0:1] * lp[1:2], axis=-1, keepdims=True)
    s2 = jnp.sum(lp[2:3] * lp[3:4], axis=-1, keepdims=True)
    return jnp.exp(s1) - jnp.exp(s2) + lam_init


def _diff_finish(o0, o1, lam, sg, lam_init):
    o = o0 - lam * o1
    ms = jnp.mean(o * o, axis=-1, keepdims=True)
    return o * lax.rsqrt(ms + EPS) * sg * (1.0 - lam_init)


def _chunk_end(pos):
    return (pos // CHUNK + 1) * CHUNK - 1


def _diff_attn_kernel(qt_ref, kt_ref, ok_ref, q_ref, k_ref, v_ref, lam_ref, sg_ref, o_ref,
                      qs_ref, v1_ref, acc_ref, m_ref, l_ref, *, tq, tk, lam_init):
    step_id = pl.program_id(1)
    qi = qt_ref[step_id]
    kj = kt_ref[step_id]
    nk = _cdiv((qi + 1) * tq, tk)
    unshifted = ok_ref[0] == 1
    full = (kj + 1) * tk <= qi * tq + CHUNK
    rows2 = 2 * tq

    @pl.when(kj == 0)
    def _():
        q = q_ref[...]
        lo = lax.broadcasted_iota(jnp.int32, q.shape, 1) < DH_A
        zero = jnp.zeros_like(q)
        qs_ref[0:tq, :] = jnp.where(lo, q, zero)
        qs_ref[tq:rows2, :] = jnp.where(lo, zero, q)
        v1_ref[:, DV_A:] = jnp.ones((tk, LANES), v1_ref.dtype)
        m_ref[...] = jnp.full(m_ref.shape, NEG, jnp.float32)
        l_ref[...] = jnp.zeros(l_ref.shape, jnp.float32)
        acc_ref[...] = jnp.zeros(acc_ref.shape, jnp.float32)

    v1_ref[:, :DV_A] = v_ref[...]

    def visible(r0, nrows):
        row = r0 + lax.broadcasted_iota(jnp.int32, (nrows, tk), 0)
        row = jnp.where(row >= tq, row - tq, row)
        kpos = kj * tk + lax.broadcasted_iota(jnp.int32, (nrows, tk), 1)
        return kpos <= _chunk_end(qi * tq + row)

    def fast_step(masked):
        k = k_ref[...]
        v1 = v1_ref[...]
        for r0 in range(0, rows2, ROW_GROUP):
            rows = slice(r0, r0 + ROW_GROUP)
            s = lax.dot_general(qs_ref[rows, :], k, _NT, preferred_element_type=jnp.float32)
            if masked:
                s = jnp.where(visible(r0, ROW_GROUP), s, NEG)
            p = jnp.exp2(s).astype(v1.dtype)
            acc_ref[rows, :] += jnp.dot(p, v1, preferred_element_type=jnp.float32)

    def exact_step(masked):
        s = lax.dot_general(qs_ref[...], k_ref[...], _NT, preferred_element_type=jnp.float32)
        if masked:
            s = jnp.where(visible(0, rows2), s, NEG)
        _online_update(s, v_ref[...], m_ref, l_ref, acc_ref.at[:, :DV_A])

    for fast, step in ((True, fast_step), (False, exact_step)):
        mode = unshifted if fast else jnp.logical_not(unshifted)

        @pl.when(jnp.logical_and(mode, full))
        def _():
            step(False)

        @pl.when(jnp.logical_and(mode, jnp.logical_not(full)))
        def _():
            step(True)

    @pl.when(kj == nk - 1)
    def _():
        @pl.when(jnp.logical_not(unshifted))
        def _():
            acc_ref[:, DV_A:] = jnp.broadcast_to(l_ref[...], (rows2, LANES))

        o = acc_ref[:, :DV_A] / acc_ref[:, DV_A:]
        o_ref[...] = _diff_finish(o[0:tq], o[tq:rows2], _lambda(lam_ref, lam_init), sg_ref[...], lam_init)


def _diff_attn(q, k, v, ok, lam_p, sg, lam_init, tq, tk):
    _, t, _ = q.shape
    qt, kt = _causal_steps(t, tq, tk)
    kv_spec = pl.BlockSpec((None, tk, LANES), lambda h, s, qt, kt, ok: (h, kt[s], 0))
    const = lambda h, s, qt, kt, ok: (0, 0)
    grid_spec = pltpu.PrefetchScalarGridSpec(
        num_scalar_prefetch=3,
        grid=(H_A, len(qt)),
        in_specs=[pl.BlockSpec((None, tq, LANES), lambda h, s, qt, kt, ok: (h, qt[s], 0)),
                  kv_spec, kv_spec,
                  pl.BlockSpec((4, DH_A), const),
                  pl.BlockSpec((1, DV_A), const)],
        out_specs=pl.BlockSpec((tq, LANES), lambda h, s, qt, kt, ok: (qt[s], h)),
        scratch_shapes=[pltpu.VMEM((2 * tq, LANES), jnp.bfloat16),
                        pltpu.VMEM((tk, DV_A + LANES), jnp.bfloat16),
                        pltpu.VMEM((2 * tq, DV_A + LANES), jnp.float32),
                        pltpu.VMEM((2 * tq, 1), jnp.float32),
                        pltpu.VMEM((2 * tq, 1), jnp.float32)])
    return pl.pallas_call(
        functools.partial(_diff_attn_kernel, tq=tq, tk=tk, lam_init=lam_init),
        grid_spec=grid_spec,
        out_shape=jax.ShapeDtypeStruct((t, H_A * DV_A), jnp.float32),
        compiler_params=_params(("parallel", "arbitrary")),
        name="diff_attn",
    )(jnp.asarray(qt), jnp.asarray(kt), ok, q, k, v, lam_p, sg)


def _sort_key(score):
    bits = pltpu.bitcast(score, jnp.int32)
    key = jnp.where(bits < 0, bits ^ INT_MAX, bits)
    return jnp.where(key == -1, 0, key)


def _index_kernel(qt_ref, kt_ref, q_ref, w_ref, k_ref, bias_ref, qs_ref, key_ref, *, tq, tk, t, topk):
    step_id = pl.program_id(0)
    qi = qt_ref[step_id]
    kj = kt_ref[step_id]
    nk = _cdiv((qi + 1) * tq, tk)
    nsub = tk // LANES

    @pl.when(kj == 0)
    def _():
        for p in range(GROUPS):
            q = q_ref[p]
            lo = lax.broadcasted_iota(jnp.int32, q.shape, 1) < IDX_DIM
            zero = jnp.zeros_like(q)
            qs_ref[(2 * p) * tq:(2 * p + 1) * tq, :] = jnp.where(lo, q, zero)
            qs_ref[(2 * p + 1) * tq:(2 * p + 2) * tq, :] = jnp.where(lo, zero, q)

    logits = lax.dot_general(qs_ref[...], k_ref[...], _NT, preferred_element_type=jnp.float32)
    w = w_ref[...]
    score = None
    for h in range(IDX_HEADS):
        term = jnp.maximum(logits[h * tq:(h + 1) * tq], 0.0) * w[:, h:h + 1]
        score = term if score is None else score + term
    qpos = qi * tq + lax.broadcasted_iota(jnp.int32, score.shape, 0)
    kpos = kj * tk + lax.broadcasted_iota(jnp.int32, score.shape, 1)
    key = jnp.where(kpos <= _chunk_end(qpos), _sort_key(score), INT_MIN)
    for c in range(nsub):
        key_ref[kj * nsub + c] = key[:, c * LANES:(c + 1) * LANES]

    @pl.when(kj == nk - 1)
    def _():
        lane = lax.broadcasted_iota(jnp.int32, (tq, LANES), 1)
        cend = _chunk_end(qi * tq + lax.broadcasted_iota(jnp.int32, (tq, LANES), 0))

        def rows(x):
            return jnp.broadcast_to(x, (tq, LANES))

        def sweep(fn, init):
            def body(j, acc):
                for c in range(nsub):
                    acc = fn(acc, key_ref[j * nsub + c], j * tk + c * LANES + lane, c % 2)
                return acc
            acc = lax.fori_loop(0, nk // 2, lambda j, a: body(2 * j + 1, body(2 * j, a)), init)
            return lax.fori_loop(nk - nk % 2, nk, body, acc)

        def count(pred):
            acc = sweep(lambda a, kt, idx, par: jnp.where(pred(kt, idx), a + 1.0, a),
                        jnp.zeros((tq, LANES), jnp.float32))
            return rows(jnp.sum(acc, axis=1, keepdims=True))

        def key_score(key):
            f = pltpu.bitcast(jnp.where(key < 0, key ^ INT_MAX, key), jnp.float32)
            return jnp.where(key == INT_MIN, -jnp.inf, f)

        kf = float(topk)
        imin = jnp.full((tq, LANES), INT_MIN, jnp.int32)

        def group_max(acc, kt, idx, par):
            return (jnp.maximum(acc[0], kt), acc[1]) if par == 0 else (acc[0], jnp.maximum(acc[1], kt))

        m0, m1 = sweep(group_max, (imin, imin))
        m0, m1 = key_score(m0), key_score(m1)
        lo = _sort_key(rows(jnp.min(jnp.minimum(m0, m1), axis=1, keepdims=True)))
        hi = _sort_key(rows(jnp.max(jnp.maximum(m0, m1), axis=1, keepdims=True))) + 1
        c_lo = count(lambda kt, idx: kt >= lo)
        c_hi = jnp.zeros((tq, LANES), jnp.float32)
        all_sel = cend + 1 <= topk
        lo = jnp.where(all_sel, INT_MIN + 1, lo)
        done = jnp.where(jnp.logical_or(all_sel, c_lo == kf), 1.0, 0.0)

        def search_cond(st):
            return jnp.logical_and(st[0] < 80, st[6] < 0.5)

        def search_body(st):
            it, lo, hi, c_lo, c_hi, done, _ = st
            mid = (lo >> 1) + (hi >> 1) + (lo & hi & 1)
            lo_f, hi_f = key_score(lo), key_score(hi)
            frac = (jnp.log(c_lo) - math.log(kf)) / (jnp.log(c_lo) - jnp.log(jnp.maximum(c_hi, 0.5)))
            t_f = lo_f + (hi_f - lo_f) * frac
            interp = jnp.logical_and((it & 1) == 0, jnp.abs(t_f) < jnp.inf)
            t = jnp.where(interp, _sort_key(t_f), mid)
            t = jnp.minimum(jnp.maximum(t, lo + 1), hi - 1)
            c = count(lambda kt, idx: kt >= t)
            up = jnp.logical_and(done < 0.5, c >= kf)
            dn = jnp.logical_and(done < 0.5, c < kf)
            lo, c_lo = jnp.where(up, t, lo), jnp.where(up, c, c_lo)
            hi, c_hi = jnp.where(dn, t, hi), jnp.where(dn, c, c_hi)
            done = jnp.where(jnp.logical_or(c_lo == kf, hi == lo + 1), 1.0, done)
            return it + 1, lo, hi, c_lo, c_hi, done, jnp.min(done)

        st = (jnp.int32(0), lo, hi, c_lo, c_hi, done, jnp.min(done))
        _, thr, hi, c_lo, c_hi, _, _ = lax.while_loop(search_cond, search_body, st)
        tied = jnp.logical_and(jnp.logical_not(all_sel), c_lo > kf)
        need = jnp.where(tied, kf - c_hi, jnp.inf)

        def write(select):
            def body(j, carry):
                for c in range(nsub):
                    off = pl.multiple_of(j * tk + c * LANES, LANES)
                    sel = select(key_ref[j * nsub + c], off + lane)
                    bias_ref[:, pl.ds(off, LANES)] = jnp.where(sel, 0.0, NEG).astype(bias_ref.dtype)
                return carry
            lax.fori_loop(0, nk, body, 0)

        any_tied = jnp.max(jnp.where(tied, 1.0, 0.0)) > 0.5

        @pl.when(jnp.logical_not(any_tied))
        def _():
            write(lambda kt, idx: kt >= thr)

        @pl.when(any_tied)
        def _():
            nbits = max(1, (t - 1).bit_length())

            def pos_body(b, pos):
                cand = pos | lax.shift_left(jnp.int32(1), jnp.int32(nbits - 1) - b)
                g = count(lambda kt, idx: jnp.logical_and(kt == thr, idx < cand))
                return jnp.where(g < need, cand, pos)

            last = lax.fori_loop(0, nbits, pos_body, jnp.zeros((tq, LANES), jnp.int32))
            write(lambda kt, idx: jnp.logical_or(kt > thr, jnp.logical_and(kt == thr, idx <= last)))

        def fill_body(j, carry):
            off = pl.multiple_of(j * tk, tk)
            bias_ref[:, pl.ds(off, tk)] = jnp.full((tq, tk), NEG, bias_ref.dtype)
            return carry

        lax.fori_loop(nk, t // tk, fill_body, 0)


def _index_select(qi16, wi, ki2, topk, tq, tk):
    _, t, _ = qi16.shape
    qt, kt = _causal_steps(t, tq, tk)
    grid_spec = pltpu.PrefetchScalarGridSpec(
        num_scalar_prefetch=2,
        grid=(len(qt),),
        in_specs=[pl.BlockSpec((GROUPS, tq, LANES), lambda s, qt, kt: (0, qt[s], 0)),
                  pl.BlockSpec((tq, IDX_HEADS), lambda s, qt, kt: (qt[s], 0)),
                  pl.BlockSpec((tk, LANES), lambda s, qt, kt: (kt[s], 0))],
        out_specs=pl.BlockSpec((tq, t), lambda s, qt, kt: (qt[s], 0)),
        scratch_shapes=[pltpu.VMEM((IDX_HEADS * tq, LANES), jnp.bfloat16),
                        pltpu.VMEM((t // LANES, tq, LANES), jnp.int32)])
    return pl.pallas_call(
        functools.partial(_index_kernel, tq=tq, tk=tk, t=t, topk=topk),
        grid_spec=grid_spec,
        out_shape=jax.ShapeDtypeStruct((t, t), jnp.bfloat16),
        compiler_params=_params(("arbitrary",)),
        name="index_select",
    )(jnp.asarray(qt), jnp.asarray(kt), qi16, wi, ki2)


def _dsa_attn_kernel(qt_ref, kt_ref, ok_ref, q_ref, k_ref, v_ref, bias_ref, o_ref,
                     v1_ref, acc_ref, m_ref, l_ref, *, tq, tk):
    step_id = pl.program_id(0)
    qi = qt_ref[step_id]
    kj = kt_ref[step_id]
    nk = _cdiv((qi + 1) * tq, tk)
    unshifted = ok_ref[0] == 1

    @pl.when(kj == 0)
    def _():
        v1_ref[:, :, DH_B:] = jnp.ones((H_B, tk, LANES), v1_ref.dtype)
        m_ref[...] = jnp.full(m_ref.shape, NEG, jnp.float32)
        l_ref[...] = jnp.zeros(l_ref.shape, jnp.float32)
        acc_ref[...] = jnp.zeros(acc_ref.shape, jnp.float32)

    @pl.when(unshifted)
    def _():
        v1_ref[:, :, :DH_B] = v_ref[...]
        for r0 in range(0, tq, ROW_GROUP):
            rows = slice(r0, r0 + ROW_GROUP)
            bias = bias_ref[rows, :].astype(jnp.float32)
            for h in range(H_B):
                s = lax.dot_general(q_ref[h, rows, :], k_ref[h], _NT, preferred_element_type=jnp.float32) + bias
                p = jnp.exp2(s).astype(v1_ref.dtype)
                acc_ref[h, rows, :] += jnp.dot(p, v1_ref[h], preferred_element_type=jnp.float32)

    @pl.when(jnp.logical_not(unshifted))
    def _():
        bias = bias_ref[...].astype(jnp.float32)
        for h in range(H_B):
            s = lax.dot_general(q_ref[h], k_ref[h], _NT, preferred_element_type=jnp.float32) + bias
            _online_update(s, v_ref[h], m_ref.at[h], l_ref.at[h], acc_ref.at[h, :, :DH_B])

    @pl.when(kj == nk - 1)
    def _():
        @pl.when(jnp.logical_not(unshifted))
        def _():
            acc_ref[:, :, DH_B:] = jnp.broadcast_to(l_ref[...], (H_B, tq, LANES))

        for h in range(H_B):
            o_ref[:, h * LANES:(h + 1) * LANES] = acc_ref[h, :, :DH_B] / acc_ref[h, :, DH_B:]


def _dsa_attn(q, k, v, bias, ok, tq, tk):
    _, t, _ = q.shape
    qt, kt = _causal_steps(t, tq, tk)
    kv_spec = pl.BlockSpec((H_B, tk, LANES), lambda s, qt, kt, ok: (0, kt[s], 0))
    grid_spec = pltpu.PrefetchScalarGridSpec(
        num_scalar_prefetch=3,
        grid=(len(qt),),
        in_specs=[pl.BlockSpec((H_B, tq, LANES), lambda s, qt, kt, ok: (0, qt[s], 0)),
                  kv_spec, kv_spec,
                  pl.BlockSpec((tq, tk), lambda s, qt, kt, ok: (qt[s], kt[s]))],
        out_specs=pl.BlockSpec((tq, H_B * DH_B), lambda s, qt, kt, ok: (qt[s], 0)),
        scratch_shapes=[pltpu.VMEM((H_B, tk, DH_B + LANES), jnp.bfloat16),
                        pltpu.VMEM((H_B, tq, DH_B + LANES), jnp.float32),
                        pltpu.VMEM((H_B, tq, 1), jnp.float32),
                        pltpu.VMEM((H_B, tq, 1), jnp.float32)])
    return pl.pallas_call(
        functools.partial(_dsa_attn_kernel, tq=tq, tk=tk),
        grid_spec=grid_spec,
        out_shape=jax.ShapeDtypeStruct((t, H_B * DH_B), jnp.float32),
        compiler_params=_params(("arbitrary",)),
        name="dsa_attn",
    )(jnp.asarray(qt), jnp.asarray(kt), ok, q, k, v, bias)


def _sample_kernel(qa_ref, qb_ref, qi_ref, wi_ref, ckd_ref, cvd_ref, ckb_ref, cvb_ref, cki_ref,
                   nkd_ref, nvd_ref, nkb_ref, nvb_ref, nki_ref, lam_ref, sg_ref,
                   oa_ref, ob_ref,
                   qda_ref, qdb_ref, bias_ref, key_ref, ma_ref, la_ref, acca_ref, mb_ref, lb_ref, accb_ref,
                   *, ts, past, tk, topk, lam_init):
    kj = pl.program_id(1)
    nkt = past // tk
    width = past + LANES
    ra = 2 * H_A * ts
    rb = H_B * ts

    def new_key_valid(rows):
        t_q = lax.broadcasted_iota(jnp.int32, (rows, LANES), 0) % ts
        lane = lax.broadcasted_iota(jnp.int32, (rows, LANES), 1)
        return jnp.logical_and(lane < ts, past + lane <= _chunk_end(past + t_q))

    @pl.when(kj == 0)
    def _():
        qa = qa_ref[...]
        grp = lax.broadcasted_iota(jnp.int32, qa.shape, 1) // DH_A
        for hc in range(2 * H_A):
            qda_ref[hc * ts:(hc + 1) * ts, :] = jnp.where(grp == hc, qa, jnp.zeros_like(qa))
        qb = qb_ref[...]
        grp = lax.broadcasted_iota(jnp.int32, qb.shape, 1) // DH_B
        for h in range(H_B):
            qdb_ref[h * ts:(h + 1) * ts, :] = jnp.where(grp == h, qb, jnp.zeros_like(qb))
        for ref in (ma_ref, mb_ref):
            ref[...] = jnp.full(ref.shape, NEG, jnp.float32)
        for ref in (la_ref, lb_ref, acca_ref, accb_ref):
            ref[...] = jnp.zeros(ref.shape, jnp.float32)

        qi = qi_ref[...]
        w = wi_ref[...]

        def scores(keys16):
            logits = lax.dot_general(qi, keys16, _NT, preferred_element_type=jnp.float32)
            score = None
            for h in range(IDX_HEADS):
                term = jnp.maximum(logits[h * ts:(h + 1) * ts], 0.0) * w[h * ts:(h + 1) * ts]
                score = term if score is None else score + term
            return score

        step = min(past, 1024)
        for c in range(past // step):
            keys16 = cki_ref[c * step:(c + 1) * step, :].astype(jnp.bfloat16)
            key_ref[:, c * step:(c + 1) * step] = _sort_key(scores(keys16))
        key_ref[:, past:width] = jnp.where(new_key_valid(ts), _sort_key(scores(nki_ref[...])), INT_MIN)

        kf = float(topk)

        def count(pred):
            idx = lax.broadcasted_iota(jnp.int32, (ts, width), 1)
            c = jnp.sum(jnp.where(pred(key_ref[...], idx), 1.0, 0.0), axis=1, keepdims=True)
            return jnp.broadcast_to(c, (ts, LANES))

        def wide(x):
            return x[:, 0:1]

        c0 = count(lambda kt, idx: kt >= 0)
        prefix = jnp.where(c0 >= kf, 0, INT_MIN).astype(jnp.int32)

        def bit_body(b, prefix):
            cand = prefix | lax.shift_left(jnp.int32(1), jnp.int32(30) - b)
            c = count(lambda kt, idx: kt >= wide(cand))
            return jnp.where(c >= kf, cand, prefix)

        thr = lax.fori_loop(0, 31, bit_body, prefix)
        n_ge = count(lambda kt, idx: kt >= wide(thr))

        def write(select):
            kt = key_ref[...]
            sel = jnp.logical_and(select(kt, lax.broadcasted_iota(jnp.int32, (ts, width), 1)), kt != INT_MIN)
            bias = jnp.where(sel, 0.0, NEG)
            for h in range(H_B):
                bias_ref[h * ts:(h + 1) * ts, :] = bias

        any_tied = jnp.max(n_ge) > kf

        @pl.when(jnp.logical_not(any_tied))
        def _():
            write(lambda kt, idx: kt >= wide(thr))

        @pl.when(any_tied)
        def _():
            need = kf - count(lambda kt, idx: kt > wide(thr))
            nbits = max(1, (width - 1).bit_length())

            def pos_body(b, pos):
                cand = pos | lax.shift_left(jnp.int32(1), jnp.int32(nbits - 1) - b)
                g = count(lambda kt, idx: jnp.logical_and(kt == wide(thr), idx < wide(cand)))
                return jnp.where(g < need, cand, pos)

            last = lax.fori_loop(0, nbits, pos_body, jnp.zeros((ts, LANES), jnp.int32))
            write(lambda kt, idx: jnp.logical_or(kt > wide(thr),
                                                 jnp.logical_and(kt == wide(thr), idx <= wide(last))))

    def head_rows(ref):
        return jnp.concatenate([ref[pl.ds(h, tk, stride=GROUPS), :].astype(jnp.bfloat16) for h in range(GROUPS)],
                               axis=1)

    sa = lax.dot_general(qda_ref[...], ckd_ref[...].astype(jnp.bfloat16), _NT, preferred_element_type=jnp.float32)
    _online_update(sa, head_rows(cvd_ref), ma_ref, la_ref, acca_ref)
    sb = lax.dot_general(qdb_ref[...], head_rows(ckb_ref), _NT, preferred_element_type=jnp.float32)
    sb = sb + bias_ref[:, pl.ds(pl.multiple_of(kj * tk, tk), tk)]
    _online_update(sb, head_rows(cvb_ref), mb_ref, lb_ref, accb_ref)

    @pl.when(kj == nkt - 1)
    def _():
        sa = lax.dot_general(qda_ref[...], nkd_ref[...], _NT, preferred_element_type=jnp.float32)
        sa = jnp.where(new_key_valid(ra), sa, NEG)
        _online_update(sa, nvd_ref[...], ma_ref, la_ref, acca_ref)
        sb = lax.dot_general(qdb_ref[...], nkb_ref[...], _NT, preferred_element_type=jnp.float32)
        sb = sb + bias_ref[:, past:width]
        _online_update(sb, nvb_ref[...], mb_ref, lb_ref, accb_ref)

        lam = _lambda(lam_ref, lam_init)
        oa = acca_ref[...] / la_ref[...]
        ob = accb_ref[...] / lb_ref[...]
        for h in range(H_A):
            cols = slice(h * DV_A, (h + 1) * DV_A)
            o0 = oa[(2 * h) * ts:(2 * h + 1) * ts, cols]
            o1 = oa[(2 * h + 1) * ts:(2 * h + 2) * ts, cols]
            oa_ref[:, cols] = _diff_finish(o0, o1, lam, sg_ref[...], lam_init)
        for h in range(H_B):
            cols = slice(h * DH_B, (h + 1) * DH_B)
            ob_ref[:, cols] = ob[h * ts:(h + 1) * ts, cols]


def _sample_attn(l, qa, qb, qi, wi, caches, news, lam_p, sg, lam_init, topk, tk):
    bsz, ts, _ = qa.shape
    ckd, cvd, ckb, cvb, cki = caches
    past = cki.shape[2]
    width = past + LANES
    ra, rb = 2 * H_A * ts, H_B * ts
    cache_spec = pl.BlockSpec((None, None, tk, SEC), lambda b, kj: (l, b, kj, 0))
    heads_spec = pl.BlockSpec((None, None, tk * GROUPS, LANES), lambda b, kj: (l, b, kj, 0))
    new_spec = pl.BlockSpec((None, LANES, SEC), lambda b, kj: (b, 0, 0))
    row_spec = pl.BlockSpec((None, ts, SEC), lambda b, kj: (b, 0, 0))
    return pl.pallas_call(
        functools.partial(_sample_kernel, ts=ts, past=past, tk=tk, topk=topk, lam_init=lam_init),
        grid=(bsz, past // tk),
        in_specs=[row_spec, row_spec,
                  pl.BlockSpec((None, IDX_HEADS * ts, IDX_DIM), lambda b, kj: (b, 0, 0)),
                  pl.BlockSpec((None, IDX_HEADS * ts, 1), lambda b, kj: (b, 0, 0)),
                  cache_spec, heads_spec, heads_spec, heads_spec,
                  pl.BlockSpec((None, None, past, IDX_DIM), lambda b, kj: (l, b, 0, 0)),
                  new_spec, new_spec, new_spec, new_spec,
                  pl.BlockSpec((None, LANES, IDX_DIM), lambda b, kj: (b, 0, 0)),
                  pl.BlockSpec((4, DH_A), lambda b, kj: (0, 0)),
                  pl.BlockSpec((1, DV_A), lambda b, kj: (0, 0))],
        out_specs=[row_spec, row_spec],
        out_shape=[jax.ShapeDtypeStruct((bsz, ts, SEC), jnp.float32),
                   jax.ShapeDtypeStruct((bsz, ts, SEC), jnp.float32)],
        scratch_shapes=[pltpu.VMEM((ra, SEC), jnp.bfloat16),
                        pltpu.VMEM((rb, SEC), jnp.bfloat16),
                        pltpu.VMEM((rb, width), jnp.float32),
                        pltpu.VMEM((ts, width), jnp.int32),
                        pltpu.VMEM((ra, 1), jnp.float32),
                        pltpu.VMEM((ra, 1), jnp.float32),
                        pltpu.VMEM((ra, SEC), jnp.float32),
                        pltpu.VMEM((rb, 1), jnp.float32),
                        pltpu.VMEM((rb, 1), jnp.float32),
                        pltpu.VMEM((rb, SEC), jnp.float32)],
        compiler_params=_params(("parallel", "arbitrary")),
        name="sample_attn",
    )(qa, qb, qi, wi, ckd, cvd, ckb, cvb, cki, *news, lam_p, sg)


def _silu(x):
    return x * jax.nn.sigmoid(x)


def _merge_kernel(oa_ref, ga_ref, ob_ref, gb_ref, mg_ref, wa_ref, wb_ref, u_ref):
    d = u_ref.shape[-1]
    ya = jnp.dot((oa_ref[...] * _silu(ga_ref[...])).astype(jnp.bfloat16), wa_ref[...],
                 preferred_element_type=jnp.float32)
    yb = jnp.dot((ob_ref[...] * _silu(gb_ref[...])).astype(jnp.bfloat16), wb_ref[...],
                 preferred_element_type=jnp.float32)
    mg = mg_ref[...]
    u = jax.nn.sigmoid(mg[:, :d]) * ya + jax.nn.sigmoid(mg[:, d:]) * yb
    u_ref[...] = u.astype(u_ref.dtype)


def _resid_kernel(x_ref, u_ref, wo_ref, g_ref, h_ref, hn_ref):
    h = x_ref[...] + jnp.dot(u_ref[...], wo_ref[...], preferred_element_type=jnp.float32)
    h_ref[...] = h
    ms = jnp.mean(h * h, axis=-1, keepdims=True)
    hn_ref[...] = (h * lax.rsqrt(ms + EPS) * g_ref[...]).astype(hn_ref.dtype)


def _ple_kernel(h_ref, hn_ref, p_ref, wg_ref, wp_ref, o_ref):
    gate = jax.nn.sigmoid(jnp.dot(hn_ref[...], wg_ref[...], preferred_element_type=jnp.float32))
    e = jnp.dot(p_ref[...].astype(jnp.bfloat16), wp_ref[...], preferred_element_type=jnp.float32)
    o_ref[...] = h_ref[...] + gate * e


def _layer_output(x, p, oa, ga, ob, gb, mg, w_ba, w_bb, w_out, ple_g, w_pg, w_ple, tm):
    r, d = x.shape
    wa = oa.shape[1]
    dp = p.shape[1]
    rows = lambda n: pl.BlockSpec((tm, n), lambda i: (i, 0))
    full = lambda a, b: pl.BlockSpec((a, b), lambda i: (0, 0))
    u = pl.pallas_call(
        _merge_kernel,
        grid=(r // tm,),
        in_specs=[rows(wa), rows(wa), rows(wa), rows(wa), rows(2 * d), full(wa, d), full(wa, d)],
        out_specs=rows(d),
        out_shape=jax.ShapeDtypeStruct((r, d), jnp.bfloat16),
        compiler_params=_params(("parallel",)),
        name="merge",
    )(oa, ga, ob, gb, mg, w_ba, w_bb)
    h, hn = pl.pallas_call(
        _resid_kernel,
        grid=(r // tm,),
        in_specs=[rows(d), rows(d), full(d, d), full(1, d)],
        out_specs=[rows(d), rows(d)],
        out_shape=[jax.ShapeDtypeStruct((r, d), jnp.float32), jax.ShapeDtypeStruct((r, d), jnp.bfloat16)],
        compiler_params=_params(("parallel",)),
        name="resid",
    )(x, u, w_out, ple_g)
    return pl.pallas_call(
        _ple_kernel,
        grid=(r // tm,),
        in_specs=[rows(d), rows(d), rows(dp), full(d, d), full(dp, d)],
        out_specs=rows(d),
        out_shape=jax.ShapeDtypeStruct((r, d), jnp.float32),
        compiler_params=_params(("parallel",)),
        name="ple",
    )(h, hn, p, w_pg, w_ple)


def _tile(n, pref):
    t = min(n, pref)
    while n % t:
        t //= 2
    return t


def _layer_inputs(x, pos, lw, tm):
    xn = _rmsnorm(x, lw["ln_g"], tm)
    t64 = _rope_tables(pos, DH_A)
    t128 = _rope_tables(pos, DH_B)
    none = t64
    one = jnp.ones((1, LANES), jnp.float32)
    w = lw["w_sec"]
    pj = functools.partial(_proj, xn, tm=tm)
    (qa16,) = pj(w["qa"], lw["qn_a"], t64, norm=DH_A, rope=DH_A, scale=DH_A ** -0.5 * LOG2E, kinds=["hm16"])
    ka, ka16 = pj(w["ka"], lw["kn_a"], t64, norm=DH_A, rope=DH_A, scale=1.0, kinds=["f32", "hm16"])
    va, va16 = pj(w["va"], one, none, norm=0, rope=0, scale=1.0, kinds=["f32", "hm16"])
    (ga,) = pj(w["ga"], one, none, norm=0, rope=0, scale=1.0, kinds=["f32"])
    (qb16,) = pj(w["qb"], lw["qn_b"], t128, norm=DH_B, rope=DH_B, scale=DH_B ** -0.5 * LOG2E, kinds=["hm16"])
    kb, kb16 = pj(w["kb"], lw["kn_b"], t128, norm=DH_B, rope=DH_B, scale=1.0, kinds=["f32", "hm16"])
    vb, vb16 = pj(w["vb"], one, none, norm=0, rope=0, scale=1.0, kinds=["f32", "hm16"])
    (gb,) = pj(w["gb"], one, none, norm=0, rope=0, scale=1.0, kinds=["f32"])
    (qi16,) = pj(w["qi"], one, t64, norm=0, rope=DH_A, scale=IDX_DIM ** -0.5, kinds=["hm16"])
    (mg,) = pj(w["mg"], one, none, norm=0, rope=0, scale=1.0, kinds=["f32"])
    small, ki2 = _small_proj(xn, w["small"], lw["kn_i"], t64, tm)
    ki = small[:, :IDX_DIM]
    wi = small[:, IDX_DIM:IDX_DIM + IDX_HEADS]
    return dict(qa16=qa16, ka=ka, ka16=ka16, va=va, va16=va16, ga=ga, qb16=qb16, kb=kb, kb16=kb16,
                vb=vb, vb16=vb16, gb=gb, qi16=qi16, mg=mg, ki=ki, ki2=ki2, wi=wi)


def _row_major(hm):
    g, r, n = hm.shape
    return jnp.transpose(hm, (1, 0, 2)).reshape(r, g * n)


def _pad_new(a, bsz, ts):
    a = a.reshape(bsz, ts, a.shape[-1]).astype(jnp.bfloat16)
    return jnp.pad(a, ((0, 0), (0, LANES - ts), (0, 0)))


def kernel(x_prompt, x_sample, p_prompt, p_sample, cache_diff_k, cache_diff_v, cache_dsa_k, cache_dsa_v, cache_idx_k, ln_g, w_in, q_norm_a, k_norm_a, lam_q1, lam_k1, lam_q2, lam_k2, subln_a, q_norm_b, k_norm_b, k_norm_idx, w_branch_a, w_branch_b, w_out, ple_norm, w_ple_gate, w_ple):
    depth = w_in.shape[0]
    bp, t_p, d_model = x_prompt.shape
    bsz, t_s, _ = x_sample.shape
    past = cache_diff_k.shape[2]
    assert bp == 1 and t_s <= LANES and t_p % LANES == 0 and past % LANES == 0
    topk_p = min(TOPK_MAX, t_p // 4)
    topk_s = min(TOPK_MAX, (past + t_s) // 4)
    pos_p = jnp.arange(t_p)
    pos_s = jnp.tile(past + jnp.arange(t_s), bsz)
    r_s = bsz * t_s

    tm_p = _tile(t_p, 512)
    tm_s = _tile(r_s, 512)
    tq_a, tk_a = _tile(t_p, 1024), _tile(t_p, 1024)
    tq_i, tk_i = _tile(t_p, 128), _tile(t_p, 1024)
    tq_b, tk_b = _tile(t_p, 512), _tile(t_p, 1024)
    tk_s = _tile(past, 512)
    tm_o = _tile(t_p, 256)
    tm_os = _tile(r_s, 256)

    by_head = lambda c: c.reshape(depth, bsz, past * GROUPS, LANES)
    caches = (cache_diff_k.reshape(depth, bsz, past, SEC), by_head(cache_diff_v),
              by_head(cache_dsa_k), by_head(cache_dsa_v), cache_idx_k)

    sizes = (SEC,) * 9 + (IDX_DIM, IDX_HEADS, 2 * d_model)
    names = ("qa", "ka", "va", "ga", "qb", "kb", "vb", "gb", "qi", "ki", "wi", "mg")
    offs = [0]
    for s in sizes:
        offs.append(offs[-1] + s)

    hp = x_prompt.reshape(t_p, d_model)
    hs = x_sample.reshape(r_s, d_model)
    outs_p = [[] for _ in range(5)]
    outs_s = [[] for _ in range(5)]
    for l in range(depth):
        lam_init = 0.8 - 0.6 * math.exp(-0.3 * l)
        wl = w_in[l].astype(jnp.bfloat16)
        w_sec = {n: wl[:, offs[i]:offs[i + 1]] for i, n in enumerate(names)}
        w_sec["small"] = jnp.pad(jnp.concatenate([w_sec.pop("ki"), w_sec.pop("wi")], axis=1),
                                 ((0, 0), (0, LANES - IDX_DIM - IDX_HEADS)))
        lane_gain = lambda g: jnp.tile(g.astype(jnp.float32), LANES // g.shape[0]).reshape(1, LANES)
        lw = dict(ln_g=ln_g[l], w_sec=w_sec, qn_a=lane_gain(q_norm_a[l]), kn_a=lane_gain(k_norm_a[l]),
                  qn_b=lane_gain(q_norm_b[l]), kn_b=lane_gain(k_norm_b[l]), kn_i=lane_gain(k_norm_idx[l]))
        ok_a = _unshifted_ok(q_norm_a[l], k_norm_a[l], DH_A, DH_A ** -0.5)
        ok_b = _unshifted_ok(q_norm_b[l], k_norm_b[l], DH_B, DH_B ** -0.5)
        lam_p = jnp.stack([lam_q1[l], lam_k1[l], lam_q2[l], lam_k2[l]]).astype(jnp.float32)
        sg = subln_a[l].reshape(1, DV_A).astype(jnp.float32)
        w_o = (w_branch_a[l].astype(jnp.bfloat16), w_branch_b[l].astype(jnp.bfloat16),
               w_out[l].astype(jnp.bfloat16), ple_norm[l].reshape(1, d_model),
               w_ple_gate[l].astype(jnp.bfloat16), w_ple[l].astype(jnp.bfloat16))

        a = _layer_inputs(hp, pos_p, lw, tm_p)
        oa = _diff_attn(a["qa16"], a["ka16"], a["va16"], ok_a, lam_p, sg, lam_init, tq_a, tk_a)
        bias = _index_select(a["qi16"], a["wi"], a["ki2"], topk_p, tq_i, tk_i)
        ob = _dsa_attn(a["qb16"], a["kb16"], a["vb16"], bias, ok_b, tq_b, tk_b)
        hp = _layer_output(hp, p_prompt[l].reshape(t_p, -1), oa, a["ga"], ob, a["gb"], a["mg"], *w_o, tm_o)
        for lst, key in zip(outs_p, ("ka", "va", "kb", "vb", "ki")):
            lst.append(a[key])

        s = _layer_inputs(hs, pos_s, lw, tm_s)
        qa_s = _row_major(s["qa16"]).reshape(bsz, t_s, SEC)
        qb_s = _row_major(s["qb16"]).reshape(bsz, t_s, SEC)
        qi_s = _row_major(s["qi16"]).reshape(bsz, t_s, IDX_HEADS, IDX_DIM)
        qi_s = jnp.transpose(qi_s, (0, 2, 1, 3)).reshape(bsz, IDX_HEADS * t_s, IDX_DIM)
        wi_s = jnp.transpose(s["wi"].reshape(bsz, t_s, IDX_HEADS), (0, 2, 1)).reshape(bsz, IDX_HEADS * t_s, 1)
        news = tuple(_pad_new(s[k], bsz, t_s) for k in ("ka", "va", "kb", "vb", "ki"))
        oa_s, ob_s = _sample_attn(l, qa_s, qb_s, qi_s, wi_s, caches, news, lam_p, sg, lam_init, topk_s, tk_s)
        hs = _layer_output(hs, p_sample[l].reshape(r_s, -1), oa_s.reshape(r_s, SEC), s["ga"],
                           ob_s.reshape(r_s, SEC), s["gb"], s["mg"], *w_o, tm_os)
        for lst, key in zip(outs_s, ("ka", "va", "kb", "vb", "ki")):
            lst.append(s[key])

    def stack(lst, shape):
        return jnp.stack(lst).reshape((depth,) + shape)

    return (hp.reshape(bp, t_p, d_model), hs.reshape(bsz, t_s, d_model),
            stack(outs_p[0], (bp, t_p, H_A, 2, DH_A)), stack(outs_p[1], (bp, t_p, H_A, DV_A)),
            stack(outs_p[2], (bp, t_p, H_B, DH_B)), stack(outs_p[3], (bp, t_p, H_B, DH_B)),
            stack(outs_p[4], (bp, t_p, IDX_DIM)),
            stack(outs_s[0], (bsz, t_s, H_A, 2, DH_A)), stack(outs_s[1], (bsz, t_s, H_A, DV_A)),
            stack(outs_s[2], (bsz, t_s, H_B, DH_B)), stack(outs_s[3], (bsz, t_s, H_B, DH_B)),
            stack(outs_s[4], (bsz, t_s, IDX_DIM)))
```

```python
import functools
import math

import jax
import jax.numpy as jnp
import numpy as np
from jax import lax
from jax.experimental import pallas as pl
from jax.experimental.pallas import tpu as pltpu

CHUNK = 64
ROPE_THETA = 500000.0
EPS = 1e-6
H_A = 8
DH_A = 64
DV_A = 128
H_B = 8
DH_B = 128
IDX_HEADS = 16
IDX_DIM = 64
TOPK_MAX = 256
LANES = 128
GROUPS = 8
SEC = GROUPS * LANES
ROW_GROUP = 128
NEG = -1e30
LOG2E = 1.4426950408889634
MAX_UNSHIFTED = 60.0
MOST_NEG = -3.0e38
VMEM_LIMIT = 56 * 1024 * 1024

_NT = (((1,), (1,)), ((), ()))


def _cdiv(a, b):
    return (a + b - 1) // b


def _params(sem, vmem=VMEM_LIMIT):
    return pltpu.CompilerParams(dimension_semantics=sem, vmem_limit_bytes=vmem)


def _rmsnorm_kernel(x_ref, g_ref, o_ref):
    x = x_ref[...]
    ms = jnp.mean(x * x, axis=-1, keepdims=True)
    o_ref[...] = (x * lax.rsqrt(ms + EPS) * g_ref[...]).astype(o_ref.dtype)


def _rmsnorm(x, g, tm):
    r, d = x.shape
    return pl.pallas_call(
        _rmsnorm_kernel,
        grid=(r // tm,),
        in_specs=[pl.BlockSpec((tm, d), lambda i: (i, 0)),
                  pl.BlockSpec((1, d), lambda i: (0, 0))],
        out_specs=pl.BlockSpec((tm, d), lambda i: (i, 0)),
        out_shape=jax.ShapeDtypeStruct((r, d), jnp.bfloat16),
        compiler_params=_params(("parallel",)),
        name="rmsnorm",
    )(x, g.reshape(1, d))


def _head_norm(z, g, d):
    sq = z * z
    if d == LANES:
        r = lax.rsqrt(jnp.sum(sq, axis=-1, keepdims=True) * (1.0 / d) + EPS)
    else:
        lo = lax.broadcasted_iota(jnp.int32, z.shape, 1) < d
        s_lo = jnp.sum(jnp.where(lo, sq, 0.0), axis=-1, keepdims=True)
        s_hi = jnp.sum(jnp.where(lo, 0.0, sq), axis=-1, keepdims=True)
        r = jnp.where(lo, lax.rsqrt(s_lo * (1.0 / d) + EPS), lax.rsqrt(s_hi * (1.0 / d) + EPS))
    return z * r * g


def _rope(z, c, a, b, half):
    return z * c + pltpu.roll(z, LANES - half, 1) * a + pltpu.roll(z, half, 1) * b


def _proj_kernel(xn_ref, w_ref, g_ref, rc_ref, ra_ref, rb_ref, *out_refs, norm, rope, scale, kinds):
    tm = xn_ref.shape[0]
    chunk = tm // 2 if (norm or rope) and tm % 512 == 0 else tm
    for r0 in range(0, tm, chunk):
        rows = slice(r0, r0 + chunk)
        z = jnp.dot(xn_ref[rows, :], w_ref[...], preferred_element_type=jnp.float32)
        for c in range(GROUPS):
            zc = z[:, c * LANES:(c + 1) * LANES]
            if norm:
                zc = _head_norm(zc, g_ref[...], norm)
            if rope:
                zc = _rope(zc, rc_ref[rows, :], ra_ref[rows, :], rb_ref[rows, :], rope // 8)
            for kind, o_ref in zip(kinds, out_refs):
                if kind == "f32":
                    o_ref[rows, c * LANES:(c + 1) * LANES] = zc
                else:
                    o_ref[c, rows, :] = (zc * scale).astype(o_ref.dtype)


def _proj(xn, w, gain, tables, *, norm, rope, scale, kinds, tm):
    r, d = xn.shape
    nj = w.shape[1] // SEC
    rc, ra, rb = tables
    out_shape, out_specs = [], []
    for kind in kinds:
        if kind == "f32":
            out_shape.append(jax.ShapeDtypeStruct((r, nj * SEC), jnp.float32))
            out_specs.append(pl.BlockSpec((tm, SEC), lambda j, i: (i, j)))
        else:
            assert nj == 1
            out_shape.append(jax.ShapeDtypeStruct((GROUPS, r, LANES), jnp.bfloat16))
            out_specs.append(pl.BlockSpec((GROUPS, tm, LANES), lambda j, i: (0, i, 0)))
    row = pl.BlockSpec((tm, LANES), lambda j, i: (i, 0))
    outs = pl.pallas_call(
        functools.partial(_proj_kernel, norm=norm, rope=rope, scale=scale, kinds=tuple(kinds)),
        grid=(nj, r // tm),
        in_specs=[pl.BlockSpec((tm, d), lambda j, i: (i, 0)),
                  pl.BlockSpec((d, SEC), lambda j, i: (0, j)),
                  pl.BlockSpec((1, LANES), lambda j, i: (0, 0)),
                  row, row, row],
        out_specs=out_specs,
        out_shape=out_shape,
        compiler_params=_params(("parallel", "parallel")),
        name="proj",
    )(xn, w, gain, rc, ra, rb)
    return outs


def _small_proj_kernel(xn_ref, w_ref, g_ref, rc_ref, ra_ref, rb_ref, f_ref, k2_ref, *, wscale):
    z = jnp.dot(xn_ref[...], w_ref[...], preferred_element_type=jnp.float32)
    lo = lax.broadcasted_iota(jnp.int32, z.shape, 1) < IDX_DIM
    ms = jnp.sum(jnp.where(lo, z * z, 0.0), axis=-1, keepdims=True) * (1.0 / IDX_DIM)
    zn = z * lax.rsqrt(ms + EPS) * g_ref[...]
    zr = _rope(zn, rc_ref[...], ra_ref[...], rb_ref[...], IDX_DIM // 8)
    f_ref[...] = jnp.where(lo, zr, z * wscale)
    k2_ref[...] = jnp.where(lo, zr, pltpu.roll(zr, IDX_DIM, 1)).astype(k2_ref.dtype)


def _small_proj(xn, w, gain, tables, tm):
    r, d = xn.shape
    rc, ra, rb = tables
    row = pl.BlockSpec((tm, LANES), lambda i: (i, 0))
    return pl.pallas_call(
        functools.partial(_small_proj_kernel, wscale=IDX_HEADS ** -0.5),
        grid=(r // tm,),
        in_specs=[pl.BlockSpec((tm, d), lambda i: (i, 0)),
                  pl.BlockSpec((d, LANES), lambda i: (0, 0)),
                  pl.BlockSpec((1, LANES), lambda i: (0, 0)),
                  row, row, row],
        out_specs=[row, row],
        out_shape=[jax.ShapeDtypeStruct((r, LANES), jnp.float32),
                   jax.ShapeDtypeStruct((r, LANES), jnp.bfloat16)],
        compiler_params=_params(("parallel",)),
        name="small_proj",
    )(xn, w, gain, rc, ra, rb)


def _rope_tables(pos, d):
    r = d // 4
    h = r // 2
    n = pos.shape[0]
    inv = ROPE_THETA ** (-jnp.arange(0, r, 2, dtype=jnp.float32) / r)
    ang = pos.astype(jnp.float32)[:, None] * inv[None, :]
    cos, sin = jnp.cos(ang), jnp.sin(ang)
    zh = jnp.zeros((n, h), jnp.float32)
    zr = jnp.zeros((n, d - r), jnp.float32)
    c = jnp.concatenate([cos, cos, jnp.ones((n, d - r), jnp.float32)], axis=-1)
    a = jnp.concatenate([-sin, zh, zr], axis=-1)
    b = jnp.concatenate([zh, sin, zr], axis=-1)
    rep = LANES // d
    return tuple(jnp.tile(t, (1, rep)) for t in (c, a, b))


def _online_update(s, v, m_ref, l_ref, acc_ref):
    m_prev = m_ref[...]
    m_new = jnp.maximum(m_prev, jnp.max(s, axis=-1, keepdims=True))
    alpha = jnp.exp2(m_prev - m_new)
    p = jnp.exp2(s - m_new)
    l_ref[...] = alpha * l_ref[...] + jnp.sum(p, axis=-1, keepdims=True)
    acc_ref[...] = alpha * acc_ref[...] + jnp.dot(p.astype(v.dtype), v, preferred_element_type=jnp.float32)
    m_ref[...] = m_new


def _causal_steps(t, tq, tk):
    qs, ks = [], []
    for qi in range(t // tq):
        nk = _cdiv((qi + 1) * tq, tk)
        qs += [qi] * nk
        ks += list(range(nk))
    return np.asarray(qs, np.int32), np.asarray(ks, np.int32)


def _unshifted_ok(gq, gk, d, scale):
    bound = jnp.max(jnp.abs(gq)) * jnp.max(jnp.abs(gk)) * (d * scale * (1.0 + 2.0 ** -6))
    return (bound <= MAX_UNSHIFTED).astype(jnp.int32).reshape(1)


def _lambda(lam_ref, lam_init):
    lp = lam_ref[...]
    s1 = jnp.sum(lp[0:1] * lp[1:2], axis=-1, keepdims=True)
    s2 = jnp.sum(lp[2:3] * lp[3:4], axis=-1, keepdims=True)
    return jnp.exp(s1) - jnp.exp(s2) + lam_init


def _diff_finish(o0, o1, lam, sg, lam_init):
    o = o0 - lam * o1
    ms = jnp.mean(o * o, axis=-1, keepdims=True)
    return o * lax.rsqrt(ms + EPS) * sg * (1.0 - lam_init)


def _chunk_end(pos):
    return (pos // CHUNK + 1) * CHUNK - 1


def _diff_attn_kernel(qt_ref, kt_ref, ok_ref, q_ref, k_ref, v_ref, lam_ref, sg_ref, o_ref,
                      qs_ref, v1_ref, acc_ref, m_ref, l_ref, *, tq, tk, lam_init):
    step_id = pl.program_id(1)
    qi = qt_ref[step_id]
    kj = kt_ref[step_id]
    nk = _cdiv((qi + 1) * tq, tk)
    unshifted = ok_ref[0] == 1
    full = (kj + 1) * tk <= qi * tq + CHUNK
    rows2 = 2 * tq

    @pl.when(kj == 0)
    def _():
        q = q_ref[...]
        lo = lax.broadcasted_iota(jnp.int32, q.shape, 1) < DH_A
        zero = jnp.zeros_like(q)
        qs_ref[0:tq, :] = jnp.where(lo, q, zero)
        qs_ref[tq:rows2, :] = jnp.where(lo, zero, q)
        v1_ref[:, DV_A:] = jnp.ones((tk, LANES), v1_ref.dtype)
        m_ref[...] = jnp.full(m_ref.shape, NEG, jnp.float32)
        l_ref[...] = jnp.zeros(l_ref.shape, jnp.float32)
        acc_ref[...] = jnp.zeros(acc_ref.shape, jnp.float32)

    v1_ref[:, :DV_A] = v_ref[...]

    def visible(r0, nrows):
        row = r0 + lax.broadcasted_iota(jnp.int32, (nrows, tk), 0)
        row = jnp.where(row >= tq, row - tq, row)
        kpos = kj * tk + lax.broadcasted_iota(jnp.int32, (nrows, tk), 1)
        return kpos <= _chunk_end(qi * tq + row)

    def fast_step(masked):
        k = k_ref[...]
        v1 = v1_ref[...]
        for r0 in range(0, rows2, ROW_GROUP):
            rows = slice(r0, r0 + ROW_GROUP)
            s = lax.dot_general(qs_ref[rows, :], k, _NT, preferred_element_type=jnp.float32)
            if masked:
                s = jnp.where(visible(r0, ROW_GROUP), s, NEG)
            p = jnp.exp2(s).astype(v1.dtype)
            acc_ref[rows, :] += jnp.dot(p, v1, preferred_element_type=jnp.float32)

    def exact_step(masked):
        s = lax.dot_general(qs_ref[...], k_ref[...], _NT, preferred_element_type=jnp.float32)
        if masked:
            s = jnp.where(visible(0, rows2), s, NEG)
        _online_update(s, v_ref[...], m_ref, l_ref, acc_ref.at[:, :DV_A])

    for fast, step in ((True, fast_step), (False, exact_step)):
        mode = unshifted if fast else jnp.logical_not(unshifted)

        @pl.when(jnp.logical_and(mode, full))
        def _():
            step(False)

        @pl.when(jnp.logical_and(mode, jnp.logical_not(full)))
        def _():
            step(True)

    @pl.when(kj == nk - 1)
    def _():
        @pl.when(jnp.logical_not(unshifted))
        def _():
            acc_ref[:, DV_A:] = jnp.broadcast_to(l_ref[...], (rows2, LANES))

        o = acc_ref[:, :DV_A] / acc_ref[:, DV_A:]
        o_ref[...] = _diff_finish(o[0:tq], o[tq:rows2], _lambda(lam_ref, lam_init), sg_ref[...], lam_init)


def _diff_attn(q, k, v, ok, lam_p, sg, lam_init, tq, tk):
    _, t, _ = q.shape
    qt, kt = _causal_steps(t, tq, tk)
    kv_spec = pl.BlockSpec((None, tk, LANES), lambda h, s, qt, kt, ok: (h, kt[s], 0))
    const = lambda h, s, qt, kt, ok: (0, 0)
    grid_spec = pltpu.PrefetchScalarGridSpec(
        num_scalar_prefetch=3,
        grid=(H_A, len(qt)),
        in_specs=[pl.BlockSpec((None, tq, LANES), lambda h, s, qt, kt, ok: (h, qt[s], 0)),
                  kv_spec, kv_spec,
                  pl.BlockSpec((4, DH_A), const),
                  pl.BlockSpec((1, DV_A), const)],
        out_specs=pl.BlockSpec((tq, LANES), lambda h, s, qt, kt, ok: (qt[s], h)),
        scratch_shapes=[pltpu.VMEM((2 * tq, LANES), jnp.bfloat16),
                        pltpu.VMEM((tk, DV_A + LANES), jnp.bfloat16),
                        pltpu.VMEM((2 * tq, DV_A + LANES), jnp.float32),
                        pltpu.VMEM((2 * tq, 1), jnp.float32),
                        pltpu.VMEM((2 * tq, 1), jnp.float32)])
    return pl.pallas_call(
        functools.partial(_diff_attn_kernel, tq=tq, tk=tk, lam_init=lam_init),
        grid_spec=grid_spec,
        out_shape=jax.ShapeDtypeStruct((t, H_A * DV_A), jnp.float32),
        compiler_params=_params(("parallel", "arbitrary")),
        name="diff_attn",
    )(jnp.asarray(qt), jnp.asarray(kt), ok, q, k, v, lam_p, sg)


def _topk_threshold(count, lo, hi, all_sel, topk):
    kf = float(topk)
    c_lo = count(lo)
    c_hi = jnp.zeros_like(c_lo)
    done = jnp.where(jnp.logical_or(all_sel, c_lo == kf), 1.0, 0.0)

    def cond(st):
        return jnp.logical_and(st[0] < 120, st[6] < 0.5)

    def body(st):
        it, lo, hi, c_lo, c_hi, done, _ = st
        mid = 0.5 * lo + 0.5 * hi
        stuck = jnp.logical_or(mid <= lo, mid >= hi)
        frac = (jnp.log(c_lo) - math.log(kf)) / (jnp.log(c_lo) - jnp.log(jnp.maximum(c_hi, 0.5)))
        t = lo + (hi - lo) * frac
        use = jnp.logical_and(lax.rem(it, 3) != 2, jnp.logical_and(t > lo, t < hi))
        t = jnp.where(use, t, mid)
        c = count(t)
        live = jnp.logical_and(done < 0.5, jnp.logical_not(stuck))
        up = jnp.logical_and(live, c >= kf)
        dn = jnp.logical_and(live, c < kf)
        lo, c_lo = jnp.where(up, t, lo), jnp.where(up, c, c_lo)
        hi, c_hi = jnp.where(dn, t, hi), jnp.where(dn, c, c_hi)
        done = jnp.where(jnp.logical_or(c_lo == kf, stuck), 1.0, done)
        return it + 1, lo, hi, c_lo, c_hi, done, jnp.min(done)

    st = (jnp.int32(0), lo, hi, c_lo, c_hi, done, jnp.min(done))
    _, lo, hi, c_lo, c_hi, _, _ = lax.while_loop(cond, body, st)
    thr = jnp.where(all_sel, MOST_NEG, lo)
    tied = jnp.logical_and(jnp.logical_not(all_sel), c_lo > kf)
    return thr, tied, c_hi


def _above(x):
    return x + jnp.abs(x) * 2.0 ** -20 + 1e-30


def _index_kernel(qt_ref, kt_ref, q_ref, w_ref, k_ref, bias_ref, qs_ref, sc_ref, lo_ref, *, tq, tk, t, topk):
    step_id = pl.program_id(0)
    qi = qt_ref[step_id]
    kj = kt_ref[step_id]
    nk = _cdiv((qi + 1) * tq, tk)
    nsub = tk // LANES

    @pl.when(kj == 0)
    def _():
        for p in range(GROUPS):
            q = q_ref[p]
            lo = lax.broadcasted_iota(jnp.int32, q.shape, 1) < IDX_DIM
            zero = jnp.zeros_like(q)
            qs_ref[(2 * p) * tq:(2 * p + 1) * tq, :] = jnp.where(lo, q, zero)
            qs_ref[(2 * p + 1) * tq:(2 * p + 2) * tq, :] = jnp.where(lo, zero, q)

    logits = lax.dot_general(qs_ref[...], k_ref[...], _NT, preferred_element_type=jnp.float32)
    w = w_ref[...]
    score = None
    for h in range(IDX_HEADS):
        term = jnp.maximum(logits[h * tq:(h + 1) * tq], 0.0) * w[:, h:h + 1]
        score = term if score is None else score + term
    qpos = qi * tq + lax.broadcasted_iota(jnp.int32, score.shape, 0)
    kpos = kj * tk + lax.broadcasted_iota(jnp.int32, score.shape, 1)
    score = jnp.where(kpos <= _chunk_end(qpos), score, -jnp.inf)
    for c in range(nsub):
        sc_ref[kj * nsub + c] = score[:, c * LANES:(c + 1) * LANES]

    @pl.when(kj == nk - 1)
    def _():
        lane = lax.broadcasted_iota(jnp.int32, (tq, LANES), 1)
        cend = _chunk_end(qi * tq + lax.broadcasted_iota(jnp.int32, (tq, LANES), 0))

        def rows(x):
            return jnp.broadcast_to(x, (tq, LANES))

        def sweep(fn, init):
            def body(j, acc):
                for c in range(nsub):
                    acc = fn(acc, sc_ref[j * nsub + c], j * tk + c * LANES + lane, c % 2)
                return acc
            acc = lax.fori_loop(0, nk // 2, lambda j, a: body(2 * j + 1, body(2 * j, a)), init)
            return lax.fori_loop(nk - nk % 2, nk, body, acc)

        def count(pred):
            acc = sweep(lambda a, s, idx, par: jnp.where(pred(s, idx), a + 1.0, a),
                        jnp.zeros((tq, LANES), jnp.float32))
            return rows(jnp.sum(acc, axis=1, keepdims=True))

        ninf = jnp.full((tq, LANES), -jnp.inf, jnp.float32)

        def group_max(acc, s, idx, par):
            return (jnp.maximum(acc[0], s), acc[1]) if par == 0 else (acc[0], jnp.maximum(acc[1], s))

        m0, m1 = sweep(group_max, (ninf, ninf))
        lo = rows(jnp.min(jnp.minimum(m0, m1), axis=1, keepdims=True))
        row_max = rows(jnp.max(jnp.maximum(m0, m1), axis=1, keepdims=True))
        all_sel = cend + 1 <= topk
        lo_ref[...] = lo

        @pl.when(jnp.max(jnp.where(jnp.logical_and(lo == -jnp.inf, jnp.logical_not(all_sel)), 1.0, 0.0)) > 0.5)
        def _():
            low = sweep(lambda a, s, idx, par: jnp.minimum(a, jnp.where(idx <= cend, s, jnp.inf)),
                        jnp.full((tq, LANES), jnp.inf, jnp.float32))
            lo_ref[...] = rows(jnp.min(low, axis=1, keepdims=True))

        lo = jnp.where(all_sel, 0.0, lo_ref[...])
        thr, tied, n_gt = _topk_threshold(lambda t: count(lambda s, idx: s >= t), lo,
                                          jnp.where(all_sel, 1.0, _above(row_max)), all_sel, topk)
        need = jnp.where(tied, float(topk) - n_gt, jnp.inf)

        def write(select):
            def body(j, carry):
                for c in range(nsub):
                    off = pl.multiple_of(j * tk + c * LANES, LANES)
                    sel = select(sc_ref[j * nsub + c], off + lane)
                    bias_ref[:, pl.ds(off, LANES)] = jnp.where(sel, 0.0, NEG).astype(bias_ref.dtype)
                return carry
            lax.fori_loop(0, nk, body, 0)

        any_tied = jnp.max(jnp.where(tied, 1.0, 0.0)) > 0.5

        @pl.when(jnp.logical_not(any_tied))
        def _():
            write(lambda s, idx: s >= thr)

        @pl.when(any_tied)
        def _():
            nbits = max(1, (t - 1).bit_length())

            def pos_body(b, pos):
                cand = pos | lax.shift_left(jnp.int32(1), jnp.int32(nbits - 1) - b)
                g = count(lambda s, idx: jnp.logical_and(s == thr, idx < cand))
                return jnp.where(g < need, cand, pos)

            last = lax.fori_loop(0, nbits, pos_body, jnp.zeros((tq, LANES), jnp.int32))
            write(lambda s, idx: jnp.logical_or(s > thr, jnp.logical_and(s == thr, idx <= last)))

        def fill_body(j, carry):
            off = pl.multiple_of(j * tk, tk)
            bias_ref[:, pl.ds(off, tk)] = jnp.full((tq, tk), NEG, bias_ref.dtype)
            return carry

        lax.fori_loop(nk, t // tk, fill_body, 0)


def _index_select(qi16, wi, ki2, topk, tq, tk):
    _, t, _ = qi16.shape
    qt, kt = _causal_steps(t, tq, tk)
    grid_spec = pltpu.PrefetchScalarGridSpec(
        num_scalar_prefetch=2,
        grid=(len(qt),),
        in_specs=[pl.BlockSpec((GROUPS, tq, LANES), lambda s, qt, kt: (0, qt[s], 0)),
                  pl.BlockSpec((tq, IDX_HEADS), lambda s, qt, kt: (qt[s], 0)),
                  pl.BlockSpec((tk, LANES), lambda s, qt, kt: (kt[s], 0))],
        out_specs=pl.BlockSpec((tq, t), lambda s, qt, kt: (qt[s], 0)),
        scratch_shapes=[pltpu.VMEM((IDX_HEADS * tq, LANES), jnp.bfloat16),
                        pltpu.VMEM((t // LANES, tq, LANES), jnp.float32),
                        pltpu.VMEM((tq, LANES), jnp.float32)])
    return pl.pallas_call(
        functools.partial(_index_kernel, tq=tq, tk=tk, t=t, topk=topk),
        grid_spec=grid_spec,
        out_shape=jax.ShapeDtypeStruct((t, t), jnp.bfloat16),
        compiler_params=_params(("arbitrary",)),
        name="index_select",
    )(jnp.asarray(qt), jnp.asarray(kt), qi16, wi, ki2)


def _dsa_attn_kernel(qt_ref, kt_ref, ok_ref, q_ref, k_ref, v_ref, bias_ref, o_ref,
                     v1_ref, acc_ref, m_ref, l_ref, *, tq, tk):
    step_id = pl.program_id(0)
    qi = qt_ref[step_id]
    kj = kt_ref[step_id]
    nk = _cdiv((qi + 1) * tq, tk)
    unshifted = ok_ref[0] == 1

    @pl.when(kj == 0)
    def _():
        v1_ref[:, :, DH_B:] = jnp.ones((H_B, tk, LANES), v1_ref.dtype)
        m_ref[...] = jnp.full(m_ref.shape, NEG, jnp.float32)
        l_ref[...] = jnp.zeros(l_ref.shape, jnp.float32)
        acc_ref[...] = jnp.zeros(acc_ref.shape, jnp.float32)

    @pl.when(unshifted)
    def _():
        v1_ref[:, :, :DH_B] = v_ref[...]
        for r0 in range(0, tq, ROW_GROUP):
            rows = slice(r0, r0 + ROW_GROUP)
            bias = bias_ref[rows, :].astype(jnp.float32)
            for h in range(H_B):
                s = lax.dot_general(q_ref[h, rows, :], k_ref[h], _NT, preferred_element_type=jnp.float32) + bias
                p = jnp.exp2(s).astype(v1_ref.dtype)
                acc_ref[h, rows, :] += jnp.dot(p, v1_ref[h], preferred_element_type=jnp.float32)

    @pl.when(jnp.logical_not(unshifted))
    def _():
        bias = bias_ref[...].astype(jnp.float32)
        for h in range(H_B):
            s = lax.dot_general(q_ref[h], k_ref[h], _NT, preferred_element_type=jnp.float32) + bias
            _online_update(s, v_ref[h], m_ref.at[h], l_ref.at[h], acc_ref.at[h, :, :DH_B])

    @pl.when(kj == nk - 1)
    def _():
        @pl.when(jnp.logical_not(unshifted))
        def _():
            acc_ref[:, :, DH_B:] = jnp.broadcast_to(l_ref[...], (H_B, tq, LANES))

        for h in range(H_B):
            o_ref[:, h * LANES:(h + 1) * LANES] = acc_ref[h, :, :DH_B] / acc_ref[h, :, DH_B:]


def _dsa_attn(q, k, v, bias, ok, tq, tk):
    _, t, _ = q.shape
    qt, kt = _causal_steps(t, tq, tk)
    kv_spec = pl.BlockSpec((H_B, tk, LANES), lambda s, qt, kt, ok: (0, kt[s], 0))
    grid_spec = pltpu.PrefetchScalarGridSpec(
        num_scalar_prefetch=3,
        grid=(len(qt),),
        in_specs=[pl.BlockSpec((H_B, tq, LANES), lambda s, qt, kt, ok: (0, qt[s], 0)),
                  kv_spec, kv_spec,
                  pl.BlockSpec((tq, tk), lambda s, qt, kt, ok: (qt[s], kt[s]))],
        out_specs=pl.BlockSpec((tq, H_B * DH_B), lambda s, qt, kt, ok: (qt[s], 0)),
        scratch_shapes=[pltpu.VMEM((H_B, tk, DH_B + LANES), jnp.bfloat16),
                        pltpu.VMEM((H_B, tq, DH_B + LANES), jnp.float32),
                        pltpu.VMEM((H_B, tq, 1), jnp.float32),
                        pltpu.VMEM((H_B, tq, 1), jnp.float32)])
    return pl.pallas_call(
        functools.partial(_dsa_attn_kernel, tq=tq, tk=tk),
        grid_spec=grid_spec,
        out_shape=jax.ShapeDtypeStruct((t, H_B * DH_B), jnp.float32),
        compiler_params=_params(("arbitrary",)),
        name="dsa_attn",
    )(jnp.asarray(qt), jnp.asarray(kt), ok, q, k, v, bias)


def _sample_kernel(qa_ref, qb_ref, qi_ref, wi_ref, ckd_ref, cvd_ref, ckb_ref, cvb_ref, cki_ref,
                   nkd_ref, nvd_ref, nkb_ref, nvb_ref, nki_ref, lam_ref, sg_ref,
                   oa_ref, ob_ref,
                   qda_ref, qdb_ref, bias_ref, sc_ref, ma_ref, la_ref, acca_ref, mb_ref, lb_ref, accb_ref,
                   *, ts, past, tk, topk, lam_init):
    kj = pl.program_id(1)
    nkt = past // tk
    width = past + LANES
    ra = 2 * H_A * ts
    rb = H_B * ts

    def new_key_valid(rows):
        t_q = lax.broadcasted_iota(jnp.int32, (rows, LANES), 0) % ts
        lane = lax.broadcasted_iota(jnp.int32, (rows, LANES), 1)
        return jnp.logical_and(lane < ts, past + lane <= _chunk_end(past + t_q))

    @pl.when(kj == 0)
    def _():
        qa = qa_ref[...]
        grp = lax.broadcasted_iota(jnp.int32, qa.shape, 1) // DH_A
        for hc in range(2 * H_A):
            qda_ref[hc * ts:(hc + 1) * ts, :] = jnp.where(grp == hc, qa, jnp.zeros_like(qa))
        qb = qb_ref[...]
        grp = lax.broadcasted_iota(jnp.int32, qb.shape, 1) // DH_B
        for h in range(H_B):
            qdb_ref[h * ts:(h + 1) * ts, :] = jnp.where(grp == h, qb, jnp.zeros_like(qb))
        for ref in (ma_ref, mb_ref):
            ref[...] = jnp.full(ref.shape, NEG, jnp.float32)
        for ref in (la_ref, lb_ref, acca_ref, accb_ref):
            ref[...] = jnp.zeros(ref.shape, jnp.float32)

        qi = qi_ref[...]
        w = wi_ref[...]

        def scores(keys16):
            logits = lax.dot_general(qi, keys16, _NT, preferred_element_type=jnp.float32)
            score = None
            for h in range(IDX_HEADS):
                term = jnp.maximum(logits[h * ts:(h + 1) * ts], 0.0) * w[h * ts:(h + 1) * ts]
                score = term if score is None else score + term
            return score

        step = min(past, 1024)
        for c in range(past // step):
            keys16 = cki_ref[c * step:(c + 1) * step, :].astype(jnp.bfloat16)
            sc_ref[:, c * step:(c + 1) * step] = scores(keys16)
        sc_ref[:, past:width] = jnp.where(new_key_valid(ts), scores(nki_ref[...]), -jnp.inf)

        def rows(x):
            return jnp.broadcast_to(x, (ts, LANES))

        def wide(x):
            return x[:, 0:1]

        def count(pred):
            idx = lax.broadcasted_iota(jnp.int32, (ts, width), 1)
            return rows(jnp.sum(jnp.where(pred(sc_ref[...], idx), 1.0, 0.0), axis=1, keepdims=True))

        sc = sc_ref[...]
        m0 = m1 = jnp.full((ts, LANES), -jnp.inf, jnp.float32)
        for c in range(width // LANES):
            slab = sc[:, c * LANES:(c + 1) * LANES]
            m0, m1 = (jnp.maximum(m0, slab), m1) if c % 2 == 0 else (m0, jnp.maximum(m1, slab))
        lo = rows(jnp.min(jnp.minimum(m0, m1), axis=1, keepdims=True))
        row_min = rows(jnp.min(jnp.where(sc == -jnp.inf, jnp.inf, sc), axis=1, keepdims=True))
        lo = jnp.where(lo == -jnp.inf, row_min, lo)
        row_max = rows(jnp.max(jnp.maximum(m0, m1), axis=1, keepdims=True))
        n_valid = count(lambda s, idx: s > -jnp.inf)
        all_sel = n_valid <= float(topk)
        thr, tied, n_gt = _topk_threshold(lambda t: count(lambda s, idx: s >= wide(t)),
                                          jnp.where(all_sel, 0.0, lo), jnp.where(all_sel, 1.0, _above(row_max)),
                                          all_sel, topk)

        def write(select):
            s = sc_ref[...]
            sel = select(s, lax.broadcasted_iota(jnp.int32, (ts, width), 1))
            bias = jnp.where(sel, 0.0, NEG)
            for h in range(H_B):
                bias_ref[h * ts:(h + 1) * ts, :] = bias

        any_tied = jnp.max(jnp.where(tied, 1.0, 0.0)) > 0.5

        @pl.when(jnp.logical_not(any_tied))
        def _():
            write(lambda s, idx: s >= wide(thr))

        @pl.when(any_tied)
        def _():
            need = jnp.where(tied, float(topk) - n_gt, jnp.inf)
            nbits = max(1, (width - 1).bit_length())

            def pos_body(b, pos):
                cand = pos | lax.shift_left(jnp.int32(1), jnp.int32(nbits - 1) - b)
                g = count(lambda s, idx: jnp.logical_and(s == wide(thr), idx < wide(cand)))
                return jnp.where(g < need, cand, pos)

            last = lax.fori_loop(0, nbits, pos_body, jnp.zeros((ts, LANES), jnp.int32))
            write(lambda s, idx: jnp.logical_or(s > wide(thr), jnp.logical_and(s == wide(thr), idx <= wide(last))))

    def head_rows(ref):
        return jnp.concatenate([ref[pl.ds(h, tk, stride=GROUPS), :].astype(jnp.bfloat16) for h in range(GROUPS)],
                               axis=1)

    sa = lax.dot_general(qda_ref[...], ckd_ref[...], _NT, preferred_element_type=jnp.float32)
    _online_update(sa, head_rows(cvd_ref), ma_ref, la_ref, acca_ref)
    sb = lax.dot_general(qdb_ref[...], head_rows(ckb_ref), _NT, preferred_element_type=jnp.float32)
    sb = sb + bias_ref[:, pl.ds(pl.multiple_of(kj * tk, tk), tk)]
    _online_update(sb, head_rows(cvb_ref), mb_ref, lb_ref, accb_ref)

    @pl.when(kj == nkt - 1)
    def _():
        sa = lax.dot_general(qda_ref[...], nkd_ref[...], _NT, preferred_element_type=jnp.float32)
        sa = jnp.where(new_key_valid(ra), sa, NEG)
        _online_update(sa, nvd_ref[...], ma_ref, la_ref, acca_ref)
        sb = lax.dot_general(qdb_ref[...], nkb_ref[...], _NT, preferred_element_type=jnp.float32)
        sb = sb + bias_ref[:, past:width]
        _online_update(sb, nvb_ref[...], mb_ref, lb_ref, accb_ref)

        lam = _lambda(lam_ref, lam_init)
        oa = acca_ref[...] / la_ref[...]
        ob = accb_ref[...] / lb_ref[...]
        for h in range(H_A):
            cols = slice(h * DV_A, (h + 1) * DV_A)
            o0 = oa[(2 * h) * ts:(2 * h + 1) * ts, cols]
            o1 = oa[(2 * h + 1) * ts:(2 * h + 2) * ts, cols]
            oa_ref[:, cols] = _diff_finish(o0, o1, lam, sg_ref[...], lam_init)
        for h in range(H_B):
            cols = slice(h * DH_B, (h + 1) * DH_B)
            ob_ref[:, cols] = ob[h * ts:(h + 1) * ts, cols]


def _sample_attn(l, qa, qb, qi, wi, caches, news, lam_p, sg, lam_init, topk, tk):
    bsz, ts, _ = qa.shape
    ckd, cvd, ckb, cvb, cki = caches
    past = cki.shape[2]
    width = past + LANES
    ra, rb = 2 * H_A * ts, H_B * ts
    cache_spec = pl.BlockSpec((None, None, tk, SEC), lambda b, kj: (l, b, kj, 0))
    heads_spec = pl.BlockSpec((None, None, tk * GROUPS, LANES), lambda b, kj: (l, b, kj, 0))
    new_spec = pl.BlockSpec((None, LANES, SEC), lambda b, kj: (b, 0, 0))
    row_spec = pl.BlockSpec((None, ts, SEC), lambda b, kj: (b, 0, 0))
    return pl.pallas_call(
        functools.partial(_sample_kernel, ts=ts, past=past, tk=tk, topk=topk, lam_init=lam_init),
        grid=(bsz, past // tk),
        in_specs=[row_spec, row_spec,
                  pl.BlockSpec((None, IDX_HEADS * ts, IDX_DIM), lambda b, kj: (b, 0, 0)),
                  pl.BlockSpec((None, IDX_HEADS * ts, 1), lambda b, kj: (b, 0, 0)),
                  cache_spec, heads_spec, heads_spec, heads_spec,
                  pl.BlockSpec((None, None, past, IDX_DIM), lambda b, kj: (l, b, 0, 0)),
                  new_spec, new_spec, new_spec, new_spec,
                  pl.BlockSpec((None, LANES, IDX_DIM), lambda b, kj: (b, 0, 0)),
                  pl.BlockSpec((4, DH_A), lambda b, kj: (0, 0)),
                  pl.BlockSpec((1, DV_A), lambda b, kj: (0, 0))],
        out_specs=[row_spec, row_spec],
        out_shape=[jax.ShapeDtypeStruct((bsz, ts, SEC), jnp.float32),
                   jax.ShapeDtypeStruct((bsz, ts, SEC), jnp.float32)],
        scratch_shapes=[pltpu.VMEM((ra, SEC), jnp.bfloat16),
                        pltpu.VMEM((rb, SEC), jnp.bfloat16),
                        pltpu.VMEM((rb, width), jnp.float32),
                        pltpu.VMEM((ts, width), jnp.float32),
                        pltpu.VMEM((ra, 1), jnp.float32),
                        pltpu.VMEM((ra, 1), jnp.float32),
                        pltpu.VMEM((ra, SEC), jnp.float32),
                        pltpu.VMEM((rb, 1), jnp.float32),
                        pltpu.VMEM((rb, 1), jnp.float32),
                        pltpu.VMEM((rb, SEC), jnp.float32)],
        compiler_params=_params(("parallel", "arbitrary")),
        name="sample_attn",
    )(qa, qb, qi, wi, ckd, cvd, ckb, cvb, cki, *news, lam_p, sg)


def _silu(x):
    return x * jax.nn.sigmoid(x)


def _merge_kernel(oa_ref, ga_ref, ob_ref, gb_ref, mg_ref, wa_ref, wb_ref, u_ref):
    d = u_ref.shape[-1]
    ya = jnp.dot((oa_ref[...] * _silu(ga_ref[...])).astype(jnp.bfloat16), wa_ref[...],
                 preferred_element_type=jnp.float32)
    yb = jnp.dot((ob_ref[...] * _silu(gb_ref[...])).astype(jnp.bfloat16), wb_ref[...],
                 preferred_element_type=jnp.float32)
    mg = mg_ref[...]
    u = jax.nn.sigmoid(mg[:, :d]) * ya + jax.nn.sigmoid(mg[:, d:]) * yb
    u_ref[...] = u.astype(u_ref.dtype)


def _resid_kernel(x_ref, u_ref, wo_ref, g_ref, h_ref, hn_ref):
    h = x_ref[...] + jnp.dot(u_ref[...], wo_ref[...], preferred_element_type=jnp.float32)
    h_ref[...] = h
    ms = jnp.mean(h * h, axis=-1, keepdims=True)
    hn_ref[...] = (h * lax.rsqrt(ms + EPS) * g_ref[...]).astype(hn_ref.dtype)


def _ple_kernel(h_ref, hn_ref, p_ref, wg_ref, wp_ref, o_ref):
    gate = jax.nn.sigmoid(jnp.dot(hn_ref[...], wg_ref[...], preferred_element_type=jnp.float32))
    e = jnp.dot(p_ref[...].astype(jnp.bfloat16), wp_ref[...], preferred_element_type=jnp.float32)
    o_ref[...] = h_ref[...] + gate * e


def _layer_output(x, p, oa, ga, ob, gb, mg, w_ba, w_bb, w_out, ple_g, w_pg, w_ple, tm):
    r, d = x.shape
    wa = oa.shape[1]
    dp = p.shape[1]
    rows = lambda n: pl.BlockSpec((tm, n), lambda i: (i, 0))
    full = lambda a, b: pl.BlockSpec((a, b), lambda i: (0, 0))
    u = pl.pallas_call(
        _merge_kernel,
        grid=(r // tm,),
        in_specs=[rows(wa), rows(wa), rows(wa), rows(wa), rows(2 * d), full(wa, d), full(wa, d)],
        out_specs=rows(d),
        out_shape=jax.ShapeDtypeStruct((r, d), jnp.bfloat16),
        compiler_params=_params(("parallel",)),
        name="merge",
    )(oa, ga, ob, gb, mg, w_ba, w_bb)
    h, hn = pl.pallas_call(
        _resid_kernel,
        grid=(r // tm,),
        in_specs=[rows(d), rows(d), full(d, d), full(1, d)],
        out_specs=[rows(d), rows(d)],
        out_shape=[jax.ShapeDtypeStruct((r, d), jnp.float32), jax.ShapeDtypeStruct((r, d), jnp.bfloat16)],
        compiler_params=_params(("parallel",)),
        name="resid",
    )(x, u, w_out, ple_g)
    return pl.pallas_call(
        _ple_kernel,
        grid=(r // tm,),
        in_specs=[rows(d), rows(d), rows(dp), full(d, d), full(dp, d)],
        out_specs=rows(d),
        out_shape=jax.ShapeDtypeStruct((r, d), jnp.float32),
        compiler_params=_params(("parallel",)),
        name="ple",
    )(h, hn, p, w_pg, w_ple)


def _tile(n, pref):
    t = min(n, pref)
    while n % t:
        t //= 2
    return t


def _layer_inputs(x, pos, lw, tm):
    xn = _rmsnorm(x, lw["ln_g"], tm)
    t64 = _rope_tables(pos, DH_A)
    t128 = _rope_tables(pos, DH_B)
    none = t64
    one = jnp.ones((1, LANES), jnp.float32)
    w = lw["w_sec"]
    pj = functools.partial(_proj, xn, tm=tm)
    (qa16,) = pj(w["qa"], lw["qn_a"], t64, norm=DH_A, rope=DH_A, scale=DH_A ** -0.5 * LOG2E, kinds=["hm16"])
    ka, ka16 = pj(w["ka"], lw["kn_a"], t64, norm=DH_A, rope=DH_A, scale=1.0, kinds=["f32", "hm16"])
    va, va16 = pj(w["va"], one, none, norm=0, rope=0, scale=1.0, kinds=["f32", "hm16"])
    (ga,) = pj(w["ga"], one, none, norm=0, rope=0, scale=1.0, kinds=["f32"])
    (qb16,) = pj(w["qb"], lw["qn_b"], t128, norm=DH_B, rope=DH_B, scale=DH_B ** -0.5 * LOG2E, kinds=["hm16"])
    kb, kb16 = pj(w["kb"], lw["kn_b"], t128, norm=DH_B, rope=DH_B, scale=1.0, kinds=["f32", "hm16"])
    vb, vb16 = pj(w["vb"], one, none, norm=0, rope=0, scale=1.0, kinds=["f32", "hm16"])
    (gb,) = pj(w["gb"], one, none, norm=0, rope=0, scale=1.0, kinds=["f32"])
    (qi16,) = pj(w["qi"], one, t64, norm=0, rope=DH_A, scale=IDX_DIM ** -0.5, kinds=["hm16"])
    (mg,) = pj(w["mg"], one, none, norm=0, rope=0, scale=1.0, kinds=["f32"])
    small, ki2 = _small_proj(xn, w["small"], lw["kn_i"], t64, tm)
    ki = small[:, :IDX_DIM]
    wi = small[:, IDX_DIM:IDX_DIM + IDX_HEADS]
    return dict(qa16=qa16, ka=ka, ka16=ka16, va=va, va16=va16, ga=ga, qb16=qb16, kb=kb, kb16=kb16,
                vb=vb, vb16=vb16, gb=gb, qi16=qi16, mg=mg, ki=ki, ki2=ki2, wi=wi)


def _row_major(hm):
    g, r, n = hm.shape
    return jnp.transpose(hm, (1, 0, 2)).reshape(r, g * n)


def _pad_new(a, bsz, ts):
    a = a.reshape(bsz, ts, a.shape[-1]).astype(jnp.bfloat16)
    return jnp.pad(a, ((0, 0), (0, LANES - ts), (0, 0)))


def kernel(x_prompt, x_sample, p_prompt, p_sample, cache_diff_k, cache_diff_v, cache_dsa_k, cache_dsa_v, cache_idx_k, ln_g, w_in, q_norm_a, k_norm_a, lam_q1, lam_k1, lam_q2, lam_k2, subln_a, q_norm_b, k_norm_b, k_norm_idx, w_branch_a, w_branch_b, w_out, ple_norm, w_ple_gate, w_ple):
    depth = w_in.shape[0]
    bp, t_p, d_model = x_prompt.shape
    bsz, t_s, _ = x_sample.shape
    past = cache_diff_k.shape[2]
    assert bp == 1 and t_s <= LANES and t_p % LANES == 0 and past % LANES == 0
    topk_p = min(TOPK_MAX, t_p // 4)
    topk_s = min(TOPK_MAX, (past + t_s) // 4)
    pos_p = jnp.arange(t_p)
    pos_s = jnp.tile(past + jnp.arange(t_s), bsz)
    r_s = bsz * t_s

    tm_p = _tile(t_p, 512)
    tm_s = _tile(r_s, 512)
    tq_a, tk_a = _tile(t_p, 1024), _tile(t_p, 1024)
    tq_i, tk_i = _tile(t_p, 128), _tile(t_p, 1024)
    tq_b, tk_b = _tile(t_p, 512), _tile(t_p, 1024)
    tk_s = _tile(past, 1024)
    tm_o = _tile(t_p, 256)
    tm_os = _tile(r_s, 256)

    by_head = lambda c: c.reshape(depth, bsz, past * GROUPS, LANES)
    caches = (cache_diff_k.astype(jnp.bfloat16).reshape(depth, bsz, past, SEC), by_head(cache_diff_v),
              by_head(cache_dsa_k), by_head(cache_dsa_v), cache_idx_k)

    sizes = (SEC,) * 9 + (IDX_DIM, IDX_HEADS, 2 * d_model)
    names = ("qa", "ka", "va", "ga", "qb", "kb", "vb", "gb", "qi", "ki", "wi", "mg")
    offs = [0]
    for s in sizes:
        offs.append(offs[-1] + s)

    hp = x_prompt.reshape(t_p, d_model)
    hs = x_sample.reshape(r_s, d_model)
    outs_p = [[] for _ in range(5)]
    outs_s = [[] for _ in range(5)]
    for l in range(depth):
        lam_init = 0.8 - 0.6 * math.exp(-0.3 * l)
        wl = w_in[l].astype(jnp.bfloat16)
        w_sec = {n: wl[:, offs[i]:offs[i + 1]] for i, n in enumerate(names)}
        w_sec["small"] = jnp.pad(jnp.concatenate([w_sec.pop("ki"), w_sec.pop("wi")], axis=1),
                                 ((0, 0), (0, LANES - IDX_DIM - IDX_HEADS)))
        lane_gain = lambda g: jnp.tile(g.astype(jnp.float32), LANES // g.shape[0]).reshape(1, LANES)
        lw = dict(ln_g=ln_g[l], w_sec=w_sec, qn_a=lane_gain(q_norm_a[l]), kn_a=lane_gain(k_norm_a[l]),
                  qn_b=lane_gain(q_norm_b[l]), kn_b=lane_gain(k_norm_b[l]), kn_i=lane_gain(k_norm_idx[l]))
        ok_a = _unshifted_ok(q_norm_a[l], k_norm_a[l], DH_A, DH_A ** -0.5)
        ok_b = _unshifted_ok(q_norm_b[l], k_norm_b[l], DH_B, DH_B ** -0.5)
        lam_p = jnp.stack([lam_q1[l], lam_k1[l], lam_q2[l], lam_k2[l]]).astype(jnp.float32)
        sg = subln_a[l].reshape(1, DV_A).astype(jnp.float32)
        w_o = (w_branch_a[l].astype(jnp.bfloat16), w_branch_b[l].astype(jnp.bfloat16),
               w_out[l].astype(jnp.bfloat16), ple_norm[l].reshape(1, d_model),
               w_ple_gate[l].astype(jnp.bfloat16), w_ple[l].astype(jnp.bfloat16))

        a = _layer_inputs(hp, pos_p, lw, tm_p)
        oa = _diff_attn(a["qa16"], a["ka16"], a["va16"], ok_a, lam_p, sg, lam_init, tq_a, tk_a)
        bias = _index_select(a["qi16"], a["wi"], a["ki2"], topk_p, tq_i, tk_i)
        ob = _dsa_attn(a["qb16"], a["kb16"], a["vb16"], bias, ok_b, tq_b, tk_b)
        hp = _layer_output(hp, p_prompt[l].reshape(t_p, -1), oa, a["ga"], ob, a["gb"], a["mg"], *w_o, tm_o)
        for lst, key in zip(outs_p, ("ka", "va", "kb", "vb", "ki")):
            lst.append(a[key])

        s = _layer_inputs(hs, pos_s, lw, tm_s)
        qa_s = _row_major(s["qa16"]).reshape(bsz, t_s, SEC)
        qb_s = _row_major(s["qb16"]).reshape(bsz, t_s, SEC)
        qi_s = _row_major(s["qi16"]).reshape(bsz, t_s, IDX_HEADS, IDX_DIM)
        qi_s = jnp.transpose(qi_s, (0, 2, 1, 3)).reshape(bsz, IDX_HEADS * t_s, IDX_DIM)
        wi_s = jnp.transpose(s["wi"].reshape(bsz, t_s, IDX_HEADS), (0, 2, 1)).reshape(bsz, IDX_HEADS * t_s, 1)
        news = tuple(_pad_new(s[k], bsz, t_s) for k in ("ka", "va", "kb", "vb", "ki"))
        oa_s, ob_s = _sample_attn(l, qa_s, qb_s, qi_s, wi_s, caches, news, lam_p, sg, lam_init, topk_s, tk_s)
        hs = _layer_output(hs, p_sample[l].reshape(r_s, -1), oa_s.reshape(r_s, SEC), s["ga"],
                           ob_s.reshape(r_s, SEC), s["gb"], s["mg"], *w_o, tm_os)
        for lst, key in zip(outs_s, ("ka", "va", "kb", "vb", "ki")):
            lst.append(s[key])

    def stack(lst, shape):
        return jnp.stack(lst).reshape((depth,) + shape)

    return (hp.reshape(bp, t_p, d_model), hs.reshape(bsz, t_s, d_model),
            stack(outs_p[0], (bp, t_p, H_A, 2, DH_A)), stack(outs_p[1], (bp, t_p, H_A, DV_A)),
            stack(outs_p[2], (bp, t_p, H_B, DH_B)), stack(outs_p[3], (bp, t_p, H_B, DH_B)),
            stack(outs_p[4], (bp, t_p, IDX_DIM)),
            stack(outs_s[0], (bsz, t_s, H_A, 2, DH_A)), stack(outs_s[1], (bsz, t_s, H_A, DV_A)),
            stack(outs_s[2], (bsz, t_s, H_B, DH_B)), stack(outs_s[3], (bsz, t_s, H_B, DH_B)),
            stack(outs_s[4], (bsz, t_s, IDX_DIM)))
```

```python
import functools
import math

import jax
import jax.numpy as jnp
import numpy as np
from jax import lax
from jax.experimental import pallas as pl
from jax.experimental.pallas import tpu as pltpu

CHUNK = 64
ROPE_THETA = 500000.0
EPS = 1e-6
H_A = 8
DH_A = 64
DV_A = 128
H_B = 8
DH_B = 128
IDX_HEADS = 16
IDX_DIM = 64
TOPK_MAX = 256
LANES = 128
GROUPS = 8
SEC = GROUPS * LANES
ROW_GROUP = 128
NEG = -1e30
LOG2E = 1.4426950408889634
MAX_UNSHIFTED = 60.0
MOST_NEG = -3.0e38
VMEM_LIMIT = 56 * 1024 * 1024

_NT = (((1,), (1,)), ((), ()))


def _cdiv(a, b):
    return (a + b - 1) // b


def _params(sem, vmem=VMEM_LIMIT):
    return pltpu.CompilerParams(dimension_semantics=sem, vmem_limit_bytes=vmem)


def _rmsnorm_kernel(x_ref, g_ref, o_ref):
    x = x_ref[...]
    ms = jnp.mean(x * x, axis=-1, keepdims=True)
    o_ref[...] = (x * lax.rsqrt(ms + EPS) * g_ref[...]).astype(o_ref.dtype)


def _rmsnorm(x, g, tm):
    r, d = x.shape
    return pl.pallas_call(
        _rmsnorm_kernel,
        grid=(r // tm,),
        in_specs=[pl.BlockSpec((tm, d), lambda i: (i, 0)),
                  pl.BlockSpec((1, d), lambda i: (0, 0))],
        out_specs=pl.BlockSpec((tm, d), lambda i: (i, 0)),
        out_shape=jax.ShapeDtypeStruct((r, d), jnp.bfloat16),
        compiler_params=_params(("parallel",)),
        name="rmsnorm",
    )(x, g.reshape(1, d))


def _head_norm(z, g, d):
    sq = z * z
    if d == LANES:
        r = lax.rsqrt(jnp.sum(sq, axis=-1, keepdims=True) * (1.0 / d) + EPS)
    else:
        lo = lax.broadcasted_iota(jnp.int32, z.shape, 1) < d
        s_lo = jnp.sum(jnp.where(lo, sq, 0.0), axis=-1, keepdims=True)
        s_hi = jnp.sum(jnp.where(lo, 0.0, sq), axis=-1, keepdims=True)
        r = jnp.where(lo, lax.rsqrt(s_lo * (1.0 / d) + EPS), lax.rsqrt(s_hi * (1.0 / d) + EPS))
    return z * r * g


def _rope(z, c, a, b, half):
    return z * c + pltpu.roll(z, LANES - half, 1) * a + pltpu.roll(z, half, 1) * b


def _proj_kernel(xn_ref, w_ref, g_ref, rc_ref, ra_ref, rb_ref, *out_refs, norm, rope, scale, kinds):
    tm = xn_ref.shape[0]
    chunk = tm // 2 if (norm or rope) and tm % 512 == 0 else tm
    for r0 in range(0, tm, chunk):
        rows = slice(r0, r0 + chunk)
        z = jnp.dot(xn_ref[rows, :], w_ref[...], preferred_element_type=jnp.float32)
        for c in range(GROUPS):
            zc = z[:, c * LANES:(c + 1) * LANES]
            if norm:
                zc = _head_norm(zc, g_ref[...], norm)
            if rope:
                zc = _rope(zc, rc_ref[rows, :], ra_ref[rows, :], rb_ref[rows, :], rope // 8)
            for kind, o_ref in zip(kinds, out_refs):
                if kind == "f32":
                    o_ref[rows, c * LANES:(c + 1) * LANES] = zc
                else:
                    o_ref[c, rows, :] = (zc * scale).astype(o_ref.dtype)


def _proj(xn, w, gain, tables, *, norm, rope, scale, kinds, tm):
    r, d = xn.shape
    nj = w.shape[1] // SEC
    rc, ra, rb = tables
    out_shape, out_specs = [], []
    for kind in kinds:
        if kind == "f32":
            out_shape.append(jax.ShapeDtypeStruct((r, nj * SEC), jnp.float32))
            out_specs.append(pl.BlockSpec((tm, SEC), lambda j, i: (i, j)))
        else:
            assert nj == 1
            out_shape.append(jax.ShapeDtypeStruct((GROUPS, r, LANES), jnp.bfloat16))
            out_specs.append(pl.BlockSpec((GROUPS, tm, LANES), lambda j, i: (0, i, 0)))
    row = pl.BlockSpec((tm, LANES), lambda j, i: (i, 0))
    outs = pl.pallas_call(
        functools.partial(_proj_kernel, norm=norm, rope=rope, scale=scale, kinds=tuple(kinds)),
        grid=(nj, r // tm),
        in_specs=[pl.BlockSpec((tm, d), lambda j, i: (i, 0)),
                  pl.BlockSpec((d, SEC), lambda j, i: (0, j)),
                  pl.BlockSpec((1, LANES), lambda j, i: (0, 0)),
                  row, row, row],
        out_specs=out_specs,
        out_shape=out_shape,
        compiler_params=_params(("parallel", "parallel")),
        name="proj",
    )(xn, w, gain, rc, ra, rb)
    return outs


def _small_proj_kernel(xn_ref, w_ref, g_ref, rc_ref, ra_ref, rb_ref, f_ref, k2_ref, *, wscale):
    z = jnp.dot(xn_ref[...], w_ref[...], preferred_element_type=jnp.float32)
    lo = lax.broadcasted_iota(jnp.int32, z.shape, 1) < IDX_DIM
    ms = jnp.sum(jnp.where(lo, z * z, 0.0), axis=-1, keepdims=True) * (1.0 / IDX_DIM)
    zn = z * lax.rsqrt(ms + EPS) * g_ref[...]
    zr = _rope(zn, rc_ref[...], ra_ref[...], rb_ref[...], IDX_DIM // 8)
    f_ref[...] = jnp.where(lo, zr, z * wscale)
    k2_ref[...] = jnp.where(lo, zr, pltpu.roll(zr, IDX_DIM, 1)).astype(k2_ref.dtype)


def _small_proj(xn, w, gain, tables, tm):
    r, d = xn.shape
    rc, ra, rb = tables
    row = pl.BlockSpec((tm, LANES), lambda i: (i, 0))
    return pl.pallas_call(
        functools.partial(_small_proj_kernel, wscale=IDX_HEADS ** -0.5),
        grid=(r // tm,),
        in_specs=[pl.BlockSpec((tm, d), lambda i: (i, 0)),
                  pl.BlockSpec((d, LANES), lambda i: (0, 0)),
                  pl.BlockSpec((1, LANES), lambda i: (0, 0)),
                  row, row, row],
        out_specs=[row, row],
        out_shape=[jax.ShapeDtypeStruct((r, LANES), jnp.float32),
                   jax.ShapeDtypeStruct((r, LANES), jnp.bfloat16)],
        compiler_params=_params(("parallel",)),
        name="small_proj",
    )(xn, w, gain, rc, ra, rb)


def _rope_tables(pos, d):
    r = d // 4
    h = r // 2
    n = pos.shape[0]
    inv = ROPE_THETA ** (-jnp.arange(0, r, 2, dtype=jnp.float32) / r)
    ang = pos.astype(jnp.float32)[:, None] * inv[None, :]
    cos, sin = jnp.cos(ang), jnp.sin(ang)
    zh = jnp.zeros((n, h), jnp.float32)
    zr = jnp.zeros((n, d - r), jnp.float32)
    c = jnp.concatenate([cos, cos, jnp.ones((n, d - r), jnp.float32)], axis=-1)
    a = jnp.concatenate([-sin, zh, zr], axis=-1)
    b = jnp.concatenate([zh, sin, zr], axis=-1)
    rep = LANES // d
    return tuple(jnp.tile(t, (1, rep)) for t in (c, a, b))


def _online_update(s, v, m_ref, l_ref, acc_ref):
    m_prev = m_ref[...]
    m_new = jnp.maximum(m_prev, jnp.max(s, axis=-1, keepdims=True))
    alpha = jnp.exp2(m_prev - m_new)
    p = jnp.exp2(s - m_new)
    l_ref[...] = alpha * l_ref[...] + jnp.sum(p, axis=-1, keepdims=True)
    acc_ref[...] = alpha * acc_ref[...] + jnp.dot(p.astype(v.dtype), v, preferred_element_type=jnp.float32)
    m_ref[...] = m_new


def _causal_steps(t, tq, tk):
    qs, ks = [], []
    for qi in range(t // tq):
        nk = _cdiv((qi + 1) * tq, tk)
        qs += [qi] * nk
        ks += list(range(nk))
    return np.asarray(qs, np.int32), np.asarray(ks, np.int32)


def _unshifted_ok(gq, gk, d, scale):
    bound = jnp.max(jnp.abs(gq)) * jnp.max(jnp.abs(gk)) * (d * scale * (1.0 + 2.0 ** -6))
    return (bound <= MAX_UNSHIFTED).astype(jnp.int32).reshape(1)


def _lambda(lam_ref, lam_init):
    lp = lam_ref[...]
    s1 = jnp.sum(lp[0:1] * lp[1:2], axis=-1, keepdims=True)
    s2 = jnp.sum(lp[2:3] * lp[3:4], axis=-1, keepdims=True)
    return jnp.exp(s1) - jnp.exp(s2) + lam_init


def _diff_finish(o0, o1, lam, sg, lam_init):
    o = o0 - lam * o1
    ms = jnp.mean(o * o, axis=-1, keepdims=True)
    return o * lax.rsqrt(ms + EPS) * sg * (1.0 - lam_init)


def _chunk_end(pos):
    return (pos // CHUNK + 1) * CHUNK - 1


def _diff_attn_kernel(qt_ref, kt_ref, ok_ref, q_ref, k_ref, v_ref, lam_ref, sg_ref, o_ref,
                      qs_ref, v1_ref, acc_ref, m_ref, l_ref, *, tq, tk, lam_init):
    step_id = pl.program_id(1)
    qi = qt_ref[step_id]
    kj = kt_ref[step_id]
    nk = _cdiv((qi + 1) * tq, tk)
    unshifted = ok_ref[0] == 1
    full = (kj + 1) * tk <= qi * tq + CHUNK
    rows2 = 2 * tq

    @pl.when(kj == 0)
    def _():
        q = q_ref[...]
        lo = lax.broadcasted_iota(jnp.int32, q.shape, 1) < DH_A
        zero = jnp.zeros_like(q)
        qs_ref[0:tq, :] = jnp.where(lo, q, zero)
        qs_ref[tq:rows2, :] = jnp.where(lo, zero, q)
        v1_ref[:, DV_A:] = jnp.ones((tk, LANES), v1_ref.dtype)
        m_ref[...] = jnp.full(m_ref.shape, NEG, jnp.float32)
        l_ref[...] = jnp.zeros(l_ref.shape, jnp.float32)
        acc_ref[...] = jnp.zeros(acc_ref.shape, jnp.float32)

    v1_ref[:, :DV_A] = v_ref[...]

    def visible(r0, nrows):
        row = r0 + lax.broadcasted_iota(jnp.int32, (nrows, tk), 0)
        row = jnp.where(row >= tq, row - tq, row)
        kpos = kj * tk + lax.broadcasted_iota(jnp.int32, (nrows, tk), 1)
        return kpos <= _chunk_end(qi * tq + row)

    def fast_step(masked):
        k = k_ref[...]
        v1 = v1_ref[...]
        for r0 in range(0, rows2, ROW_GROUP):
            rows = slice(r0, r0 + ROW_GROUP)
            s = lax.dot_general(qs_ref[rows, :], k, _NT, preferred_element_type=jnp.float32)
            if masked:
                s = jnp.where(visible(r0, ROW_GROUP), s, NEG)
            p = jnp.exp2(s).astype(v1.dtype)
            acc_ref[rows, :] += jnp.dot(p, v1, preferred_element_type=jnp.float32)

    def exact_step(masked):
        s = lax.dot_general(qs_ref[...], k_ref[...], _NT, preferred_element_type=jnp.float32)
        if masked:
            s = jnp.where(visible(0, rows2), s, NEG)
        _online_update(s, v_ref[...], m_ref, l_ref, acc_ref.at[:, :DV_A])

    for fast, step in ((True, fast_step), (False, exact_step)):
        mode = unshifted if fast else jnp.logical_not(unshifted)

        @pl.when(jnp.logical_and(mode, full))
        def _():
            step(False)

        @pl.when(jnp.logical_and(mode, jnp.logical_not(full)))
        def _():
            step(True)

    @pl.when(kj == nk - 1)
    def _():
        @pl.when(jnp.logical_not(unshifted))
        def _():
            acc_ref[:, DV_A:] = jnp.broadcast_to(l_ref[...], (rows2, LANES))

        o = acc_ref[:, :DV_A] / acc_ref[:, DV_A:]
        o_ref[...] = _diff_finish(o[0:tq], o[tq:rows2], _lambda(lam_ref, lam_init), sg_ref[...], lam_init)


def _diff_attn(q, k, v, ok, lam_p, sg, lam_init, tq, tk):
    _, t, _ = q.shape
    qt, kt = _causal_steps(t, tq, tk)
    kv_spec = pl.BlockSpec((None, tk, LANES), lambda h, s, qt, kt, ok: (h, kt[s], 0))
    const = lambda h, s, qt, kt, ok: (0, 0)
    grid_spec = pltpu.PrefetchScalarGridSpec(
        num_scalar_prefetch=3,
        grid=(H_A, len(qt)),
        in_specs=[pl.BlockSpec((None, tq, LANES), lambda h, s, qt, kt, ok: (h, qt[s], 0)),
                  kv_spec, kv_spec,
                  pl.BlockSpec((4, DH_A), const),
                  pl.BlockSpec((1, DV_A), const)],
        out_specs=pl.BlockSpec((tq, LANES), lambda h, s, qt, kt, ok: (qt[s], h)),
        scratch_shapes=[pltpu.VMEM((2 * tq, LANES), jnp.bfloat16),
                        pltpu.VMEM((tk, DV_A + LANES), jnp.bfloat16),
                        pltpu.VMEM((2 * tq, DV_A + LANES), jnp.float32),
                        pltpu.VMEM((2 * tq, 1), jnp.float32),
                        pltpu.VMEM((2 * tq, 1), jnp.float32)])
    return pl.pallas_call(
        functools.partial(_diff_attn_kernel, tq=tq, tk=tk, lam_init=lam_init),
        grid_spec=grid_spec,
        out_shape=jax.ShapeDtypeStruct((t, H_A * DV_A), jnp.float32),
        compiler_params=_params(("parallel", "arbitrary")),
        name="diff_attn",
    )(jnp.asarray(qt), jnp.asarray(kt), ok, q, k, v, lam_p, sg)


def _topk_threshold(count, lo, hi, all_sel, topk):
    kf = float(topk)
    c_lo = count(lo)
    c_hi = jnp.zeros_like(c_lo)
    done = jnp.where(jnp.logical_or(all_sel, c_lo == kf), 1.0, 0.0)

    def cond(st):
        return jnp.logical_and(st[0] < 120, st[6] < 0.5)

    def body(st):
        it, lo, hi, c_lo, c_hi, done, _ = st
        mid = 0.5 * lo + 0.5 * hi
        stuck = jnp.logical_or(mid <= lo, mid >= hi)
        frac = (jnp.log(c_lo) - math.log(kf)) / (jnp.log(c_lo) - jnp.log(jnp.maximum(c_hi, 0.5)))
        t = lo + (hi - lo) * frac
        use = jnp.logical_and(lax.rem(it, 3) != 2, jnp.logical_and(t > lo, t < hi))
        t = jnp.where(use, t, mid)
        c = count(t)
        live = jnp.logical_and(done < 0.5, jnp.logical_not(stuck))
        up = jnp.logical_and(live, c >= kf)
        dn = jnp.logical_and(live, c < kf)
        lo, c_lo = jnp.where(up, t, lo), jnp.where(up, c, c_lo)
        hi, c_hi = jnp.where(dn, t, hi), jnp.where(dn, c, c_hi)
        done = jnp.where(jnp.logical_or(c_lo == kf, stuck), 1.0, done)
        return it + 1, lo, hi, c_lo, c_hi, done, jnp.min(done)

    st = (jnp.int32(0), lo, hi, c_lo, c_hi, done, jnp.min(done))
    _, lo, hi, c_lo, c_hi, _, _ = lax.while_loop(cond, body, st)
    thr = jnp.where(all_sel, MOST_NEG, lo)
    tied = jnp.logical_and(jnp.logical_not(all_sel), c_lo > kf)
    return thr, tied, c_hi


def _above(x):
    return x + jnp.abs(x) * 2.0 ** -20 + 1e-30


def _index_kernel(q_ref, w_ref, k_ref, bias_ref, qs_ref, sc_ref, lo_ref, *, tq, tk, t, topk):
    qi = pl.program_id(0)
    nk = _cdiv((qi + 1) * tq, tk)
    nsub = tk // LANES

    for p in range(GROUPS):
        q = q_ref[p]
        lo = lax.broadcasted_iota(jnp.int32, q.shape, 1) < IDX_DIM
        zero = jnp.zeros_like(q)
        qs_ref[(2 * p) * tq:(2 * p + 1) * tq, :] = jnp.where(lo, q, zero)
        qs_ref[(2 * p + 1) * tq:(2 * p + 2) * tq, :] = jnp.where(lo, zero, q)
    w = w_ref[...]

    def score_tile(kj, carry):
        keys = k_ref[pl.ds(pl.multiple_of(kj * tk, tk), tk), :]
        logits = lax.dot_general(qs_ref[...], keys, _NT, preferred_element_type=jnp.float32)
        score = None
        for h in range(IDX_HEADS):
            term = jnp.maximum(logits[h * tq:(h + 1) * tq], 0.0) * w[:, h:h + 1]
            score = term if score is None else score + term
        qpos = qi * tq + lax.broadcasted_iota(jnp.int32, score.shape, 0)
        kpos = kj * tk + lax.broadcasted_iota(jnp.int32, score.shape, 1)
        score = jnp.where(kpos <= _chunk_end(qpos), score, -jnp.inf)
        for c in range(nsub):
            sc_ref[kj * nsub + c] = score[:, c * LANES:(c + 1) * LANES]
        return carry

    lax.fori_loop(0, nk, score_tile, 0)

    def select():
        lane = lax.broadcasted_iota(jnp.int32, (tq, LANES), 1)
        cend = _chunk_end(qi * tq + lax.broadcasted_iota(jnp.int32, (tq, LANES), 0))

        def rows(x):
            return jnp.broadcast_to(x, (tq, LANES))

        def sweep(fn, init):
            def body(j, acc):
                for c in range(nsub):
                    acc = fn(acc, sc_ref[j * nsub + c], j * tk + c * LANES + lane, c % 2)
                return acc
            acc = lax.fori_loop(0, nk // 2, lambda j, a: body(2 * j + 1, body(2 * j, a)), init)
            return lax.fori_loop(nk - nk % 2, nk, body, acc)

        def count(pred):
            acc = sweep(lambda a, s, idx, par: jnp.where(pred(s, idx), a + 1.0, a),
                        jnp.zeros((tq, LANES), jnp.float32))
            return rows(jnp.sum(acc, axis=1, keepdims=True))

        ninf = jnp.full((tq, LANES), -jnp.inf, jnp.float32)

        def group_max(acc, s, idx, par):
            return (jnp.maximum(acc[0], s), acc[1]) if par == 0 else (acc[0], jnp.maximum(acc[1], s))

        m0, m1 = sweep(group_max, (ninf, ninf))
        lo = rows(jnp.min(jnp.minimum(m0, m1), axis=1, keepdims=True))
        row_max = rows(jnp.max(jnp.maximum(m0, m1), axis=1, keepdims=True))
        all_sel = cend + 1 <= topk
        lo_ref[...] = lo

        @pl.when(jnp.max(jnp.where(jnp.logical_and(lo == -jnp.inf, jnp.logical_not(all_sel)), 1.0, 0.0)) > 0.5)
        def _():
            low = sweep(lambda a, s, idx, par: jnp.minimum(a, jnp.where(idx <= cend, s, jnp.inf)),
                        jnp.full((tq, LANES), jnp.inf, jnp.float32))
            lo_ref[...] = rows(jnp.min(low, axis=1, keepdims=True))

        lo = jnp.where(all_sel, 0.0, lo_ref[...])
        thr, tied, n_gt = _topk_threshold(lambda t: count(lambda s, idx: s >= t), lo,
                                          jnp.where(all_sel, 1.0, _above(row_max)), all_sel, topk)
        need = jnp.where(tied, float(topk) - n_gt, jnp.inf)

        def write(select):
            def body(j, carry):
                for c in range(nsub):
                    off = pl.multiple_of(j * tk + c * LANES, LANES)
                    sel = select(sc_ref[j * nsub + c], off + lane)
                    bias_ref[:, pl.ds(off, LANES)] = jnp.where(sel, 0.0, NEG).astype(bias_ref.dtype)
                return carry
            lax.fori_loop(0, nk, body, 0)

        any_tied = jnp.max(jnp.where(tied, 1.0, 0.0)) > 0.5

        @pl.when(jnp.logical_not(any_tied))
        def _():
            write(lambda s, idx: s >= thr)

        @pl.when(any_tied)
        def _():
            nbits = max(1, (t - 1).bit_length())

            def pos_body(b, pos):
                cand = pos | lax.shift_left(jnp.int32(1), jnp.int32(nbits - 1) - b)
                g = count(lambda s, idx: jnp.logical_and(s == thr, idx < cand))
                return jnp.where(g < need, cand, pos)

            last = lax.fori_loop(0, nbits, pos_body, jnp.zeros((tq, LANES), jnp.int32))
            write(lambda s, idx: jnp.logical_or(s > thr, jnp.logical_and(s == thr, idx <= last)))

        def fill_body(j, carry):
            off = pl.multiple_of(j * tk, tk)
            bias_ref[:, pl.ds(off, tk)] = jnp.full((tq, tk), NEG, bias_ref.dtype)
            return carry

        lax.fori_loop(nk, t // tk, fill_body, 0)

    select()


def _index_select(qi16, wi, ki2, topk, tq, tk):
    _, t, _ = qi16.shape
    return pl.pallas_call(
        functools.partial(_index_kernel, tq=tq, tk=tk, t=t, topk=topk),
        grid=(t // tq,),
        in_specs=[pl.BlockSpec((GROUPS, tq, LANES), lambda i: (0, i, 0)),
                  pl.BlockSpec((tq, IDX_HEADS), lambda i: (i, 0)),
                  pl.BlockSpec((t, LANES), lambda i: (0, 0))],
        out_specs=pl.BlockSpec((tq, t), lambda i: (i, 0)),
        out_shape=jax.ShapeDtypeStruct((t, t), jnp.bfloat16),
        scratch_shapes=[pltpu.VMEM((IDX_HEADS * tq, LANES), jnp.bfloat16),
                        pltpu.VMEM((t // LANES, tq, LANES), jnp.float32),
                        pltpu.VMEM((tq, LANES), jnp.float32)],
        compiler_params=_params(("arbitrary",)),
        name="index_select",
    )(qi16, wi, ki2)


def _dsa_attn_kernel(qt_ref, kt_ref, ok_ref, q_ref, k_ref, v_ref, bias_ref, o_ref,
                     v1_ref, acc_ref, m_ref, l_ref, *, tq, tk):
    step_id = pl.program_id(0)
    qi = qt_ref[step_id]
    kj = kt_ref[step_id]
    nk = _cdiv((qi + 1) * tq, tk)
    unshifted = ok_ref[0] == 1

    @pl.when(kj == 0)
    def _():
        v1_ref[:, :, DH_B:] = jnp.ones((H_B, tk, LANES), v1_ref.dtype)
        m_ref[...] = jnp.full(m_ref.shape, NEG, jnp.float32)
        l_ref[...] = jnp.zeros(l_ref.shape, jnp.float32)
        acc_ref[...] = jnp.zeros(acc_ref.shape, jnp.float32)

    @pl.when(unshifted)
    def _():
        v1_ref[:, :, :DH_B] = v_ref[...]
        for r0 in range(0, tq, ROW_GROUP):
            rows = slice(r0, r0 + ROW_GROUP)
            bias = bias_ref[rows, :].astype(jnp.float32)
            for h in range(H_B):
                s = lax.dot_general(q_ref[h, rows, :], k_ref[h], _NT, preferred_element_type=jnp.float32) + bias
                p = jnp.exp2(s).astype(v1_ref.dtype)
                acc_ref[h, rows, :] += jnp.dot(p, v1_ref[h], preferred_element_type=jnp.float32)

    @pl.when(jnp.logical_not(unshifted))
    def _():
        bias = bias_ref[...].astype(jnp.float32)
        for h in range(H_B):
            s = lax.dot_general(q_ref[h], k_ref[h], _NT, preferred_element_type=jnp.float32) + bias
            _online_update(s, v_ref[h], m_ref.at[h], l_ref.at[h], acc_ref.at[h, :, :DH_B])

    @pl.when(kj == nk - 1)
    def _():
        @pl.when(jnp.logical_not(unshifted))
        def _():
            acc_ref[:, :, DH_B:] = jnp.broadcast_to(l_ref[...], (H_B, tq, LANES))

        for h in range(H_B):
            o_ref[:, h * LANES:(h + 1) * LANES] = acc_ref[h, :, :DH_B] / acc_ref[h, :, DH_B:]


def _dsa_attn(q, k, v, bias, ok, tq, tk):
    _, t, _ = q.shape
    qt, kt = _causal_steps(t, tq, tk)
    kv_spec = pl.BlockSpec((H_B, tk, LANES), lambda s, qt, kt, ok: (0, kt[s], 0))
    grid_spec = pltpu.PrefetchScalarGridSpec(
        num_scalar_prefetch=3,
        grid=(len(qt),),
        in_specs=[pl.BlockSpec((H_B, tq, LANES), lambda s, qt, kt, ok: (0, qt[s], 0)),
                  kv_spec, kv_spec,
                  pl.BlockSpec((tq, tk), lambda s, qt, kt, ok: (qt[s], kt[s]))],
        out_specs=pl.BlockSpec((tq, H_B * DH_B), lambda s, qt, kt, ok: (qt[s], 0)),
        scratch_shapes=[pltpu.VMEM((H_B, tk, DH_B + LANES), jnp.bfloat16),
                        pltpu.VMEM((H_B, tq, DH_B + LANES), jnp.float32),
                        pltpu.VMEM((H_B, tq, 1), jnp.float32),
                        pltpu.VMEM((H_B, tq, 1), jnp.float32)])
    return pl.pallas_call(
        functools.partial(_dsa_attn_kernel, tq=tq, tk=tk),
        grid_spec=grid_spec,
        out_shape=jax.ShapeDtypeStruct((t, H_B * DH_B), jnp.float32),
        compiler_params=_params(("arbitrary",)),
        name="dsa_attn",
    )(jnp.asarray(qt), jnp.asarray(kt), ok, q, k, v, bias)


def _sample_kernel(qa_ref, qb_ref, qi_ref, wi_ref, ckd_ref, cvd_ref, ckb_ref, cvb_ref, cki_ref,
                   nkd_ref, nvd_ref, nkb_ref, nvb_ref, nki_ref, lam_ref, sg_ref,
                   oa_ref, ob_ref,
                   qda_ref, qdb_ref, bias_ref, sc_ref, ma_ref, la_ref, acca_ref, mb_ref, lb_ref, accb_ref,
                   *, ts, past, tk, topk, lam_init):
    kj = pl.program_id(1)
    nkt = past // tk
    width = past + LANES
    ra = 2 * H_A * ts
    rb = H_B * ts

    def new_key_valid(rows):
        t_q = lax.broadcasted_iota(jnp.int32, (rows, LANES), 0) % ts
        lane = lax.broadcasted_iota(jnp.int32, (rows, LANES), 1)
        return jnp.logical_and(lane < ts, past + lane <= _chunk_end(past + t_q))

    @pl.when(kj == 0)
    def _():
        qa = qa_ref[...]
        grp = lax.broadcasted_iota(jnp.int32, qa.shape, 1) // DH_A
        for hc in range(2 * H_A):
            qda_ref[hc * ts:(hc + 1) * ts, :] = jnp.where(grp == hc, qa, jnp.zeros_like(qa))
        qb = qb_ref[...]
        grp = lax.broadcasted_iota(jnp.int32, qb.shape, 1) // DH_B
        for h in range(H_B):
            qdb_ref[h * ts:(h + 1) * ts, :] = jnp.where(grp == h, qb, jnp.zeros_like(qb))
        for ref in (ma_ref, mb_ref):
            ref[...] = jnp.full(ref.shape, NEG, jnp.float32)
        for ref in (la_ref, lb_ref, acca_ref, accb_ref):
            ref[...] = jnp.zeros(ref.shape, jnp.float32)

        qi = qi_ref[...]
        w = wi_ref[...]

        def scores(keys16):
            logits = lax.dot_general(qi, keys16, _NT, preferred_element_type=jnp.float32)
            score = None
            for h in range(IDX_HEADS):
                term = jnp.maximum(logits[h * ts:(h + 1) * ts], 0.0) * w[h * ts:(h + 1) * ts]
                score = term if score is None else score + term
            return score

        step = min(past, 1024)
        for c in range(past // step):
            keys16 = cki_ref[c * step:(c + 1) * step, :].astype(jnp.bfloat16)
            sc_ref[:, c * step:(c + 1) * step] = scores(keys16)
        sc_ref[:, past:width] = jnp.where(new_key_valid(ts), scores(nki_ref[...]), -jnp.inf)

        def rows(x):
            return jnp.broadcast_to(x, (ts, LANES))

        def wide(x):
            return x[:, 0:1]

        def count(pred):
            idx = lax.broadcasted_iota(jnp.int32, (ts, width), 1)
            return rows(jnp.sum(jnp.where(pred(sc_ref[...], idx), 1.0, 0.0), axis=1, keepdims=True))

        sc = sc_ref[...]
        m0 = m1 = jnp.full((ts, LANES), -jnp.inf, jnp.float32)
        for c in range(width // LANES):
            slab = sc[:, c * LANES:(c + 1) * LANES]
            m0, m1 = (jnp.maximum(m0, slab), m1) if c % 2 == 0 else (m0, jnp.maximum(m1, slab))
        lo = rows(jnp.min(jnp.minimum(m0, m1), axis=1, keepdims=True))
        row_min = rows(jnp.min(jnp.where(sc == -jnp.inf, jnp.inf, sc), axis=1, keepdims=True))
        lo = jnp.where(lo == -jnp.inf, row_min, lo)
        row_max = rows(jnp.max(jnp.maximum(m0, m1), axis=1, keepdims=True))
        n_valid = count(lambda s, idx: s > -jnp.inf)
        all_sel = n_valid <= float(topk)
        thr, tied, n_gt = _topk_threshold(lambda t: count(lambda s, idx: s >= wide(t)),
                                          jnp.where(all_sel, 0.0, lo), jnp.where(all_sel, 1.0, _above(row_max)),
                                          all_sel, topk)

        def write(select):
            s = sc_ref[...]
            sel = select(s, lax.broadcasted_iota(jnp.int32, (ts, width), 1))
            bias = jnp.where(sel, 0.0, NEG)
            for h in range(H_B):
                bias_ref[h * ts:(h + 1) * ts, :] = bias

        any_tied = jnp.max(jnp.where(tied, 1.0, 0.0)) > 0.5

        @pl.when(jnp.logical_not(any_tied))
        def _():
            write(lambda s, idx: s >= wide(thr))

        @pl.when(any_tied)
        def _():
            need = jnp.where(tied, float(topk) - n_gt, jnp.inf)
            nbits = max(1, (width - 1).bit_length())

            def pos_body(b, pos):
                cand = pos | lax.shift_left(jnp.int32(1), jnp.int32(nbits - 1) - b)
                g = count(lambda s, idx: jnp.logical_and(s == wide(thr), idx < wide(cand)))
                return jnp.where(g < need, cand, pos)

            last = lax.fori_loop(0, nbits, pos_body, jnp.zeros((ts, LANES), jnp.int32))
            write(lambda s, idx: jnp.logical_or(s > wide(thr), jnp.logical_and(s == wide(thr), idx <= wide(last))))

    def head_rows(ref):
        return jnp.concatenate([ref[pl.ds(h, tk, stride=GROUPS), :].astype(jnp.bfloat16) for h in range(GROUPS)],
                               axis=1)

    sa = lax.dot_general(qda_ref[...], ckd_ref[...], _NT, preferred_element_type=jnp.float32)
    _online_update(sa, head_rows(cvd_ref), ma_ref, la_ref, acca_ref)
    sb = lax.dot_general(qdb_ref[...], head_rows(ckb_ref), _NT, preferred_element_type=jnp.float32)
    sb = sb + bias_ref[:, pl.ds(pl.multiple_of(kj * tk, tk), tk)]
    _online_update(sb, head_rows(cvb_ref), mb_ref, lb_ref, accb_ref)

    @pl.when(kj == nkt - 1)
    def _():
        sa = lax.dot_general(qda_ref[...], nkd_ref[...], _NT, preferred_element_type=jnp.float32)
        sa = jnp.where(new_key_valid(ra), sa, NEG)
        _online_update(sa, nvd_ref[...], ma_ref, la_ref, acca_ref)
        sb = lax.dot_general(qdb_ref[...], nkb_ref[...], _NT, preferred_element_type=jnp.float32)
        sb = sb + bias_ref[:, past:width]
        _online_update(sb, nvb_ref[...], mb_ref, lb_ref, accb_ref)

        lam = _lambda(lam_ref, lam_init)
        oa = acca_ref[...] / la_ref[...]
        ob = accb_ref[...] / lb_ref[...]
        for h in range(H_A):
            cols = slice(h * DV_A, (h + 1) * DV_A)
            o0 = oa[(2 * h) * ts:(2 * h + 1) * ts, cols]
            o1 = oa[(2 * h + 1) * ts:(2 * h + 2) * ts, cols]
            oa_ref[:, cols] = _diff_finish(o0, o1, lam, sg_ref[...], lam_init)
        for h in range(H_B):
            cols = slice(h * DH_B, (h + 1) * DH_B)
            ob_ref[:, cols] = ob[h * ts:(h + 1) * ts, cols]


def _sample_attn(l, qa, qb, qi, wi, caches, news, lam_p, sg, lam_init, topk, tk):
    bsz, ts, _ = qa.shape
    ckd, cvd, ckb, cvb, cki = caches
    past = cki.shape[2]
    width = past + LANES
    ra, rb = 2 * H_A * ts, H_B * ts
    cache_spec = pl.BlockSpec((None, None, tk, SEC), lambda b, kj: (l, b, kj, 0))
    heads_spec = pl.BlockSpec((None, None, tk * GROUPS, LANES), lambda b, kj: (l, b, kj, 0))
    new_spec = pl.BlockSpec((None, LANES, SEC), lambda b, kj: (b, 0, 0))
    row_spec = pl.BlockSpec((None, ts, SEC), lambda b, kj: (b, 0, 0))
    return pl.pallas_call(
        functools.partial(_sample_kernel, ts=ts, past=past, tk=tk, topk=topk, lam_init=lam_init),
        grid=(bsz, past // tk),
        in_specs=[row_spec, row_spec,
                  pl.BlockSpec((None, IDX_HEADS * ts, IDX_DIM), lambda b, kj: (b, 0, 0)),
                  pl.BlockSpec((None, IDX_HEADS * ts, 1), lambda b, kj: (b, 0, 0)),
                  cache_spec, heads_spec, heads_spec, heads_spec,
                  pl.BlockSpec((None, None, past, IDX_DIM), lambda b, kj: (l, b, 0, 0)),
                  new_spec, new_spec, new_spec, new_spec,
                  pl.BlockSpec((None, LANES, IDX_DIM), lambda b, kj: (b, 0, 0)),
                  pl.BlockSpec((4, DH_A), lambda b, kj: (0, 0)),
                  pl.BlockSpec((1, DV_A), lambda b, kj: (0, 0))],
        out_specs=[row_spec, row_spec],
        out_shape=[jax.ShapeDtypeStruct((bsz, ts, SEC), jnp.float32),
                   jax.ShapeDtypeStruct((bsz, ts, SEC), jnp.float32)],
        scratch_shapes=[pltpu.VMEM((ra, SEC), jnp.bfloat16),
                        pltpu.VMEM((rb, SEC), jnp.bfloat16),
                        pltpu.VMEM((rb, width), jnp.float32),
                        pltpu.VMEM((ts, width), jnp.float32),
                        pltpu.VMEM((ra, 1), jnp.float32),
                        pltpu.VMEM((ra, 1), jnp.float32),
                        pltpu.VMEM((ra, SEC), jnp.float32),
                        pltpu.VMEM((rb, 1), jnp.float32),
                        pltpu.VMEM((rb, 1), jnp.float32),
                        pltpu.VMEM((rb, SEC), jnp.float32)],
        compiler_params=_params(("parallel", "arbitrary")),
        name="sample_attn",
    )(qa, qb, qi, wi, ckd, cvd, ckb, cvb, cki, *news, lam_p, sg)


def _silu(x):
    return x * jax.nn.sigmoid(x)


def _merge_kernel(oa_ref, ga_ref, ob_ref, gb_ref, mg_ref, wa_ref, wb_ref, u_ref):
    d = u_ref.shape[-1]
    ya = jnp.dot((oa_ref[...] * _silu(ga_ref[...])).astype(jnp.bfloat16), wa_ref[...],
                 preferred_element_type=jnp.float32)
    yb = jnp.dot((ob_ref[...] * _silu(gb_ref[...])).astype(jnp.bfloat16), wb_ref[...],
                 preferred_element_type=jnp.float32)
    mg = mg_ref[...]
    u = jax.nn.sigmoid(mg[:, :d]) * ya + jax.nn.sigmoid(mg[:, d:]) * yb
    u_ref[...] = u.astype(u_ref.dtype)


def _resid_kernel(x_ref, u_ref, wo_ref, g_ref, h_ref, hn_ref):
    h = x_ref[...] + jnp.dot(u_ref[...], wo_ref[...], preferred_element_type=jnp.float32)
    h_ref[...] = h
    ms = jnp.mean(h * h, axis=-1, keepdims=True)
    hn_ref[...] = (h * lax.rsqrt(ms + EPS) * g_ref[...]).astype(hn_ref.dtype)


def _ple_kernel(h_ref, hn_ref, p_ref, wg_ref, wp_ref, o_ref):
    gate = jax.nn.sigmoid(jnp.dot(hn_ref[...], wg_ref[...], preferred_element_type=jnp.float32))
    e = jnp.dot(p_ref[...].astype(jnp.bfloat16), wp_ref[...], preferred_element_type=jnp.float32)
    o_ref[...] = h_ref[...] + gate * e


def _layer_output(x, p, oa, ga, ob, gb, mg, w_ba, w_bb, w_out, ple_g, w_pg, w_ple, tm):
    r, d = x.shape
    wa = oa.shape[1]
    dp = p.shape[1]
    rows = lambda n: pl.BlockSpec((tm, n), lambda i: (i, 0))
    full = lambda a, b: pl.BlockSpec((a, b), lambda i: (0, 0))
    u = pl.pallas_call(
        _merge_kernel,
        grid=(r // tm,),
        in_specs=[rows(wa), rows(wa), rows(wa), rows(wa), rows(2 * d), full(wa, d), full(wa, d)],
        out_specs=rows(d),
        out_shape=jax.ShapeDtypeStruct((r, d), jnp.bfloat16),
        compiler_params=_params(("parallel",)),
        name="merge",
    )(oa, ga, ob, gb, mg, w_ba, w_bb)
    h, hn = pl.pallas_call(
        _resid_kernel,
        grid=(r // tm,),
        in_specs=[rows(d), rows(d), full(d, d), full(1, d)],
        out_specs=[rows(d), rows(d)],
        out_shape=[jax.ShapeDtypeStruct((r, d), jnp.float32), jax.ShapeDtypeStruct((r, d), jnp.bfloat16)],
        compiler_params=_params(("parallel",)),
        name="resid",
    )(x, u, w_out, ple_g)
    return pl.pallas_call(
        _ple_kernel,
        grid=(r // tm,),
        in_specs=[rows(d), rows(d), rows(dp), full(d, d), full(dp, d)],
        out_specs=rows(d),
        out_shape=jax.ShapeDtypeStruct((r, d), jnp.float32),
        compiler_params=_params(("parallel",)),
        name="ple",
    )(h, hn, p, w_pg, w_ple)


def _tile(n, pref):
    t = min(n, pref)
    while n % t:
        t //= 2
    return t


def _layer_inputs(x, pos, lw, tm):
    xn = _rmsnorm(x, lw["ln_g"], tm)
    t64 = _rope_tables(pos, DH_A)
    t128 = _rope_tables(pos, DH_B)
    none = t64
    one = jnp.ones((1, LANES), jnp.float32)
    w = lw["w_sec"]
    pj = functools.partial(_proj, xn, tm=tm)
    (qa16,) = pj(w["qa"], lw["qn_a"], t64, norm=DH_A, rope=DH_A, scale=DH_A ** -0.5 * LOG2E, kinds=["hm16"])
    ka, ka16 = pj(w["ka"], lw["kn_a"], t64, norm=DH_A, rope=DH_A, scale=1.0, kinds=["f32", "hm16"])
    va, va16 = pj(w["va"], one, none, norm=0, rope=0, scale=1.0, kinds=["f32", "hm16"])
    (ga,) = pj(w["ga"], one, none, norm=0, rope=0, scale=1.0, kinds=["f32"])
    (qb16,) = pj(w["qb"], lw["qn_b"], t128, norm=DH_B, rope=DH_B, scale=DH_B ** -0.5 * LOG2E, kinds=["hm16"])
    kb, kb16 = pj(w["kb"], lw["kn_b"], t128, norm=DH_B, rope=DH_B, scale=1.0, kinds=["f32", "hm16"])
    vb, vb16 = pj(w["vb"], one, none, norm=0, rope=0, scale=1.0, kinds=["f32", "hm16"])
    (gb,) = pj(w["gb"], one, none, norm=0, rope=0, scale=1.0, kinds=["f32"])
    (qi16,) = pj(w["qi"], one, t64, norm=0, rope=DH_A, scale=IDX_DIM ** -0.5, kinds=["hm16"])
    (mg,) = pj(w["mg"], one, none, norm=0, rope=0, scale=1.0, kinds=["f32"])
    small, ki2 = _small_proj(xn, w["small"], lw["kn_i"], t64, tm)
    ki = small[:, :IDX_DIM]
    wi = small[:, IDX_DIM:IDX_DIM + IDX_HEADS]
    return dict(qa16=qa16, ka=ka, ka16=ka16, va=va, va16=va16, ga=ga, qb16=qb16, kb=kb, kb16=kb16,
                vb=vb, vb16=vb16, gb=gb, qi16=qi16, mg=mg, ki=ki, ki2=ki2, wi=wi)


def _row_major(hm):
    g, r, n = hm.shape
    return jnp.transpose(hm, (1, 0, 2)).reshape(r, g * n)


def _pad_new(a, bsz, ts):
    a = a.reshape(bsz, ts, a.shape[-1]).astype(jnp.bfloat16)
    return jnp.pad(a, ((0, 0), (0, LANES - ts), (0, 0)))


def kernel(x_prompt, x_sample, p_prompt, p_sample, cache_diff_k, cache_diff_v, cache_dsa_k, cache_dsa_v, cache_idx_k, ln_g, w_in, q_norm_a, k_norm_a, lam_q1, lam_k1, lam_q2, lam_k2, subln_a, q_norm_b, k_norm_b, k_norm_idx, w_branch_a, w_branch_b, w_out, ple_norm, w_ple_gate, w_ple):
    depth = w_in.shape[0]
    bp, t_p, d_model = x_prompt.shape
    bsz, t_s, _ = x_sample.shape
    past = cache_diff_k.shape[2]
    assert bp == 1 and t_s <= LANES and t_p % LANES == 0 and past % LANES == 0
    topk_p = min(TOPK_MAX, t_p // 4)
    topk_s = min(TOPK_MAX, (past + t_s) // 4)
    pos_p = jnp.arange(t_p)
    pos_s = jnp.tile(past + jnp.arange(t_s), bsz)
    r_s = bsz * t_s

    tm_p = _tile(t_p, 512)
    tm_s = _tile(r_s, 512)
    tq_a, tk_a = _tile(t_p, 1024), _tile(t_p, 1024)
    tq_i, tk_i = _tile(t_p, 128), _tile(t_p, 1024)
    tq_b, tk_b = _tile(t_p, 512), _tile(t_p, 1024)
    tk_s = _tile(past, 1024)
    tm_o = _tile(t_p, 256)
    tm_os = _tile(r_s, 256)

    by_head = lambda c: c.reshape(depth, bsz, past * GROUPS, LANES)
    caches = (cache_diff_k.astype(jnp.bfloat16).reshape(depth, bsz, past, SEC), by_head(cache_diff_v),
              by_head(cache_dsa_k), by_head(cache_dsa_v), cache_idx_k)

    sizes = (SEC,) * 9 + (IDX_DIM, IDX_HEADS, 2 * d_model)
    names = ("qa", "ka", "va", "ga", "qb", "kb", "vb", "gb", "qi", "ki", "wi", "mg")
    offs = [0]
    for s in sizes:
        offs.append(offs[-1] + s)

    hp = x_prompt.reshape(t_p, d_model)
    hs = x_sample.reshape(r_s, d_model)
    outs_p = [[] for _ in range(5)]
    outs_s = [[] for _ in range(5)]
    for l in range(depth):
        lam_init = 0.8 - 0.6 * math.exp(-0.3 * l)
        wl = w_in[l].astype(jnp.bfloat16)
        w_sec = {n: wl[:, offs[i]:offs[i + 1]] for i, n in enumerate(names)}
        w_sec["small"] = jnp.pad(jnp.concatenate([w_sec.pop("ki"), w_sec.pop("wi")], axis=1),
                                 ((0, 0), (0, LANES - IDX_DIM - IDX_HEADS)))
        lane_gain = lambda g: jnp.tile(g.astype(jnp.float32), LANES // g.shape[0]).reshape(1, LANES)
        lw = dict(ln_g=ln_g[l], w_sec=w_sec, qn_a=lane_gain(q_norm_a[l]), kn_a=lane_gain(k_norm_a[l]),
                  qn_b=lane_gain(q_norm_b[l]), kn_b=lane_gain(k_norm_b[l]), kn_i=lane_gain(k_norm_idx[l]))
        ok_a = _unshifted_ok(q_norm_a[l], k_norm_a[l], DH_A, DH_A ** -0.5)
        ok_b = _unshifted_ok(q_norm_b[l], k_norm_b[l], DH_B, DH_B ** -0.5)
        lam_p = jnp.stack([lam_q1[l], lam_k1[l], lam_q2[l], lam_k2[l]]).astype(jnp.float32)
        sg = subln_a[l].reshape(1, DV_A).astype(jnp.float32)
        w_o = (w_branch_a[l].astype(jnp.bfloat16), w_branch_b[l].astype(jnp.bfloat16),
               w_out[l].astype(jnp.bfloat16), ple_norm[l].reshape(1, d_model),
               w_ple_gate[l].astype(jnp.bfloat16), w_ple[l].astype(jnp.bfloat16))

        a = _layer_inputs(hp, pos_p, lw, tm_p)
        oa = _diff_attn(a["qa16"], a["ka16"], a["va16"], ok_a, lam_p, sg, lam_init, tq_a, tk_a)
        bias = _index_select(a["qi16"], a["wi"], a["ki2"], topk_p, tq_i, tk_i)
        ob = _dsa_attn(a["qb16"], a["kb16"], a["vb16"], bias, ok_b, tq_b, tk_b)
        hp = _layer_output(hp, p_prompt[l].reshape(t_p, -1), oa, a["ga"], ob, a["gb"], a["mg"], *w_o, tm_o)
        for lst, key in zip(outs_p, ("ka", "va", "kb", "vb", "ki")):
            lst.append(a[key])

        s = _layer_inputs(hs, pos_s, lw, tm_s)
        qa_s = _row_major(s["qa16"]).reshape(bsz, t_s, SEC)
        qb_s = _row_major(s["qb16"]).reshape(bsz, t_s, SEC)
        qi_s = _row_major(s["qi16"]).reshape(bsz, t_s, IDX_HEADS, IDX_DIM)
        qi_s = jnp.transpose(qi_s, (0, 2, 1, 3)).reshape(bsz, IDX_HEADS * t_s, IDX_DIM)
        wi_s = jnp.transpose(s["wi"].reshape(bsz, t_s, IDX_HEADS), (0, 2, 1)).reshape(bsz, IDX_HEADS * t_s, 1)
        news = tuple(_pad_new(s[k], bsz, t_s) for k in ("ka", "va", "kb", "vb", "ki"))
        oa_s, ob_s = _sample_attn(l, qa_s, qb_s, qi_s, wi_s, caches, news, lam_p, sg, lam_init, topk_s, tk_s)
        hs = _layer_output(hs, p_sample[l].reshape(r_s, -1), oa_s.reshape(r_s, SEC), s["ga"],
                           ob_s.reshape(r_s, SEC), s["gb"], s["mg"], *w_o, tm_os)
        for lst, key in zip(outs_s, ("ka", "va", "kb", "vb", "ki")):
            lst.append(s[key])

    def stack(lst, shape):
        return jnp.stack(lst).reshape((depth,) + shape)

    return (hp.reshape(bp, t_p, d_model), hs.reshape(bsz, t_s, d_model),
            stack(outs_p[0], (bp, t_p, H_A, 2, DH_A)), stack(outs_p[1], (bp, t_p, H_A, DV_A)),
            stack(outs_p[2], (bp, t_p, H_B, DH_B)), stack(outs_p[3], (bp, t_p, H_B, DH_B)),
            stack(outs_p[4], (bp, t_p, IDX_DIM)),
            stack(outs_s[0], (bsz, t_s, H_A, 2, DH_A)), stack(outs_s[1], (bsz, t_s, H_A, DV_A)),
            stack(outs_s[2], (bsz, t_s, H_B, DH_B)), stack(outs_s[3], (bsz, t_s, H_B, DH_B)),
            stack(outs_s[4], (bsz, t_s, IDX_DIM)))
```

```python
import functools
import math

import jax
import jax.numpy as jnp
import numpy as np
from jax import lax
from jax.experimental import pallas as pl
from jax.experimental.pallas import tpu as pltpu

CHUNK = 64
ROPE_THETA = 500000.0
EPS = 1e-6
H_A = 8
DH_A = 64
DV_A = 128
H_B = 8
DH_B = 128
IDX_HEADS = 16
IDX_DIM = 64
TOPK_MAX = 256
LANES = 128
GROUPS = 8
SEC = GROUPS * LANES
ROW_GROUP = 128
NEG = -1e30
LOG2E = 1.4426950408889634
MAX_UNSHIFTED = 60.0
MOST_NEG = -3.0e38
VMEM_LIMIT = 56 * 1024 * 1024

_NT = (((1,), (1,)), ((), ()))


def _cdiv(a, b):
    return (a + b - 1) // b


def _params(sem, vmem=VMEM_LIMIT):
    return pltpu.CompilerParams(dimension_semantics=sem, vmem_limit_bytes=vmem)


def _rmsnorm_kernel(x_ref, g_ref, o_ref):
    x = x_ref[...]
    ms = jnp.mean(x * x, axis=-1, keepdims=True)
    o_ref[...] = (x * lax.rsqrt(ms + EPS) * g_ref[...]).astype(o_ref.dtype)


def _rmsnorm(x, g, tm):
    r, d = x.shape
    return pl.pallas_call(
        _rmsnorm_kernel,
        grid=(r // tm,),
        in_specs=[pl.BlockSpec((tm, d), lambda i: (i, 0)),
                  pl.BlockSpec((1, d), lambda i: (0, 0))],
        out_specs=pl.BlockSpec((tm, d), lambda i: (i, 0)),
        out_shape=jax.ShapeDtypeStruct((r, d), jnp.bfloat16),
        compiler_params=_params(("parallel",)),
        name="rmsnorm",
    )(x, g.reshape(1, d))


def _head_norm(z, g, d):
    sq = z * z
    if d == LANES:
        r = lax.rsqrt(jnp.sum(sq, axis=-1, keepdims=True) * (1.0 / d) + EPS)
    else:
        lo = lax.broadcasted_iota(jnp.int32, z.shape, 1) < d
        s_lo = jnp.sum(jnp.where(lo, sq, 0.0), axis=-1, keepdims=True)
        s_hi = jnp.sum(jnp.where(lo, 0.0, sq), axis=-1, keepdims=True)
        r = jnp.where(lo, lax.rsqrt(s_lo * (1.0 / d) + EPS), lax.rsqrt(s_hi * (1.0 / d) + EPS))
    return z * r * g


def _rope(z, c, a, b, half):
    return z * c + pltpu.roll(z, LANES - half, 1) * a + pltpu.roll(z, half, 1) * b


def _proj_kernel(xn_ref, w_ref, g_ref, rc_ref, ra_ref, rb_ref, *out_refs, norm, rope, scale, kinds):
    tm = xn_ref.shape[0]
    chunk = tm // 2 if (norm or rope) and tm % 512 == 0 else tm
    for r0 in range(0, tm, chunk):
        rows = slice(r0, r0 + chunk)
        z = jnp.dot(xn_ref[rows, :], w_ref[...], preferred_element_type=jnp.float32)
        for c in range(GROUPS):
            zc = z[:, c * LANES:(c + 1) * LANES]
            if norm:
                zc = _head_norm(zc, g_ref[...], norm)
            if rope:
                zc = _rope(zc, rc_ref[rows, :], ra_ref[rows, :], rb_ref[rows, :], rope // 8)
            for kind, o_ref in zip(kinds, out_refs):
                if kind == "f32":
                    o_ref[rows, c * LANES:(c + 1) * LANES] = zc
                else:
                    o_ref[c, rows, :] = (zc * scale).astype(o_ref.dtype)


def _proj(xn, w, gain, tables, *, norm, rope, scale, kinds, tm):
    r, d = xn.shape
    nj = w.shape[1] // SEC
    rc, ra, rb = tables
    out_shape, out_specs = [], []
    for kind in kinds:
        if kind == "f32":
            out_shape.append(jax.ShapeDtypeStruct((r, nj * SEC), jnp.float32))
            out_specs.append(pl.BlockSpec((tm, SEC), lambda j, i: (i, j)))
        else:
            assert nj == 1
            out_shape.append(jax.ShapeDtypeStruct((GROUPS, r, LANES), jnp.bfloat16))
            out_specs.append(pl.BlockSpec((GROUPS, tm, LANES), lambda j, i: (0, i, 0)))
    row = pl.BlockSpec((tm, LANES), lambda j, i: (i, 0))
    outs = pl.pallas_call(
        functools.partial(_proj_kernel, norm=norm, rope=rope, scale=scale, kinds=tuple(kinds)),
        grid=(nj, r // tm),
        in_specs=[pl.BlockSpec((tm, d), lambda j, i: (i, 0)),
                  pl.BlockSpec((d, SEC), lambda j, i: (0, j)),
                  pl.BlockSpec((1, LANES), lambda j, i: (0, 0)),
                  row, row, row],
        out_specs=out_specs,
        out_shape=out_shape,
        compiler_params=_params(("parallel", "parallel")),
        name="proj",
    )(xn, w, gain, rc, ra, rb)
    return outs


def _small_proj_kernel(xn_ref, w_ref, g_ref, rc_ref, ra_ref, rb_ref, f_ref, k2_ref, *, wscale):
    z = jnp.dot(xn_ref[...], w_ref[...], preferred_element_type=jnp.float32)
    lo = lax.broadcasted_iota(jnp.int32, z.shape, 1) < IDX_DIM
    ms = jnp.sum(jnp.where(lo, z * z, 0.0), axis=-1, keepdims=True) * (1.0 / IDX_DIM)
    zn = z * lax.rsqrt(ms + EPS) * g_ref[...]
    zr = _rope(zn, rc_ref[...], ra_ref[...], rb_ref[...], IDX_DIM // 8)
    f_ref[...] = jnp.where(lo, zr, z * wscale)
    k2_ref[...] = jnp.where(lo, zr, pltpu.roll(zr, IDX_DIM, 1)).astype(k2_ref.dtype)


def _small_proj(xn, w, gain, tables, tm):
    r, d = xn.shape
    rc, ra, rb = tables
    row = pl.BlockSpec((tm, LANES), lambda i: (i, 0))
    return pl.pallas_call(
        functools.partial(_small_proj_kernel, wscale=IDX_HEADS ** -0.5),
        grid=(r // tm,),
        in_specs=[pl.BlockSpec((tm, d), lambda i: (i, 0)),
                  pl.BlockSpec((d, LANES), lambda i: (0, 0)),
                  pl.BlockSpec((1, LANES), lambda i: (0, 0)),
                  row, row, row],
        out_specs=[row, row],
        out_shape=[jax.ShapeDtypeStruct((r, LANES), jnp.float32),
                   jax.ShapeDtypeStruct((r, LANES), jnp.bfloat16)],
        compiler_params=_params(("parallel",)),
        name="small_proj",
    )(xn, w, gain, rc, ra, rb)


def _rope_tables(pos, d):
    r = d // 4
    h = r // 2
    n = pos.shape[0]
    inv = ROPE_THETA ** (-jnp.arange(0, r, 2, dtype=jnp.float32) / r)
    ang = pos.astype(jnp.float32)[:, None] * inv[None, :]
    cos, sin = jnp.cos(ang), jnp.sin(ang)
    zh = jnp.zeros((n, h), jnp.float32)
    zr = jnp.zeros((n, d - r), jnp.float32)
    c = jnp.concatenate([cos, cos, jnp.ones((n, d - r), jnp.float32)], axis=-1)
    a = jnp.concatenate([-sin, zh, zr], axis=-1)
    b = jnp.concatenate([zh, sin, zr], axis=-1)
    rep = LANES // d
    return tuple(jnp.tile(t, (1, rep)) for t in (c, a, b))


def _online_update(s, v, m_ref, l_ref, acc_ref):
    m_prev = m_ref[...]
    m_new = jnp.maximum(m_prev, jnp.max(s, axis=-1, keepdims=True))
    alpha = jnp.exp2(m_prev - m_new)
    p = jnp.exp2(s - m_new)
    l_ref[...] = alpha * l_ref[...] + jnp.sum(p, axis=-1, keepdims=True)
    acc_ref[...] = alpha * acc_ref[...] + jnp.dot(p.astype(v.dtype), v, preferred_element_type=jnp.float32)
    m_ref[...] = m_new


def _causal_steps(t, tq, tk):
    qs, ks = [], []
    for qi in range(t // tq):
        nk = _cdiv((qi + 1) * tq, tk)
        qs += [qi] * nk
        ks += list(range(nk))
    return np.asarray(qs, np.int32), np.asarray(ks, np.int32)


def _unshifted_ok(gq, gk, d, scale):
    bound = jnp.max(jnp.abs(gq)) * jnp.max(jnp.abs(gk)) * (d * scale * (1.0 + 2.0 ** -6))
    return (bound <= MAX_UNSHIFTED).astype(jnp.int32).reshape(1)


def _lambda(lam_ref, lam_init):
    lp = lam_ref[...]
    s1 = jnp.sum(lp[0:1] * lp[1:2], axis=-1, keepdims=True)
    s2 = jnp.sum(lp[2:3] * lp[3:4], axis=-1, keepdims=True)
    return jnp.exp(s1) - jnp.exp(s2) + lam_init


def _diff_finish(o0, o1, lam, sg, lam_init):
    o = o0 - lam * o1
    ms = jnp.mean(o * o, axis=-1, keepdims=True)
    return o * lax.rsqrt(ms + EPS) * sg * (1.0 - lam_init)


def _chunk_end(pos):
    return (pos // CHUNK + 1) * CHUNK - 1


def _diff_attn_kernel(qt_ref, kt_ref, ok_ref, q_ref, k_ref, v_ref, lam_ref, sg_ref, o_ref,
                      qs_ref, v1_ref, acc_ref, m_ref, l_ref, *, tq, tk, lam_init):
    step_id = pl.program_id(1)
    qi = qt_ref[step_id]
    kj = kt_ref[step_id]
    nk = _cdiv((qi + 1) * tq, tk)
    unshifted = ok_ref[0] == 1
    full = (kj + 1) * tk <= qi * tq + CHUNK
    rows2 = 2 * tq

    @pl.when(kj == 0)
    def _():
        q = q_ref[...]
        lo = lax.broadcasted_iota(jnp.int32, q.shape, 1) < DH_A
        zero = jnp.zeros_like(q)
        qs_ref[0:tq, :] = jnp.where(lo, q, zero)
        qs_ref[tq:rows2, :] = jnp.where(lo, zero, q)
        v1_ref[:, DV_A:] = jnp.ones((tk, LANES), v1_ref.dtype)
        m_ref[...] = jnp.full(m_ref.shape, NEG, jnp.float32)
        l_ref[...] = jnp.zeros(l_ref.shape, jnp.float32)
        acc_ref[...] = jnp.zeros(acc_ref.shape, jnp.float32)

    v1_ref[:, :DV_A] = v_ref[...]

    def visible(r0, nrows):
        row = r0 + lax.broadcasted_iota(jnp.int32, (nrows, tk), 0)
        row = jnp.where(row >= tq, row - tq, row)
        kpos = kj * tk + lax.broadcasted_iota(jnp.int32, (nrows, tk), 1)
        return kpos <= _chunk_end(qi * tq + row)

    def fast_step(masked):
        k = k_ref[...]
        v1 = v1_ref[...]
        for r0 in range(0, rows2, ROW_GROUP):
            rows = slice(r0, r0 + ROW_GROUP)
            s = lax.dot_general(qs_ref[rows, :], k, _NT, preferred_element_type=jnp.float32)
            if masked:
                s = jnp.where(visible(r0, ROW_GROUP), s, NEG)
            p = jnp.exp2(s).astype(v1.dtype)
            acc_ref[rows, :] += jnp.dot(p, v1, preferred_element_type=jnp.float32)

    def exact_step(masked):
        s = lax.dot_general(qs_ref[...], k_ref[...], _NT, preferred_element_type=jnp.float32)
        if masked:
            s = jnp.where(visible(0, rows2), s, NEG)
        _online_update(s, v_ref[...], m_ref, l_ref, acc_ref.at[:, :DV_A])

    for fast, step in ((True, fast_step), (False, exact_step)):
        mode = unshifted if fast else jnp.logical_not(unshifted)

        @pl.when(jnp.logical_and(mode, full))
        def _():
            step(False)

        @pl.when(jnp.logical_and(mode, jnp.logical_not(full)))
        def _():
            step(True)

    @pl.when(kj == nk - 1)
    def _():
        @pl.when(jnp.logical_not(unshifted))
        def _():
            acc_ref[:, DV_A:] = jnp.broadcast_to(l_ref[...], (rows2, LANES))

        o = acc_ref[:, :DV_A] / acc_ref[:, DV_A:]
        o_ref[...] = _diff_finish(o[0:tq], o[tq:rows2], _lambda(lam_ref, lam_init), sg_ref[...], lam_init)


def _diff_attn(q, k, v, ok, lam_p, sg, lam_init, tq, tk):
    _, t, _ = q.shape
    qt, kt = _causal_steps(t, tq, tk)
    kv_spec = pl.BlockSpec((None, tk, LANES), lambda h, s, qt, kt, ok: (h, kt[s], 0))
    const = lambda h, s, qt, kt, ok: (0, 0)
    grid_spec = pltpu.PrefetchScalarGridSpec(
        num_scalar_prefetch=3,
        grid=(H_A, len(qt)),
        in_specs=[pl.BlockSpec((None, tq, LANES), lambda h, s, qt, kt, ok: (h, qt[s], 0)),
                  kv_spec, kv_spec,
                  pl.BlockSpec((4, DH_A), const),
                  pl.BlockSpec((1, DV_A), const)],
        out_specs=pl.BlockSpec((tq, LANES), lambda h, s, qt, kt, ok: (qt[s], h)),
        scratch_shapes=[pltpu.VMEM((2 * tq, LANES), jnp.bfloat16),
                        pltpu.VMEM((tk, DV_A + LANES), jnp.bfloat16),
                        pltpu.VMEM((2 * tq, DV_A + LANES), jnp.float32),
                        pltpu.VMEM((2 * tq, 1), jnp.float32),
                        pltpu.VMEM((2 * tq, 1), jnp.float32)])
    return pl.pallas_call(
        functools.partial(_diff_attn_kernel, tq=tq, tk=tk, lam_init=lam_init),
        grid_spec=grid_spec,
        out_shape=jax.ShapeDtypeStruct((t, H_A * DV_A), jnp.float32),
        compiler_params=_params(("parallel", "arbitrary")),
        name="diff_attn",
    )(jnp.asarray(qt), jnp.asarray(kt), ok, q, k, v, lam_p, sg)


def _topk_threshold(count, lo, hi, all_sel, topk):
    kf = float(topk)
    c_lo = count(lo)
    c_hi = jnp.zeros_like(c_lo)
    done = jnp.where(jnp.logical_or(all_sel, c_lo == kf), 1.0, 0.0)

    def cond(st):
        return jnp.logical_and(st[0] < 2048, st[6] < 0.5)

    def body(st):
        it, lo, hi, c_lo, c_hi, done, _ = st
        mid = 0.5 * lo + 0.5 * hi
        stuck = jnp.logical_or(mid <= lo, mid >= hi)
        frac = (jnp.log(c_lo) - math.log(kf)) / (jnp.log(c_lo) - jnp.log(jnp.maximum(c_hi, 0.5)))
        t = lo + (hi - lo) * frac
        use = jnp.logical_and(lax.rem(it, 6) != 5, jnp.logical_and(t > lo, t < hi))
        t = jnp.where(use, t, mid)
        c = count(t)
        live = jnp.logical_and(done < 0.5, jnp.logical_not(stuck))
        up = jnp.logical_and(live, c >= kf)
        dn = jnp.logical_and(live, c < kf)
        lo, c_lo = jnp.where(up, t, lo), jnp.where(up, c, c_lo)
        hi, c_hi = jnp.where(dn, t, hi), jnp.where(dn, c, c_hi)
        done = jnp.where(jnp.logical_or(c_lo == kf, stuck), 1.0, done)
        return it + 1, lo, hi, c_lo, c_hi, done, jnp.min(done)

    st = (jnp.int32(0), lo, hi, c_lo, c_hi, done, jnp.min(done))
    _, lo, hi, c_lo, c_hi, _, _ = lax.while_loop(cond, body, st)
    thr = jnp.where(all_sel, MOST_NEG, lo)
    tied = jnp.logical_and(jnp.logical_not(all_sel), c_lo > kf)
    return thr, tied, c_hi


def _above(x):
    return x + jnp.abs(x) * 2.0 ** -20 + 1e-30


def _index_kernel(q_ref, w_ref, k_ref, bias_ref, qs_ref, sc_ref, lo_ref, *, tq, tk, t, topk):
    qi = pl.program_id(0)
    nk = _cdiv((qi + 1) * tq, tk)
    nsub = tk // LANES

    for p in range(GROUPS):
        q = q_ref[p]
        lo = lax.broadcasted_iota(jnp.int32, q.shape, 1) < IDX_DIM
        zero = jnp.zeros_like(q)
        qs_ref[(2 * p) * tq:(2 * p + 1) * tq, :] = jnp.where(lo, q, zero)
        qs_ref[(2 * p + 1) * tq:(2 * p + 2) * tq, :] = jnp.where(lo, zero, q)
    w = w_ref[...]

    def score_tile(kj, carry):
        keys = k_ref[pl.ds(pl.multiple_of(kj * tk, tk), tk), :]
        logits = lax.dot_general(qs_ref[...], keys, _NT, preferred_element_type=jnp.float32)
        score = None
        for h in range(IDX_HEADS):
            term = jnp.maximum(logits[h * tq:(h + 1) * tq], 0.0) * w[:, h:h + 1]
            score = term if score is None else score + term
        qpos = qi * tq + lax.broadcasted_iota(jnp.int32, score.shape, 0)
        kpos = kj * tk + lax.broadcasted_iota(jnp.int32, score.shape, 1)
        score = jnp.where(kpos <= _chunk_end(qpos), score, -jnp.inf)
        for c in range(nsub):
            sc_ref[kj * nsub + c] = score[:, c * LANES:(c + 1) * LANES]
        return carry

    lax.fori_loop(0, nk, score_tile, 0)

    def select():
        lane = lax.broadcasted_iota(jnp.int32, (tq, LANES), 1)
        cend = _chunk_end(qi * tq + lax.broadcasted_iota(jnp.int32, (tq, LANES), 0))

        def rows(x):
            return jnp.broadcast_to(x, (tq, LANES))

        def sweep(fn, init):
            def body(j, acc):
                for c in range(nsub):
                    acc = fn(acc, sc_ref[j * nsub + c], j * tk + c * LANES + lane, c % 2)
                return acc
            acc = lax.fori_loop(0, nk // 2, lambda j, a: body(2 * j + 1, body(2 * j, a)), init)
            return lax.fori_loop(nk - nk % 2, nk, body, acc)

        def count(pred):
            acc = sweep(lambda a, s, idx, par: jnp.where(pred(s, idx), a + 1.0, a),
                        jnp.zeros((tq, LANES), jnp.float32))
            return rows(jnp.sum(acc, axis=1, keepdims=True))

        ninf = jnp.full((tq, LANES), -jnp.inf, jnp.float32)

        def group_max(acc, s, idx, par):
            return (jnp.maximum(acc[0], s), acc[1]) if par == 0 else (acc[0], jnp.maximum(acc[1], s))

        m0, m1 = sweep(group_max, (ninf, ninf))
        lo = rows(jnp.min(jnp.minimum(m0, m1), axis=1, keepdims=True))
        row_max = rows(jnp.max(jnp.maximum(m0, m1), axis=1, keepdims=True))
        all_sel = cend + 1 <= topk
        lo_ref[...] = lo

        @pl.when(jnp.max(jnp.where(jnp.logical_and(lo == -jnp.inf, jnp.logical_not(all_sel)), 1.0, 0.0)) > 0.5)
        def _():
            low = sweep(lambda a, s, idx, par: jnp.minimum(a, jnp.where(idx <= cend, s, jnp.inf)),
                        jnp.full((tq, LANES), jnp.inf, jnp.float32))
            lo_ref[...] = rows(jnp.min(low, axis=1, keepdims=True))

        lo = jnp.where(all_sel, 0.0, lo_ref[...])
        thr, tied, n_gt = _topk_threshold(lambda t: count(lambda s, idx: s >= t), lo,
                                          jnp.where(all_sel, 1.0, _above(row_max)), all_sel, topk)
        need = jnp.where(tied, float(topk) - n_gt, jnp.inf)

        def write(select):
            def body(j, carry):
                for c in range(nsub):
                    off = pl.multiple_of(j * tk + c * LANES, LANES)
                    sel = select(sc_ref[j * nsub + c], off + lane)
                    bias_ref[:, pl.ds(off, LANES)] = jnp.where(sel, 0.0, NEG).astype(bias_ref.dtype)
                return carry
            lax.fori_loop(0, nk, body, 0)

        any_tied = jnp.max(jnp.where(tied, 1.0, 0.0)) > 0.5

        @pl.when(jnp.logical_not(any_tied))
        def _():
            write(lambda s, idx: s >= thr)

        @pl.when(any_tied)
        def _():
            nbits = max(1, (t - 1).bit_length())

            def pos_body(b, pos):
                cand = pos | lax.shift_left(jnp.int32(1), jnp.int32(nbits - 1) - b)
                g = count(lambda s, idx: jnp.logical_and(s == thr, idx < cand))
                return jnp.where(g < need, cand, pos)

            last = lax.fori_loop(0, nbits, pos_body, jnp.zeros((tq, LANES), jnp.int32))
            write(lambda s, idx: jnp.logical_or(s > thr, jnp.logical_and(s == thr, idx <= last)))

        def fill_body(j, carry):
            off = pl.multiple_of(j * tk, tk)
            bias_ref[:, pl.ds(off, tk)] = jnp.full((tq, tk), NEG, bias_ref.dtype)
            return carry

        lax.fori_loop(nk, t // tk, fill_body, 0)

    select()


def _index_select(qi16, wi, ki2, topk, tq, tk):
    _, t, _ = qi16.shape
    return pl.pallas_call(
        functools.partial(_index_kernel, tq=tq, tk=tk, t=t, topk=topk),
        grid=(t // tq,),
        in_specs=[pl.BlockSpec((GROUPS, tq, LANES), lambda i: (0, i, 0)),
                  pl.BlockSpec((tq, IDX_HEADS), lambda i: (i, 0)),
                  pl.BlockSpec((t, LANES), lambda i: (0, 0))],
        out_specs=pl.BlockSpec((tq, t), lambda i: (i, 0)),
        out_shape=jax.ShapeDtypeStruct((t, t), jnp.bfloat16),
        scratch_shapes=[pltpu.VMEM((IDX_HEADS * tq, LANES), jnp.bfloat16),
                        pltpu.VMEM((t // LANES, tq, LANES), jnp.float32),
                        pltpu.VMEM((tq, LANES), jnp.float32)],
        compiler_params=_params(("arbitrary",)),
        name="index_select",
    )(qi16, wi, ki2)


def _dsa_attn_kernel(qt_ref, kt_ref, ok_ref, q_ref, k_ref, v_ref, bias_ref, o_ref,
                     v1_ref, acc_ref, m_ref, l_ref, *, tq, tk):
    step_id = pl.program_id(0)
    qi = qt_ref[step_id]
    kj = kt_ref[step_id]
    nk = _cdiv((qi + 1) * tq, tk)
    unshifted = ok_ref[0] == 1

    @pl.when(kj == 0)
    def _():
        v1_ref[:, :, DH_B:] = jnp.ones((H_B, tk, LANES), v1_ref.dtype)
        m_ref[...] = jnp.full(m_ref.shape, NEG, jnp.float32)
        l_ref[...] = jnp.zeros(l_ref.shape, jnp.float32)
        acc_ref[...] = jnp.zeros(acc_ref.shape, jnp.float32)

    @pl.when(unshifted)
    def _():
        v1_ref[:, :, :DH_B] = v_ref[...]
        for r0 in range(0, tq, ROW_GROUP):
            rows = slice(r0, r0 + ROW_GROUP)
            bias = bias_ref[rows, :].astype(jnp.float32)
            for h in range(H_B):
                s = lax.dot_general(q_ref[h, rows, :], k_ref[h], _NT, preferred_element_type=jnp.float32) + bias
                p = jnp.exp2(s).astype(v1_ref.dtype)
                acc_ref[h, rows, :] += jnp.dot(p, v1_ref[h], preferred_element_type=jnp.float32)

    @pl.when(jnp.logical_not(unshifted))
    def _():
        bias = bias_ref[...].astype(jnp.float32)
        for h in range(H_B):
            s = lax.dot_general(q_ref[h], k_ref[h], _NT, preferred_element_type=jnp.float32) + bias
            _online_update(s, v_ref[h], m_ref.at[h], l_ref.at[h], acc_ref.at[h, :, :DH_B])

    @pl.when(kj == nk - 1)
    def _():
        @pl.when(jnp.logical_not(unshifted))
        def _():
            acc_ref[:, :, DH_B:] = jnp.broadcast_to(l_ref[...], (H_B, tq, LANES))

        for h in range(H_B):
            o_ref[:, h * LANES:(h + 1) * LANES] = acc_ref[h, :, :DH_B] / acc_ref[h, :, DH_B:]


def _dsa_attn(q, k, v, bias, ok, tq, tk):
    _, t, _ = q.shape
    qt, kt = _causal_steps(t, tq, tk)
    kv_spec = pl.BlockSpec((H_B, tk, LANES), lambda s, qt, kt, ok: (0, kt[s], 0))
    grid_spec = pltpu.PrefetchScalarGridSpec(
        num_scalar_prefetch=3,
        grid=(len(qt),),
        in_specs=[pl.BlockSpec((H_B, tq, LANES), lambda s, qt, kt, ok: (0, qt[s], 0)),
                  kv_spec, kv_spec,
                  pl.BlockSpec((tq, tk), lambda s, qt, kt, ok: (qt[s], kt[s]))],
        out_specs=pl.BlockSpec((tq, H_B * DH_B), lambda s, qt, kt, ok: (qt[s], 0)),
        scratch_shapes=[pltpu.VMEM((H_B, tk, DH_B + LANES), jnp.bfloat16),
                        pltpu.VMEM((H_B, tq, DH_B + LANES), jnp.float32),
                        pltpu.VMEM((H_B, tq, 1), jnp.float32),
                        pltpu.VMEM((H_B, tq, 1), jnp.float32)])
    return pl.pallas_call(
        functools.partial(_dsa_attn_kernel, tq=tq, tk=tk),
        grid_spec=grid_spec,
        out_shape=jax.ShapeDtypeStruct((t, H_B * DH_B), jnp.float32),
        compiler_params=_params(("arbitrary",)),
        name="dsa_attn",
    )(jnp.asarray(qt), jnp.asarray(kt), ok, q, k, v, bias)


def _sample_kernel(qa_ref, qb_ref, qi_ref, wi_ref, ckd_ref, cvd_ref, ckb_ref, cvb_ref, cki_ref,
                   nkd_ref, nvd_ref, nkb_ref, nvb_ref, nki_ref, lam_ref, sg_ref,
                   oa_ref, ob_ref,
                   qda_ref, qdb_ref, bias_ref, sc_ref, ma_ref, la_ref, acca_ref, mb_ref, lb_ref, accb_ref,
                   *, ts, past, tk, topk, lam_init):
    kj = pl.program_id(1)
    nkt = past // tk
    width = past + LANES
    ra = 2 * H_A * ts
    rb = H_B * ts

    def new_key_valid(rows):
        t_q = lax.broadcasted_iota(jnp.int32, (rows, LANES), 0) % ts
        lane = lax.broadcasted_iota(jnp.int32, (rows, LANES), 1)
        return jnp.logical_and(lane < ts, past + lane <= _chunk_end(past + t_q))

    @pl.when(kj == 0)
    def _():
        qa = qa_ref[...]
        grp = lax.broadcasted_iota(jnp.int32, qa.shape, 1) // DH_A
        for hc in range(2 * H_A):
            qda_ref[hc * ts:(hc + 1) * ts, :] = jnp.where(grp == hc, qa, jnp.zeros_like(qa))
        qb = qb_ref[...]
        grp = lax.broadcasted_iota(jnp.int32, qb.shape, 1) // DH_B
        for h in range(H_B):
            qdb_ref[h * ts:(h + 1) * ts, :] = jnp.where(grp == h, qb, jnp.zeros_like(qb))
        for ref in (ma_ref, mb_ref):
            ref[...] = jnp.full(ref.shape, NEG, jnp.float32)
        for ref in (la_ref, lb_ref, acca_ref, accb_ref):
            ref[...] = jnp.zeros(ref.shape, jnp.float32)

        qi = qi_ref[...]
        w = wi_ref[...]

        def scores(keys16):
            logits = lax.dot_general(qi, keys16, _NT, preferred_element_type=jnp.float32)
            score = None
            for h in range(IDX_HEADS):
                term = jnp.maximum(logits[h * ts:(h + 1) * ts], 0.0) * w[h * ts:(h + 1) * ts]
                score = term if score is None else score + term
            return score

        step = min(past, 1024)
        for c in range(past // step):
            keys16 = cki_ref[c * step:(c + 1) * step, :].astype(jnp.bfloat16)
            sc_ref[:, c * step:(c + 1) * step] = scores(keys16)
        sc_ref[:, past:width] = jnp.where(new_key_valid(ts), scores(nki_ref[...]), -jnp.inf)

        def rows(x):
            return jnp.broadcast_to(x, (ts, LANES))

        def wide(x):
            return x[:, 0:1]

        def count(pred):
            idx = lax.broadcasted_iota(jnp.int32, (ts, width), 1)
            return rows(jnp.sum(jnp.where(pred(sc_ref[...], idx), 1.0, 0.0), axis=1, keepdims=True))

        sc = sc_ref[...]
        m0 = m1 = jnp.full((ts, LANES), -jnp.inf, jnp.float32)
        for c in range(width // LANES):
            slab = sc[:, c * LANES:(c + 1) * LANES]
            m0, m1 = (jnp.maximum(m0, slab), m1) if c % 2 == 0 else (m0, jnp.maximum(m1, slab))
        lo = rows(jnp.min(jnp.minimum(m0, m1), axis=1, keepdims=True))
        row_min = rows(jnp.min(jnp.where(sc == -jnp.inf, jnp.inf, sc), axis=1, keepdims=True))
        lo = jnp.where(lo == -jnp.inf, row_min, lo)
        row_max = rows(jnp.max(jnp.maximum(m0, m1), axis=1, keepdims=True))
        n_valid = count(lambda s, idx: s > -jnp.inf)
        all_sel = n_valid <= float(topk)
        thr, tied, n_gt = _topk_threshold(lambda t: count(lambda s, idx: s >= wide(t)),
                                          jnp.where(all_sel, 0.0, lo), jnp.where(all_sel, 1.0, _above(row_max)),
                                          all_sel, topk)

        def write(select):
            s = sc_ref[...]
            sel = select(s, lax.broadcasted_iota(jnp.int32, (ts, width), 1))
            bias = jnp.where(sel, 0.0, NEG)
            for h in range(H_B):
                bias_ref[h * ts:(h + 1) * ts, :] = bias

        any_tied = jnp.max(jnp.where(tied, 1.0, 0.0)) > 0.5

        @pl.when(jnp.logical_not(any_tied))
        def _():
            write(lambda s, idx: s >= wide(thr))

        @pl.when(any_tied)
        def _():
            need = jnp.where(tied, float(topk) - n_gt, jnp.inf)
            nbits = max(1, (width - 1).bit_length())

            def pos_body(b, pos):
                cand = pos | lax.shift_left(jnp.int32(1), jnp.int32(nbits - 1) - b)
                g = count(lambda s, idx: jnp.logical_and(s == wide(thr), idx < wide(cand)))
                return jnp.where(g < need, cand, pos)

            last = lax.fori_loop(0, nbits, pos_body, jnp.zeros((ts, LANES), jnp.int32))
            write(lambda s, idx: jnp.logical_or(s > wide(thr), jnp.logical_and(s == wide(thr), idx <= wide(last))))

    def head_rows(ref):
        return jnp.concatenate([ref[pl.ds(h, tk, stride=GROUPS), :].astype(jnp.bfloat16) for h in range(GROUPS)],
                               axis=1)

    sa = lax.dot_general(qda_ref[...], ckd_ref[...], _NT, preferred_element_type=jnp.float32)
    _online_update(sa, head_rows(cvd_ref), ma_ref, la_ref, acca_ref)
    sb = lax.dot_general(qdb_ref[...], head_rows(ckb_ref), _NT, preferred_element_type=jnp.float32)
    sb = sb + bias_ref[:, pl.ds(pl.multiple_of(kj * tk, tk), tk)]
    _online_update(sb, head_rows(cvb_ref), mb_ref, lb_ref, accb_ref)

    @pl.when(kj == nkt - 1)
    def _():
        sa = lax.dot_general(qda_ref[...], nkd_ref[...], _NT, preferred_element_type=jnp.float32)
        sa = jnp.where(new_key_valid(ra), sa, NEG)
        _online_update(sa, nvd_ref[...], ma_ref, la_ref, acca_ref)
        sb = lax.dot_general(qdb_ref[...], nkb_ref[...], _NT, preferred_element_type=jnp.float32)
        sb = sb + bias_ref[:, past:width]
        _online_update(sb, nvb_ref[...], mb_ref, lb_ref, accb_ref)

        lam = _lambda(lam_ref, lam_init)
        oa = acca_ref[...] / la_ref[...]
        ob = accb_ref[...] / lb_ref[...]
        for h in range(H_A):
            cols = slice(h * DV_A, (h + 1) * DV_A)
            o0 = oa[(2 * h) * ts:(2 * h + 1) * ts, cols]
            o1 = oa[(2 * h + 1) * ts:(2 * h + 2) * ts, cols]
            oa_ref[:, cols] = _diff_finish(o0, o1, lam, sg_ref[...], lam_init)
        for h in range(H_B):
            cols = slice(h * DH_B, (h + 1) * DH_B)
            ob_ref[:, cols] = ob[h * ts:(h + 1) * ts, cols]


def _sample_attn(l, qa, qb, qi, wi, caches, news, lam_p, sg, lam_init, topk, tk):
    bsz, ts, _ = qa.shape
    ckd, cvd, ckb, cvb, cki = caches
    past = cki.shape[2]
    width = past + LANES
    ra, rb = 2 * H_A * ts, H_B * ts
    cache_spec = pl.BlockSpec((None, None, tk, SEC), lambda b, kj: (l, b, kj, 0))
    heads_spec = pl.BlockSpec((None, None, tk * GROUPS, LANES), lambda b, kj: (l, b, kj, 0))
    new_spec = pl.BlockSpec((None, LANES, SEC), lambda b, kj: (b, 0, 0))
    row_spec = pl.BlockSpec((None, ts, SEC), lambda b, kj: (b, 0, 0))
    return pl.pallas_call(
        functools.partial(_sample_kernel, ts=ts, past=past, tk=tk, topk=topk, lam_init=lam_init),
        grid=(bsz, past // tk),
        in_specs=[row_spec, row_spec,
                  pl.BlockSpec((None, IDX_HEADS * ts, IDX_DIM), lambda b, kj: (b, 0, 0)),
                  pl.BlockSpec((None, IDX_HEADS * ts, 1), lambda b, kj: (b, 0, 0)),
                  cache_spec, heads_spec, heads_spec, heads_spec,
                  pl.BlockSpec((None, None, past, IDX_DIM), lambda b, kj: (l, b, 0, 0)),
                  new_spec, new_spec, new_spec, new_spec,
                  pl.BlockSpec((None, LANES, IDX_DIM), lambda b, kj: (b, 0, 0)),
                  pl.BlockSpec((4, DH_A), lambda b, kj: (0, 0)),
                  pl.BlockSpec((1, DV_A), lambda b, kj: (0, 0))],
        out_specs=[row_spec, row_spec],
        out_shape=[jax.ShapeDtypeStruct((bsz, ts, SEC), jnp.float32),
                   jax.ShapeDtypeStruct((bsz, ts, SEC), jnp.float32)],
        scratch_shapes=[pltpu.VMEM((ra, SEC), jnp.bfloat16),
                        pltpu.VMEM((rb, SEC), jnp.bfloat16),
                        pltpu.VMEM((rb, width), jnp.float32),
                        pltpu.VMEM((ts, width), jnp.float32),
                        pltpu.VMEM((ra, 1), jnp.float32),
                        pltpu.VMEM((ra, 1), jnp.float32),
                        pltpu.VMEM((ra, SEC), jnp.float32),
                        pltpu.VMEM((rb, 1), jnp.float32),
                        pltpu.VMEM((rb, 1), jnp.float32),
                        pltpu.VMEM((rb, SEC), jnp.float32)],
        compiler_params=_params(("parallel", "arbitrary")),
        name="sample_attn",
    )(qa, qb, qi, wi, ckd, cvd, ckb, cvb, cki, *news, lam_p, sg)


def _silu(x):
    return x * jax.nn.sigmoid(x)


def _merge_kernel(oa_ref, ga_ref, ob_ref, gb_ref, mg_ref, wa_ref, wb_ref, u_ref):
    d = u_ref.shape[-1]
    ya = jnp.dot((oa_ref[...] * _silu(ga_ref[...])).astype(jnp.bfloat16), wa_ref[...],
                 preferred_element_type=jnp.float32)
    yb = jnp.dot((ob_ref[...] * _silu(gb_ref[...])).astype(jnp.bfloat16), wb_ref[...],
                 preferred_element_type=jnp.float32)
    mg = mg_ref[...]
    u = jax.nn.sigmoid(mg[:, :d]) * ya + jax.nn.sigmoid(mg[:, d:]) * yb
    u_ref[...] = u.astype(u_ref.dtype)


def _resid_kernel(x_ref, u_ref, wo_ref, g_ref, h_ref, hn_ref):
    h = x_ref[...] + jnp.dot(u_ref[...], wo_ref[...], preferred_element_type=jnp.float32)
    h_ref[...] = h
    ms = jnp.mean(h * h, axis=-1, keepdims=True)
    hn_ref[...] = (h * lax.rsqrt(ms + EPS) * g_ref[...]).astype(hn_ref.dtype)


def _ple_kernel(h_ref, hn_ref, p_ref, wg_ref, wp_ref, o_ref):
    gate = jax.nn.sigmoid(jnp.dot(hn_ref[...], wg_ref[...], preferred_element_type=jnp.float32))
    e = jnp.dot(p_ref[...].astype(jnp.bfloat16), wp_ref[...], preferred_element_type=jnp.float32)
    o_ref[...] = h_ref[...] + gate * e


def _layer_output(x, p, oa, ga, ob, gb, mg, w_ba, w_bb, w_out, ple_g, w_pg, w_ple, tm):
    r, d = x.shape
    wa = oa.shape[1]
    dp = p.shape[1]
    rows = lambda n: pl.BlockSpec((tm, n), lambda i: (i, 0))
    full = lambda a, b: pl.BlockSpec((a, b), lambda i: (0, 0))
    u = pl.pallas_call(
        _merge_kernel,
        grid=(r // tm,),
        in_specs=[rows(wa), rows(wa), rows(wa), rows(wa), rows(2 * d), full(wa, d), full(wa, d)],
        out_specs=rows(d),
        out_shape=jax.ShapeDtypeStruct((r, d), jnp.bfloat16),
        compiler_params=_params(("parallel",)),
        name="merge",
    )(oa, ga, ob, gb, mg, w_ba, w_bb)
    h, hn = pl.pallas_call(
        _resid_kernel,
        grid=(r // tm,),
        in_specs=[rows(d), rows(d), full(d, d), full(1, d)],
        out_specs=[rows(d), rows(d)],
        out_shape=[jax.ShapeDtypeStruct((r, d), jnp.float32), jax.ShapeDtypeStruct((r, d), jnp.bfloat16)],
        compiler_params=_params(("parallel",)),
        name="resid",
    )(x, u, w_out, ple_g)
    return pl.pallas_call(
        _ple_kernel,
        grid=(r // tm,),
        in_specs=[rows(d), rows(d), rows(dp), full(d, d), full(dp, d)],
        out_specs=rows(d),
        out_shape=jax.ShapeDtypeStruct((r, d), jnp.float32),
        compiler_params=_params(("parallel",)),
        name="ple",
    )(h, hn, p, w_pg, w_ple)


def _tile(n, pref):
    t = min(n, pref)
    while n % t:
        t //= 2
    return t


def _layer_inputs(x, pos, lw, tm):
    xn = _rmsnorm(x, lw["ln_g"], tm)
    t64 = _rope_tables(pos, DH_A)
    t128 = _rope_tables(pos, DH_B)
    none = t64
    one = jnp.ones((1, LANES), jnp.float32)
    w = lw["w_sec"]
    pj = functools.partial(_proj, xn, tm=tm)
    (qa16,) = pj(w["qa"], lw["qn_a"], t64, norm=DH_A, rope=DH_A, scale=DH_A ** -0.5 * LOG2E, kinds=["hm16"])
    ka, ka16 = pj(w["ka"], lw["kn_a"], t64, norm=DH_A, rope=DH_A, scale=1.0, kinds=["f32", "hm16"])
    va, va16 = pj(w["va"], one, none, norm=0, rope=0, scale=1.0, kinds=["f32", "hm16"])
    (ga,) = pj(w["ga"], one, none, norm=0, rope=0, scale=1.0, kinds=["f32"])
    (qb16,) = pj(w["qb"], lw["qn_b"], t128, norm=DH_B, rope=DH_B, scale=DH_B ** -0.5 * LOG2E, kinds=["hm16"])
    kb, kb16 = pj(w["kb"], lw["kn_b"], t128, norm=DH_B, rope=DH_B, scale=1.0, kinds=["f32", "hm16"])
    vb, vb16 = pj(w["vb"], one, none, norm=0, rope=0, scale=1.0, kinds=["f32", "hm16"])
    (gb,) = pj(w["gb"], one, none, norm=0, rope=0, scale=1.0, kinds=["f32"])
    (qi16,) = pj(w["qi"], one, t64, norm=0, rope=DH_A, scale=IDX_DIM ** -0.5, kinds=["hm16"])
    (mg,) = pj(w["mg"], one, none, norm=0, rope=0, scale=1.0, kinds=["f32"])
    small, ki2 = _small_proj(xn, w["small"], lw["kn_i"], t64, tm)
    ki = small[:, :IDX_DIM]
    wi = small[:, IDX_DIM:IDX_DIM + IDX_HEADS]
    return dict(qa16=qa16, ka=ka, ka16=ka16, va=va, va16=va16, ga=ga, qb16=qb16, kb=kb, kb16=kb16,
                vb=vb, vb16=vb16, gb=gb, qi16=qi16, mg=mg, ki=ki, ki2=ki2, wi=wi)


def _row_major(hm):
    g, r, n = hm.shape
    return jnp.transpose(hm, (1, 0, 2)).reshape(r, g * n)


def _pad_new(a, bsz, ts):
    a = a.reshape(bsz, ts, a.shape[-1]).astype(jnp.bfloat16)
    return jnp.pad(a, ((0, 0), (0, LANES - ts), (0, 0)))


def kernel(x_prompt, x_sample, p_prompt, p_sample, cache_diff_k, cache_diff_v, cache_dsa_k, cache_dsa_v, cache_idx_k, ln_g, w_in, q_norm_a, k_norm_a, lam_q1, lam_k1, lam_q2, lam_k2, subln_a, q_norm_b, k_norm_b, k_norm_idx, w_branch_a, w_branch_b, w_out, ple_norm, w_ple_gate, w_ple):
    depth = w_in.shape[0]
    bp, t_p, d_model = x_prompt.shape
    bsz, t_s, _ = x_sample.shape
    past = cache_diff_k.shape[2]
    assert bp == 1 and t_s <= LANES and t_p % LANES == 0 and past % LANES == 0
    topk_p = min(TOPK_MAX, t_p // 4)
    topk_s = min(TOPK_MAX, (past + t_s) // 4)
    pos_p = jnp.arange(t_p)
    pos_s = jnp.tile(past + jnp.arange(t_s), bsz)
    r_s = bsz * t_s

    tm_p = _tile(t_p, 512)
    tm_s = _tile(r_s, 512)
    tq_a, tk_a = _tile(t_p, 1024), _tile(t_p, 1024)
    tq_i, tk_i = _tile(t_p, 128), _tile(t_p, 1024)
    tq_b, tk_b = _tile(t_p, 512), _tile(t_p, 1024)
    tk_s = _tile(past, 1024)
    tm_o = _tile(t_p, 256)
    tm_os = _tile(r_s, 256)

    by_head = lambda c: c.reshape(depth, bsz, past * GROUPS, LANES)
    caches = (cache_diff_k.astype(jnp.bfloat16).reshape(depth, bsz, past, SEC), by_head(cache_diff_v),
              by_head(cache_dsa_k), by_head(cache_dsa_v), cache_idx_k)

    sizes = (SEC,) * 9 + (IDX_DIM, IDX_HEADS, 2 * d_model)
    names = ("qa", "ka", "va", "ga", "qb", "kb", "vb", "gb", "qi", "ki", "wi", "mg")
    offs = [0]
    for s in sizes:
        offs.append(offs[-1] + s)

    hp = x_prompt.reshape(t_p, d_model)
    hs = x_sample.reshape(r_s, d_model)
    outs_p = [[] for _ in range(5)]
    outs_s = [[] for _ in range(5)]
    for l in range(depth):
        lam_init = 0.8 - 0.6 * math.exp(-0.3 * l)
        wl = w_in[l].astype(jnp.bfloat16)
        w_sec = {n: wl[:, offs[i]:offs[i + 1]] for i, n in enumerate(names)}
        w_sec["small"] = jnp.pad(jnp.concatenate([w_sec.pop("ki"), w_sec.pop("wi")], axis=1),
                                 ((0, 0), (0, LANES - IDX_DIM - IDX_HEADS)))
        lane_gain = lambda g: jnp.tile(g.astype(jnp.float32), LANES // g.shape[0]).reshape(1, LANES)
        lw = dict(ln_g=ln_g[l], w_sec=w_sec, qn_a=lane_gain(q_norm_a[l]), kn_a=lane_gain(k_norm_a[l]),
                  qn_b=lane_gain(q_norm_b[l]), kn_b=lane_gain(k_norm_b[l]), kn_i=lane_gain(k_norm_idx[l]))
        ok_a = _unshifted_ok(q_norm_a[l], k_norm_a[l], DH_A, DH_A ** -0.5)
        ok_b = _unshifted_ok(q_norm_b[l], k_norm_b[l], DH_B, DH_B ** -0.5)
        lam_p = jnp.stack([lam_q1[l], lam_k1[l], lam_q2[l], lam_k2[l]]).astype(jnp.float32)
        sg = subln_a[l].reshape(1, DV_A).astype(jnp.float32)
        w_o = (w_branch_a[l].astype(jnp.bfloat16), w_branch_b[l].astype(jnp.bfloat16),
               w_out[l].astype(jnp.bfloat16), ple_norm[l].reshape(1, d_model),
               w_ple_gate[l].astype(jnp.bfloat16), w_ple[l].astype(jnp.bfloat16))

        a = _layer_inputs(hp, pos_p, lw, tm_p)
        oa = _diff_attn(a["qa16"], a["ka16"], a["va16"], ok_a, lam_p, sg, lam_init, tq_a, tk_a)
        bias = _index_select(a["qi16"], a["wi"], a["ki2"], topk_p, tq_i, tk_i)
        ob = _dsa_attn(a["qb16"], a["kb16"], a["vb16"], bias, ok_b, tq_b, tk_b)
        hp = _layer_output(hp, p_prompt[l].reshape(t_p, -1), oa, a["ga"], ob, a["gb"], a["mg"], *w_o, tm_o)
        for lst, key in zip(outs_p, ("ka", "va", "kb", "vb", "ki")):
            lst.append(a[key])

        s = _layer_inputs(hs, pos_s, lw, tm_s)
        qa_s = _row_major(s["qa16"]).reshape(bsz, t_s, SEC)
        qb_s = _row_major(s["qb16"]).reshape(bsz, t_s, SEC)
        qi_s = _row_major(s["qi16"]).reshape(bsz, t_s, IDX_HEADS, IDX_DIM)
        qi_s = jnp.transpose(qi_s, (0, 2, 1, 3)).reshape(bsz, IDX_HEADS * t_s, IDX_DIM)
        wi_s = jnp.transpose(s["wi"].reshape(bsz, t_s, IDX_HEADS), (0, 2, 1)).reshape(bsz, IDX_HEADS * t_s, 1)
        news = tuple(_pad_new(s[k], bsz, t_s) for k in ("ka", "va", "kb", "vb", "ki"))
        oa_s, ob_s = _sample_attn(l, qa_s, qb_s, qi_s, wi_s, caches, news, lam_p, sg, lam_init, topk_s, tk_s)
        hs = _layer_output(hs, p_sample[l].reshape(r_s, -1), oa_s.reshape(r_s, SEC), s["ga"],
                           ob_s.reshape(r_s, SEC), s["gb"], s["mg"], *w_o, tm_os)
        for lst, key in zip(outs_s, ("ka", "va", "kb", "vb", "ki")):
            lst.append(s[key])

    def stack(lst, shape):
        return jnp.stack(lst).reshape((depth,) + shape)

    return (hp.reshape(bp, t_p, d_model), hs.reshape(bsz, t_s, d_model),
            stack(outs_p[0], (bp, t_p, H_A, 2, DH_A)), stack(outs_p[1], (bp, t_p, H_A, DV_A)),
            stack(outs_p[2], (bp, t_p, H_B, DH_B)), stack(outs_p[3], (bp, t_p, H_B, DH_B)),
            stack(outs_p[4], (bp, t_p, IDX_DIM)),
            stack(outs_s[0], (bsz, t_s, H_A, 2, DH_A)), stack(outs_s[1], (bsz, t_s, H_A, DV_A)),
            stack(outs_s[2], (bsz, t_s, H_B, DH_B)), stack(outs_s[3], (bsz, t_s, H_B, DH_B)),
            stack(outs_s[4], (bsz, t_s, IDX_DIM)))
```
